```python
import jax, jax.numpy as jnp
from jax import lax
import numpy as np

D_MODEL = 2048
BATCH = 8
SEQ = 4096
DEPTH = 1
DEC_BATCH = 8
DEC_SEQ = 64
PAST_LEN = 2048

CHUNK = 64
SSD_EXPAND = 2
SSD_D_INNER = SSD_EXPAND * D_MODEL
SSD_HEAD_DIM = 64
SSD_HEADS = SSD_D_INNER // SSD_HEAD_DIM
SSD_GROUPS = 8
SSD_HPG = SSD_HEADS // SSD_GROUPS
SSD_D_STATE = 128
SSD_CONV_W = 4
SSD_CONV_DIM = SSD_D_INNER + 2 * SSD_GROUPS * SSD_D_STATE
SSD_NORM_EPS = 1e-5
GLA_HEADS = 4
GLA_KEY_DIM = D_MODEL // 2
GLA_VAL_DIM = D_MODEL
GLA_HEAD_K = GLA_KEY_DIM // GLA_HEADS
GLA_HEAD_V = GLA_VAL_DIM // GLA_HEADS
GLA_GATE_RANK = 16
GLA_GATE_NORMALIZER = 16.0
N_BRANCHES = 2
NORM_EPS = 1e-6
IN_SPLITS = (SSD_D_INNER, SSD_CONV_DIM, SSD_HEADS, GLA_KEY_DIM, GLA_KEY_DIM, GLA_VAL_DIM, GLA_VAL_DIM, GLA_GATE_RANK, N_BRANCHES * D_MODEL)
IN_COLS = sum(IN_SPLITS)

kernel_name = 'hybrid_ssd_gla_stream_step'


def rmsnorm(x, gain, eps=NORM_EPS):
    x32 = x.astype(jnp.float32)
    y = x32 * lax.rsqrt(jnp.mean(x32 * x32, axis=-1, keepdims=True) + eps)
    return (y * gain.astype(jnp.float32)).astype(x.dtype)


def split_points(sizes):
    return np.cumsum(np.array(sizes))[:-1].tolist()


def causal_mask(length):
    return jnp.tril(jnp.ones((length, length), dtype=bool))


def to_chunks(a, block):
    b, t = a.shape[:2]
    return jnp.moveaxis(a.reshape((b, t // block, block) + a.shape[2:]), 1, 0)


def from_chunks(a):
    nc, b, block = a.shape[:3]
    return jnp.moveaxis(a, 0, 1).reshape((b, nc * block) + a.shape[3:])


def ssd_block(state, xs, dt, bm, cm, a_neg, d_skip):
    length = xs.shape[1]
    s = jnp.cumsum(dt * a_neg, axis=1)
    mask = causal_mask(length)[None, :, :, None, None]
    ldec = jnp.exp(jnp.where(mask, s[:, :, None] - s[:, None, :], -jnp.inf))
    xdt = xs * dt[..., None]
    cb = jnp.einsum('bign,bjgn->bijg', cm, bm)
    y = jnp.einsum('bijg,bijgr,bjgrp->bigrp', cb, ldec, xdt)
    y = y + jnp.einsum('bign,bgrpn->bigrp', cm, state) * jnp.exp(s)[..., None]
    y = y + d_skip[..., None] * xs
    to_end = jnp.exp(s[:, -1:] - s)
    new_state = jnp.exp(s[:, -1])[..., None, None] * state + jnp.einsum('bjgn,bjgr,bjgrp->bgrpn', bm, to_end, xdt)
    return new_state.astype(state.dtype), y.astype(xs.dtype)


def gla_block(state, q, k, v, glog):
    length = q.shape[1]
    bc = jnp.cumsum(glog, axis=1)
    mask = causal_mask(length)[None, :, :, None, None]
    dec = jnp.exp(jnp.where(mask, bc[:, :, None] - bc[:, None, :], -jnp.inf))
    attn = jnp.einsum('bihk,bjhk,bijhk->bhij', q, k, dec)
    o = jnp.einsum('bhij,bjhv->bihv', attn, v)
    o = o + jnp.einsum('bihk,bhkv->bihv', q * jnp.exp(bc), state)
    new_state = jnp.exp(bc[:, -1])[..., None] * state + jnp.einsum('bjhk,bjhv->bhkv', k * jnp.exp(bc[:, -1:] - bc), v)
    return new_state.astype(state.dtype), o.astype(q.dtype)


def hybrid_layer(x, conv_state, ssd_state, gla_state, norm_pre_gain, w_in, conv_w, conv_b, dt_bias, a_log, d_skip,
                 ssd_norm_gain, gla_gk_w, gla_gk_b, gla_norm_gain, w_branch_ssd, w_branch_gla, w_out, norm_post_gain):
    b, t, _ = x.shape
    h = rmsnorm(x, norm_pre_gain)
    proj = h @ w_in
    z_a, xbc, dt_raw, q, k, v, g_b, gk_low, merge = jnp.split(proj, split_points(IN_SPLITS), axis=-1)

    xbc_full = jnp.concatenate([conv_state.astype(xbc.dtype), xbc], axis=1)
    new_conv_state = xbc_full[:, t:]
    conv = conv_b + sum(xbc_full[:, i:i + t] * conv_w[i] for i in range(SSD_CONV_W))
    xbc_c = jax.nn.silu(conv)
    xs, bm, cm = jnp.split(xbc_c, split_points((SSD_D_INNER, SSD_GROUPS * SSD_D_STATE, SSD_GROUPS * SSD_D_STATE)), axis=-1)
    xs = xs.reshape(b, t, SSD_GROUPS, SSD_HPG, SSD_HEAD_DIM)
    bm = bm.reshape(b, t, SSD_GROUPS, SSD_D_STATE)
    cm = cm.reshape(b, t, SSD_GROUPS, SSD_D_STATE)
    dt = jax.nn.softplus(dt_raw.astype(jnp.float32) + dt_bias.astype(jnp.float32)).reshape(b, t, SSD_GROUPS, SSD_HPG)
    a_neg = -jnp.exp(a_log.astype(jnp.float32)).reshape(SSD_GROUPS, SSD_HPG)
    d_sk = d_skip.reshape(SSD_GROUPS, SSD_HPG)

    q = q.reshape(b, t, GLA_HEADS, GLA_HEAD_K) * (GLA_HEAD_K ** -0.5)
    k = k.reshape(b, t, GLA_HEADS, GLA_HEAD_K)
    v = v.reshape(b, t, GLA_HEADS, GLA_HEAD_V)
    glog = jax.nn.log_sigmoid((gk_low @ gla_gk_w + gla_gk_b).astype(jnp.float32)) / GLA_GATE_NORMALIZER
    glog = glog.reshape(b, t, GLA_HEADS, GLA_HEAD_K)

    block = min(CHUNK, t)

    def step(carry, inp):
        s_ssd, s_gla = carry
        xs_c, dt_c, bm_c, cm_c, q_c, k_c, v_c, g_c = inp
        s_ssd, y_ssd_c = ssd_block(s_ssd, xs_c, dt_c, bm_c, cm_c, a_neg, d_sk)
        s_gla, y_gla_c = gla_block(s_gla, q_c, k_c, v_c, g_c)
        return (s_ssd, s_gla), (y_ssd_c, y_gla_c)

    init = (ssd_state.reshape(b, SSD_GROUPS, SSD_HPG, SSD_HEAD_DIM, SSD_D_STATE), gla_state)
    chunks = [to_chunks(a, block) for a in (xs, dt, bm, cm, q, k, v, glog)]
    (new_ssd, new_gla), (y_ssd, y_gla) = lax.scan(step, init, chunks)

    y_ssd = from_chunks(y_ssd).reshape(b, t, SSD_D_INNER) * jax.nn.silu(z_a)
    y_ssd = rmsnorm(y_ssd.reshape(b, t, SSD_GROUPS, -1), ssd_norm_gain.reshape(SSD_GROUPS, -1), SSD_NORM_EPS).reshape(b, t, SSD_D_INNER)
    y_gla = rmsnorm(from_chunks(y_gla), gla_norm_gain) * jax.nn.silu(g_b.reshape(b, t, GLA_HEADS, GLA_HEAD_V))
    y_gla = y_gla.reshape(b, t, GLA_VAL_DIM)

    gate_ssd, gate_gla = jnp.split(jax.nn.sigmoid(merge), 2, axis=-1)
    mixed = gate_ssd * (y_ssd @ w_branch_ssd) + gate_gla * (y_gla @ w_branch_gla)
    y = x + rmsnorm(mixed @ w_out, norm_post_gain)
    return y, new_conv_state, new_ssd.reshape(b, SSD_HEADS, SSD_HEAD_DIM, SSD_D_STATE), new_gla


def setup_inputs(seed: int = 0) -> dict:
    key = jax.random.key(seed)
    ks = jax.random.split(key, 24)
    f32 = jnp.float32
    nrm = lambda k_, shape, scale: jax.random.normal(k_, shape, f32) * scale
    dt0 = jnp.exp(jax.random.uniform(ks[7], (DEPTH, SSD_HEADS), f32, np.log(1e-3), np.log(1e-1)))
    return {
        'x_prompt': nrm(ks[0], (BATCH, SEQ, D_MODEL), 1.0),
        'x_sample': nrm(ks[1], (DEC_BATCH, DEC_SEQ, D_MODEL), 1.0),
        'state_conv_ssd': nrm(ks[2], (DEPTH, DEC_BATCH, SSD_CONV_W - 1, SSD_CONV_DIM), 1.0),
        'state_ssd': nrm(ks[3], (DEPTH, DEC_BATCH, SSD_HEADS, SSD_HEAD_DIM, SSD_D_STATE), 0.3),
        'state_gla': nrm(ks[4], (DEPTH, DEC_BATCH, GLA_HEADS, GLA_HEAD_K, GLA_HEAD_V), 0.5),
        'norm_pre_gain': 1.0 + nrm(ks[5], (DEPTH, D_MODEL), 0.05),
        'w_in': nrm(ks[6], (DEPTH, D_MODEL, IN_COLS), D_MODEL ** -0.5),
        'conv_w': nrm(ks[8], (DEPTH, SSD_CONV_W, SSD_CONV_DIM), SSD_CONV_W ** -0.5),
        'conv_b': nrm(ks[9], (DEPTH, SSD_CONV_DIM), 0.01),
        'dt_bias': dt0 + jnp.log(-jnp.expm1(-dt0)),
        'a_log': jnp.log(jax.random.uniform(ks[10], (DEPTH, SSD_HEADS), f32, 1.0, 16.0)),
        'd_skip': 1.0 + nrm(ks[11], (DEPTH, SSD_HEADS), 0.1),
        'ssd_norm_gain': 1.0 + nrm(ks[12], (DEPTH, SSD_D_INNER), 0.05),
        'gla_gk_w': nrm(ks[13], (DEPTH, GLA_GATE_RANK, GLA_KEY_DIM), GLA_GATE_RANK ** -0.5),
        'gla_gk_b': nrm(ks[14], (DEPTH, GLA_KEY_DIM), 0.1),
        'gla_norm_gain': 1.0 + nrm(ks[15], (DEPTH, GLA_HEAD_V), 0.05),
        'w_branch_ssd': nrm(ks[16], (DEPTH, SSD_D_INNER, D_MODEL), SSD_D_INNER ** -0.5),
        'w_branch_gla': nrm(ks[17], (DEPTH, GLA_VAL_DIM, D_MODEL), GLA_VAL_DIM ** -0.5),
        'w_out': nrm(ks[18], (DEPTH, D_MODEL, D_MODEL), D_MODEL ** -0.5),
        'norm_post_gain': 1.0 + nrm(ks[19], (DEPTH, D_MODEL), 0.05),
    }


def reference(x_prompt, x_sample, state_conv_ssd, state_ssd, state_gla, norm_pre_gain, w_in, conv_w, conv_b, dt_bias,
              a_log, d_skip, ssd_norm_gain, gla_gk_w, gla_gk_b, gla_norm_gain, w_branch_ssd, w_branch_gla, w_out,
              norm_post_gain):
    bp = x_prompt.shape[0]
    dtp = x_prompt.dtype
    yp, ys = x_prompt, x_sample
    conv_p, ssd_p, gla_p, conv_s, ssd_s, gla_s = [], [], [], [], [], []
    for layer in range(DEPTH):
        weights = (norm_pre_gain[layer], w_in[layer], conv_w[layer], conv_b[layer], dt_bias[layer], a_log[layer],
                   d_skip[layer], ssd_norm_gain[layer], gla_gk_w[layer], gla_gk_b[layer], gla_norm_gain[layer],
                   w_branch_ssd[layer], w_branch_gla[layer], w_out[layer], norm_post_gain[layer])
        zc = jnp.zeros((bp, SSD_CONV_W - 1, SSD_CONV_DIM), dtp)
        zs = jnp.zeros((bp, SSD_HEADS, SSD_HEAD_DIM, SSD_D_STATE), dtp)
        zg = jnp.zeros((bp, GLA_HEADS, GLA_HEAD_K, GLA_HEAD_V), dtp)
        yp, c_p, s_p, g_p = hybrid_layer(yp, zc, zs, zg, *weights)
        ys, c_s, s_s, g_s = hybrid_layer(ys, state_conv_ssd[layer], state_ssd[layer], state_gla[layer], *weights)
        conv_p.append(c_p); ssd_p.append(s_p); gla_p.append(g_p)
        conv_s.append(c_s); ssd_s.append(s_s); gla_s.append(g_s)
    return (yp, ys, jnp.stack(conv_p), jnp.stack(ssd_p), jnp.stack(gla_p), jnp.stack(conv_s), jnp.stack(ssd_s), jnp.stack(gla_s))
```

```python
import functools

import numpy as np
import jax
import jax.numpy as jnp
from jax import lax
from jax.experimental import pallas as pl
from jax.experimental.pallas import tpu as pltpu

F32 = jnp.float32
BF16 = jnp.bfloat16
HI = lax.Precision.HIGHEST

LANES = 128
SUBLANES = 8
VMEM_LIMIT_BYTES = 56 * 1024 * 1024

CHUNK = 64
SSD_HEAD_DIM = 64
SSD_GROUPS = 8
SSD_D_STATE = 128
SSD_CONV_W = 4
GLA_HEADS = 4
GLA_GATE_RANK = 16
GLA_GATE_NORMALIZER = 16.0
NORM_EPS = 1e-6
SSD_NORM_EPS = 1e-5
SUB = 16
CONV_PAD = SUBLANES
SMALL_W = 256


def _dot(a, b, dims=(((1,), (0,)), ((), ())), precision=None):
    return lax.dot_general(a, b, dims, precision=precision, preferred_element_type=F32)


_NT = (((1,), (1,)), ((), ()))
_TN = (((0,), (0,)), ((), ()))


def _silu(x):
    return x * (1.0 / (1.0 + jnp.exp(-x)))


def _sigmoid(x):
    return 1.0 / (1.0 + jnp.exp(-x))


def _softplus(x):
    return jnp.maximum(x, 0.0) + jnp.log(1.0 + jnp.exp(-jnp.abs(x)))


def _inproj_kernel(x_ref, g_ref, w_ref, o_ref, h_ref):
    @pl.when(pl.program_id(1) == 0)
    def _():
        x = x_ref[...]
        ms = jnp.mean(x * x, axis=-1, keepdims=True)
        h_ref[...] = (x * lax.rsqrt(ms + NORM_EPS) * g_ref[...]).astype(BF16)

    o_ref[...] = _dot(h_ref[...], w_ref[...])


def _inproj(x2, gain, w_p, tm, tn):
    m, d = x2.shape
    n = w_p.shape[1]
    return pl.pallas_call(
        _inproj_kernel,
        grid=(m // tm, n // tn),
        in_specs=[
            pl.BlockSpec((tm, d), lambda i, j: (i, 0)),
            pl.BlockSpec((1, d), lambda i, j: (0, 0)),
            pl.BlockSpec((d, tn), lambda i, j: (0, j)),
        ],
        out_specs=pl.BlockSpec((tm, tn), lambda i, j: (i, j)),
        out_shape=jax.ShapeDtypeStruct((m, n), F32),
        scratch_shapes=[pltpu.VMEM((tm, d), BF16)],
        compiler_params=pltpu.CompilerParams(
            dimension_semantics=("arbitrary", "arbitrary"), vmem_limit_bytes=VMEM_LIMIT_BYTES),
    )(x2, gain, w_p)


def _ssd_kernel(has_state, n_heads, *refs):
    if has_state:
        (xs_ref, z_ref, bc_ref, sm_ref, cwx_ref, cbx_ref, cwb_ref, cbb_ref, dtb_ref, alog_ref, dsk_ref,
         gain_ref, exp_ref, tril_ref, csx_ref, csb_ref, st_ref,
         y_ref, so_ref, xbuf, bbuf, stT, srow, dtrow) = refs
    else:
        (xs_ref, z_ref, bc_ref, sm_ref, cwx_ref, cbx_ref, cwb_ref, cbb_ref, dtb_ref, alog_ref, dsk_ref,
         gain_ref, exp_ref, tril_ref,
         y_ref, so_ref, xbuf, bbuf, stT, srow, dtrow) = refs
    c = pl.program_id(1)
    n_chunks = pl.num_programs(1)
    L = CHUNK
    P = SSD_HEAD_DIM
    N = SSD_D_STATE
    G = SSD_GROUPS
    gw = (n_heads // G) * P
    d_inner = n_heads * P
    lo = CONV_PAD - (SSD_CONV_W - 1)

    @pl.when(c == 0)
    def _():
        if has_state:
            xbuf[lo:CONV_PAD, :] = csx_ref[0]
            bbuf[lo:CONV_PAD, :] = csb_ref[0]
            for g in range(G):
                stT[:, g * gw:(g + 1) * gw] = st_ref[0, g * gw:(g + 1) * gw, :].T
        else:
            xbuf[0:CONV_PAD, :] = jnp.zeros((CONV_PAD, d_inner), F32)
            bbuf[0:CONV_PAD, :] = jnp.zeros((CONV_PAD, 2 * G * N), F32)
            stT[...] = jnp.zeros_like(stT)

    xbuf[CONV_PAD:CONV_PAD + L, :] = xs_ref[0]
    bbuf[CONV_PAD:CONV_PAD + L, :] = bc_ref[0]

    def conv(buf, w_ref, b_ref):
        acc = b_ref[...] + buf[lo:lo + L, :] * w_ref[0:1, :]
        for i in range(1, SSD_CONV_W):
            acc = acc + buf[lo + i:lo + i + L, :] * w_ref[i:i + 1, :]
        return _silu(acc)

    xs = conv(xbuf, cwx_ref, cbx_ref)
    bcm = conv(bbuf, cwb_ref, cbb_ref)
    xbuf[lo:CONV_PAD, :] = xbuf[L + lo:L + CONV_PAD, :]
    bbuf[lo:CONV_PAD, :] = bbuf[L + lo:L + CONV_PAD, :]

    dt = _softplus(sm_ref[0, :, 0:n_heads] + dtb_ref[...])
    a = dt * (-jnp.exp(alog_ref[...]))
    s = _dot(tril_ref[...], a, precision=HI)
    s_e = _dot(s, exp_ref[...], precision=HI)
    dt_e = _dot(dt, exp_ref[...], precision=HI)
    s_t = s.T
    dt_t = dt.T
    hh = n_heads // 2
    srow[...] = jnp.concatenate([s_t[:hh], s_t[hh:]], axis=1)
    dtrow[...] = jnp.concatenate([dt_t[:hh], dt_t[hh:]], axis=1)

    es_e = jnp.exp(s_e)
    s_last = s_e[L - 1:L, :]
    es_last = es_e[L - 1:L, :]
    xdt_end = xs * dt_e * jnp.exp(s_last - s_e)

    row = lax.broadcasted_iota(jnp.int32, (L, 2 * P), 0)
    lane = lax.broadcasted_iota(jnp.int32, (L, 2 * P), 1)
    causal2 = (lane % P) <= row
    left = lane < P

    dsk = dsk_ref[...]
    ys = []
    for g in range(G):
        bm = bcm[:, g * N:(g + 1) * N]
        cm = bcm[:, (G + g) * N:(G + g + 1) * N]
        cm16 = cm.astype(BF16)
        bm16 = bm.astype(BF16)
        cb2 = _dot(cm16, jnp.concatenate([bm16, bm16], axis=0), _NT)
        st_g = stT[:, g * gw:(g + 1) * gw]
        y_state = _dot(cm16, st_g.astype(BF16))
        for pp in range(gw // (2 * P)):
            p = g * (gw // (2 * P)) + pp
            sl = slice(p * 2 * P, (p + 1) * 2 * P)
            dec = jnp.exp(jnp.where(causal2, s_e[:, sl] - srow[p:p + 1, :], -jnp.inf))
            m = (cb2 * dec * dtrow[p:p + 1, :]).astype(BF16)
            xp = xs[:, sl]
            rhs = jnp.concatenate([jnp.where(left, xp, 0.0), jnp.where(left, 0.0, xp)], axis=0)
            y = _dot(m, rhs.astype(BF16))
            y = y + y_state[:, pp * 2 * P:(pp + 1) * 2 * P] * es_e[:, sl] + dsk[:, sl] * xp
            ys.append(y)
        gs = slice(g * gw, (g + 1) * gw)
        upd = _dot(bm16, xdt_end[:, gs].astype(BF16), _TN)
        stT[:, gs] = st_g * es_last[:, gs] + upd

    z = z_ref[0]
    for g in range(G):
        gs = slice(g * gw, (g + 1) * gw)
        yg = jnp.concatenate(ys[g * (gw // (2 * P)):(g + 1) * (gw // (2 * P))], axis=1) * _silu(z[:, gs])
        ms = jnp.mean(yg * yg, axis=-1, keepdims=True)
        y_ref[0, :, gs] = (yg * lax.rsqrt(ms + SSD_NORM_EPS) * gain_ref[:, gs]).astype(y_ref.dtype)

    @pl.when(c == n_chunks - 1)
    def _():
        for g in range(G):
            so_ref[0, g * gw:(g + 1) * gw, :] = stT[:, g * gw:(g + 1) * gw].T


def _ssd(proj3, col, prm, conv_state, ssd_state):
    b, t, _ = proj3.shape
    n_heads = prm["n_heads"]
    d_inner = n_heads * SSD_HEAD_DIM
    bcw = 2 * SSD_GROUPS * SSD_D_STATE
    has_state = ssd_state is not None
    L = CHUNK
    const = lambda *shape: pl.BlockSpec(shape, lambda i, j: (0,) * len(shape))
    in_specs = [
        pl.BlockSpec((1, L, d_inner), lambda i, j: (i, j, col["xs"] // d_inner)),
        pl.BlockSpec((1, L, d_inner), lambda i, j: (i, j, col["z"] // d_inner)),
        pl.BlockSpec((1, L, bcw), lambda i, j: (i, j, col["bc"] // bcw)),
        pl.BlockSpec((1, L, SMALL_W), lambda i, j: (i, j, col["small"] // SMALL_W)),
        const(SSD_CONV_W, d_inner), const(1, d_inner), const(SSD_CONV_W, bcw), const(1, bcw),
        const(1, n_heads), const(1, n_heads), const(1, d_inner), const(1, d_inner),
        const(n_heads, d_inner), const(L, L),
    ]
    args = [proj3, proj3, proj3, proj3, prm["conv_w_x"], prm["conv_b_x"], prm["conv_w_bc"], prm["conv_b_bc"],
            prm["dt_bias"], prm["a_log"], prm["d_skip_e"], prm["ssd_norm_gain"], prm["expand"], prm["tril"]]
    if has_state:
        nc = SSD_CONV_W - 1
        in_specs += [
            pl.BlockSpec((1, nc, d_inner), lambda i, j: (i, 0, 0)),
            pl.BlockSpec((1, nc, bcw), lambda i, j: (i, 0, 0)),
            pl.BlockSpec((1, d_inner, SSD_D_STATE), lambda i, j: (i, 0, 0)),
        ]
        args += [conv_state[:, :, :d_inner], conv_state[:, :, d_inner:],
                 ssd_state.reshape(b, d_inner, SSD_D_STATE)]
    y, st = pl.pallas_call(
        functools.partial(_ssd_kernel, has_state, n_heads),
        grid=(b, t // L),
        in_specs=in_specs,
        out_specs=[
            pl.BlockSpec((1, L, d_inner), lambda i, j: (i, j, 0)),
            pl.BlockSpec((1, d_inner, SSD_D_STATE), lambda i, j: (i, 0, 0)),
        ],
        out_shape=[
            jax.ShapeDtypeStruct((b, t, d_inner), BF16),
            jax.ShapeDtypeStruct((b, d_inner, SSD_D_STATE), F32),
        ],
        scratch_shapes=[
            pltpu.VMEM((CONV_PAD + L, d_inner), F32),
            pltpu.VMEM((CONV_PAD + L, bcw), F32),
            pltpu.VMEM((SSD_D_STATE, d_inner), F32),
            pltpu.VMEM((n_heads // 2, 2 * L), F32),
            pltpu.VMEM((n_heads // 2, 2 * L), F32),
        ],
        compiler_params=pltpu.CompilerParams(
            dimension_semantics=("arbitrary", "arbitrary"), vmem_limit_bytes=VMEM_LIMIT_BYTES),
    )(*args)
    return y, st.reshape(b, n_heads, SSD_HEAD_DIM, SSD_D_STATE)


def _gla_kernel(has_state, *refs):
    if has_state:
        (q_ref, k_ref, v_ref, gb_ref, sm_ref, wgk_ref, bgk_ref, gain_ref, tril_ref, st_ref,
         o_ref, so_ref, stT, bcs) = refs
    else:
        (q_ref, k_ref, v_ref, gb_ref, sm_ref, wgk_ref, bgk_ref, gain_ref, tril_ref,
         o_ref, so_ref, stT, bcs) = refs
    c = pl.program_id(1)
    n_chunks = pl.num_programs(1)
    L = CHUNK
    H = GLA_HEADS
    kd = q_ref.shape[2]
    vd = v_ref.shape[2]
    hk = kd // H
    hv = vd // H
    nb = L // SUB

    @pl.when(c == 0)
    def _():
        if has_state:
            for h in range(H):
                stT[h] = st_ref[0, h].T
        else:
            stT[...] = jnp.zeros_like(stT)

    x = _dot(sm_ref[0], wgk_ref[...], precision=HI) + bgk_ref[...]
    glog = (jnp.minimum(x, 0.0) - jnp.log(1.0 + jnp.exp(-jnp.abs(x)))) * (1.0 / GLA_GATE_NORMALIZER)
    bc = _dot(tril_ref[...], glog, precision=HI)
    bcs[...] = bc
    bc_last = bc[L - 1:L, :]
    q = q_ref[0] * (hk ** -0.5)
    k = k_ref[0]
    qe = (q * jnp.exp(bc)).astype(BF16)
    k_end = (k * jnp.exp(bc_last - bc)).astype(BF16)

    lane = lax.broadcasted_iota(jnp.int32, (SUB, L), 1)
    row = lax.broadcasted_iota(jnp.int32, (SUB, L), 0)
    a_rows = [[None] * nb for _ in range(H)]
    for blk in range(nb):
        r0 = blk * SUB
        q_b = q[r0:r0 + SUB, :]
        bc_b = bc[r0:r0 + SUB, :]
        acc = [jnp.zeros((SUB, L), F32) for _ in range(H)]
        if blk > 0:
            ref_row = bcs[r0 - 1:r0, :]
            qd = (q_b * jnp.exp(bc_b - ref_row)).astype(BF16)
            kt = k[0:r0, :] * jnp.exp(ref_row - bc[0:r0, :])
            kt = jnp.concatenate([kt, jnp.zeros((L - r0, kd), F32)], axis=0).astype(BF16)
            for h in range(H):
                acc[h] = _dot(qd[:, h * hk:(h + 1) * hk], kt[:, h * hk:(h + 1) * hk], _NT)
        for jj in range(SUB):
            j = r0 + jj
            e = jnp.exp(jnp.minimum(bc_b - bcs[j:j + 1, :], 0.0))
            pr = q_b * e * k_ref[0, j:j + 1, :]
            keep = (lane == j) & (row >= jj)
            for h in range(H):
                col = jnp.sum(pr[:, h * hk:(h + 1) * hk], axis=-1, keepdims=True)
                acc[h] = jnp.where(keep, col, acc[h])
        for h in range(H):
            a_rows[h][blk] = acc[h]

    eb_last = jnp.exp(bc_last)
    for h in range(H):
        a_h = jnp.concatenate(a_rows[h], axis=0).astype(BF16)
        v_h = v_ref[0, :, h * hv:(h + 1) * hv].astype(BF16)
        st_h = stT[h]
        o = _dot(a_h, v_h) + _dot(qe[:, h * hk:(h + 1) * hk], st_h.astype(BF16), _NT)
        stT[h] = st_h * eb_last[:, h * hk:(h + 1) * hk] + _dot(v_h, k_end[:, h * hk:(h + 1) * hk], _TN)
        ms = jnp.mean(o * o, axis=-1, keepdims=True)
        on = o * lax.rsqrt(ms + NORM_EPS) * gain_ref[...]
        o_ref[0, :, h * hv:(h + 1) * hv] = (on * _silu(gb_ref[0, :, h * hv:(h + 1) * hv])).astype(o_ref.dtype)

    @pl.when(c == n_chunks - 1)
    def _():
        for h in range(H):
            so_ref[0, h] = stT[h].T


def _gla(proj3, col, prm, gla_state):
    b, t, _ = proj3.shape
    kd = prm["gla_key_dim"]
    vd = prm["gla_val_dim"]
    H = GLA_HEADS
    hk, hv = kd // H, vd // H
    has_state = gla_state is not None
    L = CHUNK
    const = lambda *shape: pl.BlockSpec(shape, lambda i, j: (0,) * len(shape))
    in_specs = [
        pl.BlockSpec((1, L, kd), lambda i, j: (i, j, col["q"] // kd)),
        pl.BlockSpec((1, L, kd), lambda i, j: (i, j, col["k"] // kd)),
        pl.BlockSpec((1, L, vd), lambda i, j: (i, j, col["v"] // vd)),
        pl.BlockSpec((1, L, vd), lambda i, j: (i, j, col["g_b"] // vd)),
        pl.BlockSpec((1, L, SMALL_W), lambda i, j: (i, j, col["small"] // SMALL_W)),
        const(SMALL_W, kd), const(1, kd), const(1, hv), const(L, L),
    ]
    args = [proj3, proj3, proj3, proj3, proj3, prm["gk_w_pad"], prm["gk_b"], prm["gla_norm_gain"], prm["tril"]]
    if has_state:
        in_specs.append(pl.BlockSpec((1, H, hk, hv), lambda i, j: (i, 0, 0, 0)))
        args.append(gla_state)
    return pl.pallas_call(
        functools.partial(_gla_kernel, has_state),
        grid=(b, t // L),
        in_specs=in_specs,
        out_specs=[
            pl.BlockSpec((1, L, vd), lambda i, j: (i, j, 0)),
            pl.BlockSpec((1, H, hk, hv), lambda i, j: (i, 0, 0, 0)),
        ],
        out_shape=[
            jax.ShapeDtypeStruct((b, t, vd), BF16),
            jax.ShapeDtypeStruct((b, H, hk, hv), F32),
        ],
        scratch_shapes=[
            pltpu.VMEM((H, hv, hk), F32),
            pltpu.VMEM((L, kd), F32),
        ],
        compiler_params=pltpu.CompilerParams(
            dimension_semantics=("arbitrary", "arbitrary"), vmem_limit_bytes=VMEM_LIMIT_BYTES),
    )(*args)


def _merge_kernel(ys_ref, yg_ref, gs_ref, gg_ref, wbs_ref, wbg_ref, o_ref):
    a = _dot(ys_ref[...], wbs_ref[...])
    b = _dot(yg_ref[...], wbg_ref[...])
    o_ref[...] = (_sigmoid(gs_ref[...]) * a + _sigmoid(gg_ref[...]) * b).astype(o_ref.dtype)


def _post_kernel(mx_ref, x_ref, wo_ref, g_ref, o_ref):
    cc = _dot(mx_ref[...], wo_ref[...])
    ms = jnp.mean(cc * cc, axis=-1, keepdims=True)
    o_ref[...] = x_ref[...] + cc * lax.rsqrt(ms + NORM_EPS) * g_ref[...]


def _outproj(y_ssd, y_gla, proj, col, x2, prm, tm, tn):
    m, d = x2.shape
    d_inner = y_ssd.shape[1]
    vd = y_gla.shape[1]
    g0 = col["merge"] // tn
    mixed = pl.pallas_call(
        _merge_kernel,
        grid=(m // tm, d // tn),
        in_specs=[
            pl.BlockSpec((tm, d_inner), lambda i, j: (i, 0)),
            pl.BlockSpec((tm, vd), lambda i, j: (i, 0)),
            pl.BlockSpec((tm, tn), lambda i, j: (i, g0 + j)),
            pl.BlockSpec((tm, tn), lambda i, j: (i, g0 + d // tn + j)),
            pl.BlockSpec((d_inner, tn), lambda i, j: (0, j)),
            pl.BlockSpec((vd, tn), lambda i, j: (0, j)),
        ],
        out_specs=pl.BlockSpec((tm, tn), lambda i, j: (i, j)),
        out_shape=jax.ShapeDtypeStruct((m, d), BF16),
        compiler_params=pltpu.CompilerParams(
            dimension_semantics=("arbitrary", "arbitrary"), vmem_limit_bytes=VMEM_LIMIT_BYTES),
    )(y_ssd, y_gla, proj, proj, prm["w_branch_ssd"], prm["w_branch_gla"])
    return pl.pallas_call(
        _post_kernel,
        grid=(m // tm,),
        in_specs=[
            pl.BlockSpec((tm, d), lambda i: (i, 0)),
            pl.BlockSpec((tm, d), lambda i: (i, 0)),
            pl.BlockSpec((d, d), lambda i: (0, 0)),
            pl.BlockSpec((1, d), lambda i: (0, 0)),
        ],
        out_specs=pl.BlockSpec((tm, d), lambda i: (i, 0)),
        out_shape=jax.ShapeDtypeStruct((m, d), F32),
        compiler_params=pltpu.CompilerParams(
            dimension_semantics=("arbitrary",), vmem_limit_bytes=VMEM_LIMIT_BYTES),
    )(mixed, x2, prm["w_out"], prm["norm_post_gain"])


def _prepare(norm_pre_gain, w_in, conv_w, conv_b, dt_bias, a_log, d_skip, ssd_norm_gain, gla_gk_w, gla_gk_b,
             gla_norm_gain, w_branch_ssd, w_branch_gla, w_out, norm_post_gain):
    d = w_in.shape[0]
    n_heads = dt_bias.shape[0]
    d_inner = n_heads * SSD_HEAD_DIM
    bcw = 2 * SSD_GROUPS * SSD_D_STATE
    kd = gla_gk_w.shape[1]
    vd = w_branch_gla.shape[0]
    sizes = (d_inner, d_inner + bcw, n_heads, kd, kd, vd, vd, GLA_GATE_RANK, 2 * d)
    offs = np.concatenate([[0], np.cumsum(sizes)])
    seg = lambda i: w_in[:, offs[i]:offs[i + 1]]
    z_w, xbc_w, dt_w, q_w, k_w, v_w, gb_w, gk_w, mg_w = [seg(i) for i in range(9)]
    perm = np.concatenate([np.arange(0, n_heads, 2), np.arange(1, n_heads, 2)])
    dt_w = dt_w[:, perm]
    small = jnp.concatenate(
        [dt_w, gk_w, jnp.zeros((d, SMALL_W - n_heads - GLA_GATE_RANK), w_in.dtype)], axis=1)
    pieces = [("xs", xbc_w[:, :d_inner]), ("z", z_w), ("merge", mg_w), ("bc", xbc_w[:, d_inner:]),
              ("v", v_w), ("g_b", gb_w), ("q", q_w), ("k", k_w), ("small", small)]
    col, o = {}, 0
    for name, p in pieces:
        assert o % p.shape[1] == 0, name
        col[name] = o
        o += p.shape[1]
    w_p = jnp.concatenate([p for _, p in pieces], axis=1).astype(BF16)

    expand = np.zeros((n_heads, d_inner), np.float32)
    for c_ in range(n_heads):
        expand[c_, perm[c_] * SSD_HEAD_DIM:(perm[c_] + 1) * SSD_HEAD_DIM] = 1.0
    gk_w_pad = jnp.zeros((SMALL_W, kd), F32).at[n_heads:n_heads + GLA_GATE_RANK].set(gla_gk_w)
    prm = dict(
        n_heads=n_heads, gla_key_dim=kd, gla_val_dim=vd,
        norm_pre_gain=norm_pre_gain[None, :], w_p=w_p,
        conv_w_x=conv_w[:, :d_inner], conv_w_bc=conv_w[:, d_inner:],
        conv_b_x=conv_b[None, :d_inner], conv_b_bc=conv_b[None, d_inner:],
        dt_bias=dt_bias[perm][None, :], a_log=a_log[perm][None, :],
        d_skip_e=jnp.repeat(d_skip, SSD_HEAD_DIM)[None, :], ssd_norm_gain=ssd_norm_gain[None, :],
        expand=jnp.asarray(expand), tril=jnp.asarray(np.tril(np.ones((CHUNK, CHUNK), np.float32))),
        gk_w_pad=gk_w_pad, gk_b=gla_gk_b[None, :], gla_norm_gain=gla_norm_gain[None, :],
        w_branch_ssd=w_branch_ssd.astype(BF16), w_branch_gla=w_branch_gla.astype(BF16),
        w_out=w_out.astype(BF16), norm_post_gain=norm_post_gain[None, :],
    )
    return prm, col


def _row_tile(m, want):
    tm = min(m, want)
    assert m % tm == 0
    return tm


def _layer(x, conv_state, ssd_state, gla_state, prm, col):
    b, t, d = x.shape
    assert t % CHUNK == 0 and t >= SSD_CONV_W - 1
    m = b * t
    x2 = x.reshape(m, d)
    n = prm["w_p"].shape[1]
    proj = _inproj(x2, prm["norm_pre_gain"], prm["w_p"], _row_tile(m, 1024), 768)
    proj3 = proj.reshape(b, t, n)
    d_inner = prm["n_heads"] * SSD_HEAD_DIM
    bcw = 2 * SSD_GROUPS * SSD_D_STATE
    nc = SSD_CONV_W - 1
    new_conv = jnp.concatenate(
        [proj3[:, t - nc:, col["xs"]:col["xs"] + d_inner], proj3[:, t - nc:, col["bc"]:col["bc"] + bcw]], axis=-1)
    y_ssd, new_ssd = _ssd(proj3, col, prm, conv_state, ssd_state)
    y_gla, new_gla = _gla(proj3, col, prm, gla_state)
    y = _outproj(y_ssd.reshape(m, -1), y_gla.reshape(m, -1), proj, col, x2, prm, _row_tile(m, 512), 512)
    return y.reshape(b, t, d), new_conv, new_ssd, new_gla


def kernel(x_prompt, x_sample, state_conv_ssd, state_ssd, state_gla, norm_pre_gain, w_in, conv_w, conv_b, dt_bias,
           a_log, d_skip, ssd_norm_gain, gla_gk_w, gla_gk_b, gla_norm_gain, w_branch_ssd, w_branch_gla, w_out,
           norm_post_gain):
    depth = w_in.shape[0]
    yp, ys = x_prompt, x_sample
    outs = [[] for _ in range(6)]
    for layer in range(depth):
        prm, col = _prepare(norm_pre_gain[layer], w_in[layer], conv_w[layer], conv_b[layer], dt_bias[layer],
                            a_log[layer], d_skip[layer], ssd_norm_gain[layer], gla_gk_w[layer], gla_gk_b[layer],
                            gla_norm_gain[layer], w_branch_ssd[layer], w_branch_gla[layer], w_out[layer],
                            norm_post_gain[layer])
        yp, c_p, s_p, g_p = _layer(yp, None, None, None, prm, col)
        ys, c_s, s_s, g_s = _layer(ys, state_conv_ssd[layer], state_ssd[layer], state_gla[layer], prm, col)
        for lst, val in zip(outs, (c_p, s_p, g_p, c_s, s_s, g_s)):
            lst.append(val)
    return (yp, ys) + tuple(jnp.stack(o) for o in outs)
```

```python
import functools
import math

import numpy as np
import jax
import jax.numpy as jnp
from jax import lax
from jax.experimental import pallas as pl
from jax.experimental.pallas import tpu as pltpu

F32 = jnp.float32
BF16 = jnp.bfloat16

LANES = 128
SUBLANES = 8
VMEM_LIMIT_BYTES = 56 * 1024 * 1024

CHUNK = 64
SSD_HEAD_DIM = 64
SSD_GROUPS = 8
SSD_D_STATE = 128
SSD_CONV_W = 4
GLA_HEADS = 4
GLA_GATE_RANK = 16
GLA_GATE_NORMALIZER = 16.0
NORM_EPS = 1e-6
SSD_NORM_EPS = 1e-5
SUB = 16
SMALL_W = 256
DT_COPIES = 3
LOG2E = math.log2(math.e)


def _dot(a, b, dims=(((1,), (0,)), ((), ()))):
    return lax.dot_general(a, b, dims, preferred_element_type=F32)


_NT = (((1,), (1,)), ((), ()))
_TN = (((0,), (0,)), ((), ()))


def _silu(x):
    return x * (1.0 / (1.0 + jnp.exp(-x)))


def _sigmoid(x):
    return 1.0 / (1.0 + jnp.exp(-x))


def _softplus(x):
    return jnp.maximum(x, 0.0) + jnp.log(1.0 + jnp.exp(-jnp.abs(x)))


def _residuals(x):
    r1 = x - x.astype(BF16).astype(F32)
    r2 = r1 - r1.astype(BF16).astype(F32)
    return x, r1, r2


def _cumsum_rows(tril3_ref, x):
    parts = [p.astype(BF16) for p in _residuals(x)]
    return _dot(tril3_ref[...], jnp.concatenate(parts, axis=0))


def _inproj_kernel(x_ref, g_ref, w_ref, ws_ref, o_ref, os_ref, h_ref):
    @pl.when(pl.program_id(1) == 0)
    def _():
        x = x_ref[...]
        ms = jnp.mean(x * x, axis=-1, keepdims=True)
        h_ref[...] = (x * lax.rsqrt(ms + NORM_EPS) * g_ref[...]).astype(BF16)
        os_ref[...] = _dot(h_ref[...], ws_ref[...])

    o_ref[...] = _dot(h_ref[...], w_ref[...])


def _inproj(x2, gain, w_main, w_small, tm, tn):
    m, d = x2.shape
    n = w_main.shape[1]
    ns = w_small.shape[1]
    return pl.pallas_call(
        _inproj_kernel,
        grid=(m // tm, n // tn),
        in_specs=[
            pl.BlockSpec((tm, d), lambda i, j: (i, 0)),
            pl.BlockSpec((1, d), lambda i, j: (0, 0)),
            pl.BlockSpec((d, tn), lambda i, j: (0, j)),
            pl.BlockSpec((d, ns), lambda i, j: (0, 0)),
        ],
        out_specs=[
            pl.BlockSpec((tm, tn), lambda i, j: (i, j)),
            pl.BlockSpec((tm, ns), lambda i, j: (i, 0)),
        ],
        out_shape=[jax.ShapeDtypeStruct((m, n), F32), jax.ShapeDtypeStruct((m, ns), F32)],
        scratch_shapes=[pltpu.VMEM((tm, d), BF16)],
        compiler_params=pltpu.CompilerParams(
            dimension_semantics=("arbitrary", "arbitrary"), vmem_limit_bytes=VMEM_LIMIT_BYTES),
    )(x2, gain, w_main, w_small)


def _ssd_kernel(has_state, n_heads, *refs):
    if has_state:
        (xs_ref, z_ref, bc_ref, sm_ref, cwx_ref, cbx_ref, cwb_ref, cbb_ref, dtb_ref, alog_ref, dsk_ref,
         gain_ref, exp_ref, tril_ref, csx_ref, csb_ref, st_ref,
         y_ref, so_ref, tailx, tailb, stT, srow, dtrow, xs_s, xs16, bc16, bcast) = refs
    else:
        (xs_ref, z_ref, bc_ref, sm_ref, cwx_ref, cbx_ref, cwb_ref, cbb_ref, dtb_ref, alog_ref, dsk_ref,
         gain_ref, exp_ref, tril_ref,
         y_ref, so_ref, tailx, tailb, stT, srow, dtrow, xs_s, xs16, bc16, bcast) = refs
    c = pl.program_id(1)
    n_chunks = pl.num_programs(1)
    L = CHUNK
    P = SSD_HEAD_DIM
    N = SSD_D_STATE
    G = SSD_GROUPS
    gw = (n_heads // G) * P
    d_inner = n_heads * P
    nc = SSD_CONV_W - 1

    @pl.when(c == 0)
    def _():
        tailx[...] = jnp.zeros_like(tailx)
        tailb[...] = jnp.zeros_like(tailb)
        if has_state:
            tailx[SUBLANES - nc:SUBLANES, :] = csx_ref[0]
            tailb[SUBLANES - nc:SUBLANES, :] = csb_ref[0]
            for g in range(G):
                stT[:, g * gw:(g + 1) * gw] = st_ref[0, g * gw:(g + 1) * gw, :].T
        else:
            stT[...] = jnp.zeros_like(stT)

    def conv(x_ref, tail, w_ref, b_ref, width, emit):
        cw = 4 * LANES
        for cb in range(width // cw):
            cs = slice(cb * cw, (cb + 1) * cw)
            full = jnp.concatenate([tail[:, cs], x_ref[0, :, cs]], axis=0)
            acc = b_ref[:, cs] + full[SUBLANES:, :] * w_ref[nc:nc + 1, cs]
            for k in range(1, SSD_CONV_W):
                acc = acc + pltpu.roll(full, k, 0)[SUBLANES:, :] * w_ref[nc - k:nc - k + 1, cs]
            emit(cs, _silu(acc))
        tail[...] = x_ref[0, L - SUBLANES:L, :]

    def emit_x(cs, act):
        xs_s[:, cs] = act
        xs16[:, cs] = act.astype(BF16)

    def emit_bc(cs, act):
        bc16[:, cs] = act.astype(BF16)

    conv(xs_ref, tailx, cwx_ref, cbx_ref, d_inner, emit_x)
    conv(bc_ref, tailb, cwb_ref, cbb_ref, 2 * G * N, emit_bc)

    dt3 = _softplus(sm_ref[0] + dtb_ref[...])
    s3 = _cumsum_rows(tril_ref, dt3 * (-jnp.exp(alog_ref[...])))
    lane_s = lax.broadcasted_iota(jnp.int32, (2 * L, SMALL_W), 1)
    x0, x1, x2 = _residuals(jnp.concatenate([s3, dt3], axis=0))
    split = jnp.where(lane_s < n_heads, x0, jnp.where(lane_s < 2 * n_heads, x1, x2)).astype(BF16)
    bcast[...] = _dot(split, exp_ref[...])
    hh = n_heads // 2
    s_t = s3[:, 0:2 * L].T
    dt_t = dt3[:, 0:2 * L].T
    srow[...] = jnp.concatenate([s_t[0:hh], s_t[hh:2 * hh]], axis=1)
    dtrow[...] = jnp.concatenate([dt_t[0:hh], dt_t[hh:2 * hh]], axis=1)

    row = lax.broadcasted_iota(jnp.int32, (L, 2 * P), 0)
    lane = lax.broadcasted_iota(jnp.int32, (L, 2 * P), 1)
    causal2 = (lane % P) <= row
    left = lane < P
    ppg = gw // (2 * P)

    for g in range(G):
        gs = slice(g * gw, (g + 1) * gw)
        bm16 = bc16[:, g * N:(g + 1) * N]
        cm16 = bc16[:, (G + g) * N:(G + g + 1) * N]
        cb2 = _dot(cm16, jnp.concatenate([bm16, bm16], axis=0), _NT)
        st_g = stT[:, gs]
        y_state = _dot(cm16, st_g.astype(BF16))
        ys = []
        for pp in range(ppg):
            p = g * ppg + pp
            sl = slice(p * 2 * P, (p + 1) * 2 * P)
            se = bcast[0:L, sl]
            dec = jnp.exp(jnp.where(causal2, se - srow[p:p + 1, :], -jnp.inf))
            m = (cb2 * dec * dtrow[p:p + 1, :]).astype(BF16)
            xp16 = xs16[:, sl]
            zero = jnp.zeros_like(xp16)
            rhs = jnp.concatenate([jnp.where(left, xp16, zero), jnp.where(left, zero, xp16)], axis=0)
            y = _dot(m, rhs)
            y = y + y_state[:, pp * 2 * P:(pp + 1) * 2 * P] * jnp.exp(se) + dsk_ref[:, sl] * xs_s[:, sl]
            ys.append(y)
        s_last = bcast[L - 1:L, gs]
        xdt_end = xs_s[:, gs] * bcast[L:2 * L, gs] * jnp.exp(s_last - bcast[0:L, gs])
        upd = _dot(bm16, xdt_end.astype(BF16), _TN)
        stT[:, gs] = st_g * jnp.exp(s_last) + upd

        yg = jnp.concatenate(ys, axis=1) * _silu(z_ref[0, :, gs])
        ms = jnp.mean(yg * yg, axis=-1, keepdims=True)
        y_ref[0, :, gs] = (yg * lax.rsqrt(ms + SSD_NORM_EPS) * gain_ref[:, gs]).astype(y_ref.dtype)

    @pl.when(c == n_chunks - 1)
    def _():
        for g in range(G):
            so_ref[0, g * gw:(g + 1) * gw, :] = stT[:, g * gw:(g + 1) * gw].T


def _ssd(proj3, small3, col, prm, conv_state, ssd_state):
    b, t, _ = proj3.shape
    n_heads = prm["n_heads"]
    d_inner = n_heads * SSD_HEAD_DIM
    bcw = 2 * SSD_GROUPS * SSD_D_STATE
    has_state = ssd_state is not None
    L = CHUNK
    const = lambda *shape: pl.BlockSpec(shape, lambda i, j: (0,) * len(shape))
    in_specs = [
        pl.BlockSpec((1, L, d_inner), lambda i, j: (i, j, col["xs"] // d_inner)),
        pl.BlockSpec((1, L, d_inner), lambda i, j: (i, j, col["z"] // d_inner)),
        pl.BlockSpec((1, L, bcw), lambda i, j: (i, j, col["bc"] // bcw)),
        pl.BlockSpec((1, L, SMALL_W), lambda i, j: (i, j, 0)),
        const(SSD_CONV_W, d_inner), const(1, d_inner), const(SSD_CONV_W, bcw), const(1, bcw),
        const(1, SMALL_W), const(1, SMALL_W), const(1, d_inner), const(1, d_inner),
        const(SMALL_W, d_inner), const(L, 3 * L),
    ]
    args = [proj3, proj3, proj3, small3, prm["conv_w_x"], prm["conv_b_x"], prm["conv_w_bc"], prm["conv_b_bc"],
            prm["dt_bias3"], prm["a_log3"], prm["d_skip_e"], prm["ssd_norm_gain"], prm["expand3"], prm["tril3"]]
    if has_state:
        nc = SSD_CONV_W - 1
        in_specs += [
            pl.BlockSpec((1, nc, d_inner), lambda i, j: (i, 0, 0)),
            pl.BlockSpec((1, nc, bcw), lambda i, j: (i, 0, 0)),
            pl.BlockSpec((1, d_inner, SSD_D_STATE), lambda i, j: (i, 0, 0)),
        ]
        args += [conv_state[:, :, :d_inner], conv_state[:, :, d_inner:],
                 ssd_state.reshape(b, d_inner, SSD_D_STATE)]
    y, st = pl.pallas_call(
        functools.partial(_ssd_kernel, has_state, n_heads),
        grid=(b, t // L),
        in_specs=in_specs,
        out_specs=[
            pl.BlockSpec((1, L, d_inner), lambda i, j: (i, j, 0)),
            pl.BlockSpec((1, d_inner, SSD_D_STATE), lambda i, j: (i, 0, 0)),
        ],
        out_shape=[
            jax.ShapeDtypeStruct((b, t, d_inner), BF16),
            jax.ShapeDtypeStruct((b, d_inner, SSD_D_STATE), F32),
        ],
        scratch_shapes=[
            pltpu.VMEM((SUBLANES, d_inner), F32),
            pltpu.VMEM((SUBLANES, bcw), F32),
            pltpu.VMEM((SSD_D_STATE, d_inner), F32),
            pltpu.VMEM((n_heads // 2, 2 * L), F32),
            pltpu.VMEM((n_heads // 2, 2 * L), F32),
            pltpu.VMEM((L, d_inner), F32),
            pltpu.VMEM((L, d_inner), BF16),
            pltpu.VMEM((L, bcw), BF16),
            pltpu.VMEM((2 * L, d_inner), F32),
        ],
        compiler_params=pltpu.CompilerParams(
            dimension_semantics=("arbitrary", "arbitrary"), vmem_limit_bytes=VMEM_LIMIT_BYTES),
    )(*args)
    return y, st.reshape(b, n_heads, SSD_HEAD_DIM, SSD_D_STATE)


def _gla_kernel(has_state, *refs):
    if has_state:
        (q_ref, k_ref, v_ref, gb_ref, sm_ref, wgk_ref, bgk_ref, gain_ref, tril_ref, st_ref,
         o_ref, so_ref, stT, bcs) = refs
    else:
        (q_ref, k_ref, v_ref, gb_ref, sm_ref, wgk_ref, bgk_ref, gain_ref, tril_ref,
         o_ref, so_ref, stT, bcs) = refs
    c = pl.program_id(1)
    n_chunks = pl.num_programs(1)
    L = CHUNK
    H = GLA_HEADS
    kd = q_ref.shape[2]
    vd = v_ref.shape[2]
    hk = kd // H
    hv = vd // H
    nb = L // SUB

    @pl.when(c == 0)
    def _():
        if has_state:
            for h in range(H):
                stT[h] = st_ref[0, h].T
        else:
            stT[...] = jnp.zeros_like(stT)

    sm = sm_ref[0]
    sm_hi = sm.astype(BF16)
    sm_lo = (sm - sm_hi.astype(F32)).astype(BF16)
    x = _dot(jnp.concatenate([sm_hi, sm_hi, sm_lo], axis=1), wgk_ref[...]) + bgk_ref[...]
    glog2 = (jnp.minimum(x, 0.0) - jnp.log(1.0 + jnp.exp(-jnp.abs(x)))) * (LOG2E / GLA_GATE_NORMALIZER)
    bc = _cumsum_rows(tril_ref, glog2)
    bcs[...] = bc
    bc_last = bc[L - 1:L, :]
    q = q_ref[0] * (hk ** -0.5)
    k = k_ref[0]
    qe = (q * jnp.exp2(bc)).astype(BF16)
    k_end = (k * jnp.exp2(bc_last - bc)).astype(BF16)

    nt = SUB // SUBLANES
    lane = lax.broadcasted_iota(jnp.int32, (SUBLANES, L), 1)
    row = lax.broadcasted_iota(jnp.int32, (SUBLANES, L), 0)
    a_rows = [[None] * nb for _ in range(H)]
    for blk in range(nb):
        r0 = blk * SUB
        off = [None] * H
        if blk > 0:
            ref_row = bcs[r0 - 1:r0, :]
            qd = (q[r0:r0 + SUB, :] * jnp.exp2(bc[r0:r0 + SUB, :] - ref_row)).astype(BF16)
            kt = k[0:r0, :] * jnp.exp2(ref_row - bc[0:r0, :])
            kt = jnp.concatenate([kt, jnp.zeros((L - r0, kd), F32)], axis=0).astype(BF16)
            for h in range(H):
                off[h] = _dot(qd[:, h * hk:(h + 1) * hk], kt[:, h * hk:(h + 1) * hk], _NT)
        diag = [[jnp.zeros((SUBLANES, L), F32) for _ in range(nt)] for _ in range(H)]
        for jj in range(SUB):
            j = r0 + jj
            for ti in range(jj // SUBLANES, nt):
                t0 = r0 + ti * SUBLANES
                e = jnp.exp2(bc[t0:t0 + SUBLANES, :] - bcs[j:j + 1, :])
                pr = q[t0:t0 + SUBLANES, :] * e * k_ref[0, j:j + 1, :]
                keep = (lane == j) & (row + t0 >= j)
                for h in range(H):
                    col = jnp.sum(pr[:, h * hk:(h + 1) * hk], axis=-1, keepdims=True)
                    diag[h][ti] = jnp.where(keep, col, diag[h][ti])
        for h in range(H):
            d_h = jnp.concatenate(diag[h], axis=0)
            a_rows[h][blk] = d_h if off[h] is None else off[h] + d_h

    eb_last = jnp.exp2(bc_last)
    for h in range(H):
        a_h = jnp.concatenate(a_rows[h], axis=0).astype(BF16)
        v_h = v_ref[0, :, h * hv:(h + 1) * hv].astype(BF16)
        st_h = stT[h]
        o = _dot(a_h, v_h) + _dot(qe[:, h * hk:(h + 1) * hk], st_h.astype(BF16), _NT)
        stT[h] = st_h * eb_last[:, h * hk:(h + 1) * hk] + _dot(v_h, k_end[:, h * hk:(h + 1) * hk], _TN)
        ms = jnp.mean(o * o, axis=-1, keepdims=True)
        on = o * lax.rsqrt(ms + NORM_EPS) * gain_ref[...]
        o_ref[0, :, h * hv:(h + 1) * hv] = (on * _silu(gb_ref[0, :, h * hv:(h + 1) * hv])).astype(o_ref.dtype)

    @pl.when(c == n_chunks - 1)
    def _():
        for h in range(H):
            so_ref[0, h] = stT[h].T


def _gla(proj3, small3, col, prm, gla_state):
    b, t, _ = proj3.shape
    kd = prm["gla_key_dim"]
    vd = prm["gla_val_dim"]
    H = GLA_HEADS
    hk, hv = kd // H, vd // H
    has_state = gla_state is not None
    L = CHUNK
    const = lambda *shape: pl.BlockSpec(shape, lambda i, j: (0,) * len(shape))
    in_specs = [
        pl.BlockSpec((1, L, kd), lambda i, j: (i, j, col["q"] // kd)),
        pl.BlockSpec((1, L, kd), lambda i, j: (i, j, col["k"] // kd)),
        pl.BlockSpec((1, L, vd), lambda i, j: (i, j, col["v"] // vd)),
        pl.BlockSpec((1, L, vd), lambda i, j: (i, j, col["g_b"] // vd)),
        pl.BlockSpec((1, L, SMALL_W), lambda i, j: (i, j, 0)),
        const(3 * SMALL_W, kd), const(1, kd), const(1, hv), const(L, 3 * L),
    ]
    args = [proj3, proj3, proj3, proj3, small3, prm["gk_w3"], prm["gk_b"], prm["gla_norm_gain"], prm["tril3"]]
    if has_state:
        in_specs.append(pl.BlockSpec((1, H, hk, hv), lambda i, j: (i, 0, 0, 0)))
        args.append(gla_state)
    return pl.pallas_call(
        functools.partial(_gla_kernel, has_state),
        grid=(b, t // L),
        in_specs=in_specs,
        out_specs=[
            pl.BlockSpec((1, L, vd), lambda i, j: (i, j, 0)),
            pl.BlockSpec((1, H, hk, hv), lambda i, j: (i, 0, 0, 0)),
        ],
        out_shape=[
            jax.ShapeDtypeStruct((b, t, vd), BF16),
            jax.ShapeDtypeStruct((b, H, hk, hv), F32),
        ],
        scratch_shapes=[
            pltpu.VMEM((H, hv, hk), F32),
            pltpu.VMEM((L, kd), F32),
        ],
        compiler_params=pltpu.CompilerParams(
            dimension_semantics=("arbitrary", "arbitrary"), vmem_limit_bytes=VMEM_LIMIT_BYTES),
    )(*args)


def _merge_kernel(ys_ref, yg_ref, gs_ref, gg_ref, wbs_ref, wbg_ref, o_ref):
    a = _dot(ys_ref[...], wbs_ref[...])
    b = _dot(yg_ref[...], wbg_ref[...])
    o_ref[...] = (_sigmoid(gs_ref[...]) * a + _sigmoid(gg_ref[...]) * b).astype(o_ref.dtype)


def _post_kernel(mx_ref, x_ref, wo_ref, g_ref, o_ref):
    cc = _dot(mx_ref[...], wo_ref[...])
    ms = jnp.mean(cc * cc, axis=-1, keepdims=True)
    o_ref[...] = x_ref[...] + cc * lax.rsqrt(ms + NORM_EPS) * g_ref[...]


def _outproj(y_ssd, y_gla, proj, col, x2, prm, tm_merge, tn, tm_post):
    m, d = x2.shape
    d_inner = y_ssd.shape[1]
    vd = y_gla.shape[1]
    g0 = col["merge"] // tn
    mixed = pl.pallas_call(
        _merge_kernel,
        grid=(m // tm_merge, d // tn),
        in_specs=[
            pl.BlockSpec((tm_merge, d_inner), lambda i, j: (i, 0)),
            pl.BlockSpec((tm_merge, vd), lambda i, j: (i, 0)),
            pl.BlockSpec((tm_merge, tn), lambda i, j: (i, g0 + j)),
            pl.BlockSpec((tm_merge, tn), lambda i, j: (i, g0 + d // tn + j)),
            pl.BlockSpec((d_inner, tn), lambda i, j: (0, j)),
            pl.BlockSpec((vd, tn), lambda i, j: (0, j)),
        ],
        out_specs=pl.BlockSpec((tm_merge, tn), lambda i, j: (i, j)),
        out_shape=jax.ShapeDtypeStruct((m, d), BF16),
        compiler_params=pltpu.CompilerParams(
            dimension_semantics=("arbitrary", "arbitrary"), vmem_limit_bytes=VMEM_LIMIT_BYTES),
    )(y_ssd, y_gla, proj, proj, prm["w_branch_ssd"], prm["w_branch_gla"])
    return pl.pallas_call(
        _post_kernel,
        grid=(m // tm_post,),
        in_specs=[
            pl.BlockSpec((tm_post, d), lambda i: (i, 0)),
            pl.BlockSpec((tm_post, d), lambda i: (i, 0)),
            pl.BlockSpec((d, d), lambda i: (0, 0)),
            pl.BlockSpec((1, d), lambda i: (0, 0)),
        ],
        out_specs=pl.BlockSpec((tm_post, d), lambda i: (i, 0)),
        out_shape=jax.ShapeDtypeStruct((m, d), F32),
        compiler_params=pltpu.CompilerParams(
            dimension_semantics=("arbitrary",), vmem_limit_bytes=VMEM_LIMIT_BYTES),
    )(mixed, x2, prm["w_out"], prm["norm_post_gain"])


def _prepare(norm_pre_gain, w_in, conv_w, conv_b, dt_bias, a_log, d_skip, ssd_norm_gain, gla_gk_w, gla_gk_b,
             gla_norm_gain, w_branch_ssd, w_branch_gla, w_out, norm_post_gain):
    d = w_in.shape[0]
    n_heads = dt_bias.shape[0]
    d_inner = n_heads * SSD_HEAD_DIM
    bcw = 2 * SSD_GROUPS * SSD_D_STATE
    kd = gla_gk_w.shape[1]
    vd = w_branch_gla.shape[0]
    assert DT_COPIES * n_heads + GLA_GATE_RANK <= SMALL_W
    sizes = (d_inner, d_inner + bcw, n_heads, kd, kd, vd, vd, GLA_GATE_RANK, 2 * d)
    offs = np.concatenate([[0], np.cumsum(sizes)])
    seg = lambda i: w_in[:, offs[i]:offs[i + 1]]
    z_w, xbc_w, dt_w, q_w, k_w, v_w, gb_w, gk_w, mg_w = [seg(i) for i in range(9)]
    perm = np.concatenate([np.arange(0, n_heads, 2), np.arange(1, n_heads, 2)])
    dt_w = dt_w[:, perm]
    gk0 = DT_COPIES * n_heads
    pad = SMALL_W - gk0 - GLA_GATE_RANK
    w_small = jnp.concatenate([dt_w] * DT_COPIES + [gk_w, jnp.zeros((d, pad), w_in.dtype)], axis=1).astype(BF16)
    pieces = [("xs", xbc_w[:, :d_inner]), ("z", z_w), ("merge", mg_w), ("bc", xbc_w[:, d_inner:]),
              ("v", v_w), ("g_b", gb_w), ("q", q_w), ("k", k_w)]
    col, o = {}, 0
    for name, p in pieces:
        assert o % p.shape[1] == 0, name
        col[name] = o
        o += p.shape[1]
    w_main = jnp.concatenate([p for _, p in pieces], axis=1).astype(BF16)

    def small_row(v, fill):
        return jnp.concatenate([v[perm]] * DT_COPIES + [jnp.full((SMALL_W - gk0,), fill, v.dtype)])[None, :]

    expand = np.zeros((SMALL_W, d_inner), np.float32)
    for r in range(DT_COPIES):
        for c_ in range(n_heads):
            expand[r * n_heads + c_, perm[c_] * SSD_HEAD_DIM:(perm[c_] + 1) * SSD_HEAD_DIM] = 1.0
    tril = np.tril(np.ones((CHUNK, CHUNK), np.float32))
    gk_w_pad = jnp.zeros((SMALL_W, kd), F32).at[gk0:gk0 + GLA_GATE_RANK].set(gla_gk_w)
    gk_hi = gk_w_pad.astype(BF16)
    gk_lo = (gk_w_pad - gk_hi.astype(F32)).astype(BF16)
    prm = dict(
        n_heads=n_heads, gla_key_dim=kd, gla_val_dim=vd,
        norm_pre_gain=norm_pre_gain[None, :], w_main=w_main, w_small=w_small,
        conv_w_x=conv_w[:, :d_inner], conv_w_bc=conv_w[:, d_inner:],
        conv_b_x=conv_b[None, :d_inner], conv_b_bc=conv_b[None, d_inner:],
        dt_bias3=small_row(dt_bias, 0.0), a_log3=small_row(a_log, 0.0),
        d_skip_e=jnp.repeat(d_skip, SSD_HEAD_DIM)[None, :], ssd_norm_gain=ssd_norm_gain[None, :],
        expand3=jnp.asarray(expand, BF16), tril3=jnp.asarray(np.concatenate([tril] * 3, axis=1), BF16),
        gk_w3=jnp.concatenate([gk_hi, gk_lo, gk_hi], axis=0), gk_b=gla_gk_b[None, :],
        gla_norm_gain=gla_norm_gain[None, :],
        w_branch_ssd=w_branch_ssd.astype(BF16), w_branch_gla=w_branch_gla.astype(BF16),
        w_out=w_out.astype(BF16), norm_post_gain=norm_post_gain[None, :],
    )
    return prm, col


def _row_tile(m, want):
    tm = min(m, want)
    assert m % tm == 0
    return tm


def _layer(x, conv_state, ssd_state, gla_state, prm, col):
    b, t, d = x.shape
    assert t % CHUNK == 0 and t >= SSD_CONV_W - 1
    m = b * t
    x2 = x.reshape(m, d)
    proj, small = _inproj(x2, prm["norm_pre_gain"], prm["w_main"], prm["w_small"], _row_tile(m, 1024), 1024)
    proj3 = proj.reshape(b, t, -1)
    small3 = small.reshape(b, t, -1)
    d_inner = prm["n_heads"] * SSD_HEAD_DIM
    bcw = 2 * SSD_GROUPS * SSD_D_STATE
    nc = SSD_CONV_W - 1
    new_conv = jnp.concatenate(
        [proj3[:, t - nc:, col["xs"]:col["xs"] + d_inner], proj3[:, t - nc:, col["bc"]:col["bc"] + bcw]], axis=-1)
    y_ssd, new_ssd = _ssd(proj3, small3, col, prm, conv_state, ssd_state)
    y_gla, new_gla = _gla(proj3, small3, col, prm, gla_state)
    y = _outproj(y_ssd.reshape(m, -1), y_gla.reshape(m, -1), proj, col, x2, prm,
                 _row_tile(m, 1024), 512, _row_tile(m, 512))
    return y.reshape(b, t, d), new_conv, new_ssd, new_gla


def kernel(x_prompt, x_sample, state_conv_ssd, state_ssd, state_gla, norm_pre_gain, w_in, conv_w, conv_b, dt_bias,
           a_log, d_skip, ssd_norm_gain, gla_gk_w, gla_gk_b, gla_norm_gain, w_branch_ssd, w_branch_gla, w_out,
           norm_post_gain):
    depth = w_in.shape[0]
    yp, ys = x_prompt, x_sample
    outs = [[] for _ in range(6)]
    for layer in range(depth):
        prm, col = _prepare(norm_pre_gain[layer], w_in[layer], conv_w[layer], conv_b[layer], dt_bias[layer],
                            a_log[layer], d_skip[layer], ssd_norm_gain[layer], gla_gk_w[layer], gla_gk_b[layer],
                            gla_norm_gain[layer], w_branch_ssd[layer], w_branch_gla[layer], w_out[layer],
                            norm_post_gain[layer])
        yp, c_p, s_p, g_p = _layer(yp, None, None, None, prm, col)
        ys, c_s, s_s, g_s = _layer(ys, state_conv_ssd[layer], state_ssd[layer], state_gla[layer], prm, col)
        for lst, val in zip(outs, (c_p, s_p, g_p, c_s, s_s, g_s)):
            lst.append(val)
    return (yp, ys) + tuple(jnp.stack(o) for o in outs)
```

```python
import functools
import math

import numpy as np
import jax
import jax.numpy as jnp
from jax import lax
from jax.experimental import pallas as pl
from jax.experimental.pallas import tpu as pltpu

F32 = jnp.float32
BF16 = jnp.bfloat16

LANES = 128
SUBLANES = 8
VMEM_LIMIT_BYTES = 56 * 1024 * 1024

CHUNK = 64
SSD_HEAD_DIM = 64
SSD_GROUPS = 8
SSD_D_STATE = 128
SSD_CONV_W = 4
GLA_HEADS = 4
GLA_GATE_RANK = 16
GLA_GATE_NORMALIZER = 16.0
NORM_EPS = 1e-6
SSD_NORM_EPS = 1e-5
SUB = 16
SMALL_W = 256
DT_COPIES = 3
LOG2E = math.log2(math.e)


def _dot(a, b, dims=(((1,), (0,)), ((), ()))):
    return lax.dot_general(a, b, dims, preferred_element_type=F32)


_NT = (((1,), (1,)), ((), ()))
_TN = (((0,), (0,)), ((), ()))


def _silu(x):
    return x * (1.0 / (1.0 + jnp.exp(-x)))


def _sigmoid(x):
    return 1.0 / (1.0 + jnp.exp(-x))


def _softplus(x):
    return jnp.maximum(x, 0.0) + jnp.log(1.0 + jnp.exp(-jnp.abs(x)))


def _residuals(x):
    r1 = x - x.astype(BF16).astype(F32)
    r2 = r1 - r1.astype(BF16).astype(F32)
    return x, r1, r2


def _cumsum_rows(tril3_ref, x):
    parts = [p.astype(BF16) for p in _residuals(x)]
    return _dot(tril3_ref[...], jnp.concatenate(parts, axis=0))


def _inproj_kernel(x_ref, g_ref, w_ref, ws_ref, o_ref, os_ref, h_ref):
    @pl.when(pl.program_id(1) == 0)
    def _():
        x = x_ref[...]
        ms = jnp.mean(x * x, axis=-1, keepdims=True)
        h_ref[...] = (x * lax.rsqrt(ms + NORM_EPS) * g_ref[...]).astype(BF16)
        os_ref[...] = _dot(h_ref[...], ws_ref[...])

    o_ref[...] = _dot(h_ref[...], w_ref[...]).astype(o_ref.dtype)


def _inproj(x2, gain, w_main, w_small, tm, tn):
    m, d = x2.shape
    n = w_main.shape[1]
    ns = w_small.shape[1]
    return pl.pallas_call(
        _inproj_kernel,
        grid=(m // tm, n // tn),
        in_specs=[
            pl.BlockSpec((tm, d), lambda i, j: (i, 0)),
            pl.BlockSpec((1, d), lambda i, j: (0, 0)),
            pl.BlockSpec((d, tn), lambda i, j: (0, j)),
            pl.BlockSpec((d, ns), lambda i, j: (0, 0)),
        ],
        out_specs=[
            pl.BlockSpec((tm, tn), lambda i, j: (i, j)),
            pl.BlockSpec((tm, ns), lambda i, j: (i, 0)),
        ],
        out_shape=[jax.ShapeDtypeStruct((m, n), BF16), jax.ShapeDtypeStruct((m, ns), F32)],
        scratch_shapes=[pltpu.VMEM((tm, d), BF16)],
        compiler_params=pltpu.CompilerParams(
            dimension_semantics=("arbitrary", "arbitrary"), vmem_limit_bytes=VMEM_LIMIT_BYTES),
    )(x2, gain, w_main, w_small)


def _ssd_kernel(has_state, n_heads, *refs):
    if has_state:
        (xs_ref, z_ref, bc_ref, sm_ref, cwx_ref, cbx_ref, cwb_ref, cbb_ref, dtb_ref, alog_ref, dsk_ref,
         gain_ref, exp_ref, tril_ref, csx_ref, csb_ref, st_ref,
         y_ref, so_ref, tailx, tailb, stT, srow, dtrow, xs_s, xs16, bc16, bcast) = refs
    else:
        (xs_ref, z_ref, bc_ref, sm_ref, cwx_ref, cbx_ref, cwb_ref, cbb_ref, dtb_ref, alog_ref, dsk_ref,
         gain_ref, exp_ref, tril_ref,
         y_ref, so_ref, tailx, tailb, stT, srow, dtrow, xs_s, xs16, bc16, bcast) = refs
    c = pl.program_id(1)
    n_chunks = pl.num_programs(1)
    L = CHUNK
    P = SSD_HEAD_DIM
    N = SSD_D_STATE
    G = SSD_GROUPS
    gw = (n_heads // G) * P
    d_inner = n_heads * P
    nc = SSD_CONV_W - 1

    @pl.when(c == 0)
    def _():
        tailx[...] = jnp.zeros_like(tailx)
        tailb[...] = jnp.zeros_like(tailb)
        if has_state:
            tailx[SUBLANES - nc:SUBLANES, :] = csx_ref[0]
            tailb[SUBLANES - nc:SUBLANES, :] = csb_ref[0]
            for g in range(G):
                stT[:, g * gw:(g + 1) * gw] = st_ref[0, g * gw:(g + 1) * gw, :].T
        else:
            stT[...] = jnp.zeros_like(stT)

    def conv(x_ref, tail, w_ref, b_ref, width, emit):
        cw = 4 * LANES
        for cb in range(width // cw):
            cs = slice(cb * cw, (cb + 1) * cw)
            xf = x_ref[0, :, cs].astype(F32)
            full = jnp.concatenate([tail[:, cs], xf], axis=0)
            tail[:, cs] = xf[L - SUBLANES:L, :]
            acc = b_ref[:, cs] + xf * w_ref[nc:nc + 1, cs]
            for k in range(1, SSD_CONV_W):
                acc = acc + pltpu.roll(full, k, 0)[SUBLANES:, :] * w_ref[nc - k:nc - k + 1, cs]
            emit(cs, _silu(acc))

    def emit_x(cs, act):
        xs_s[:, cs] = act
        xs16[:, cs] = act.astype(BF16)

    def emit_bc(cs, act):
        bc16[:, cs] = act.astype(BF16)

    conv(xs_ref, tailx, cwx_ref, cbx_ref, d_inner, emit_x)
    conv(bc_ref, tailb, cwb_ref, cbb_ref, 2 * G * N, emit_bc)

    dt3 = _softplus(sm_ref[0] + dtb_ref[...])
    s3 = _cumsum_rows(tril_ref, dt3 * (-jnp.exp(alog_ref[...])))
    lane_s = lax.broadcasted_iota(jnp.int32, (2 * L, SMALL_W), 1)
    x0, x1, x2 = _residuals(jnp.concatenate([s3, dt3], axis=0))
    split = jnp.where(lane_s < n_heads, x0, jnp.where(lane_s < 2 * n_heads, x1, x2)).astype(BF16)
    bcast[...] = _dot(split, exp_ref[...])
    hh = n_heads // 2
    s_t = s3[:, 0:2 * L].T
    dt_t = dt3[:, 0:2 * L].T
    srow[...] = jnp.concatenate([s_t[0:hh], s_t[hh:2 * hh]], axis=1)
    dtrow[...] = jnp.concatenate([dt_t[0:hh], dt_t[hh:2 * hh]], axis=1)

    row = lax.broadcasted_iota(jnp.int32, (L, 2 * P), 0)
    lane = lax.broadcasted_iota(jnp.int32, (L, 2 * P), 1)
    causal2 = (lane % P) <= row
    left = lane < P
    ppg = gw // (2 * P)

    for g in range(G):
        gs = slice(g * gw, (g + 1) * gw)
        bm16 = bc16[:, g * N:(g + 1) * N]
        cm16 = bc16[:, (G + g) * N:(G + g + 1) * N]
        cb2 = _dot(cm16, jnp.concatenate([bm16, bm16], axis=0), _NT)
        st_g = stT[:, gs]
        y_state = _dot(cm16, st_g.astype(BF16))
        ys = []
        for pp in range(ppg):
            p = g * ppg + pp
            sl = slice(p * 2 * P, (p + 1) * 2 * P)
            se = bcast[0:L, sl]
            dec = jnp.exp(jnp.where(causal2, se - srow[p:p + 1, :], -jnp.inf))
            m = (cb2 * dec * dtrow[p:p + 1, :]).astype(BF16)
            xp16 = xs16[:, sl]
            zero = jnp.zeros_like(xp16)
            rhs = jnp.concatenate([jnp.where(left, xp16, zero), jnp.where(left, zero, xp16)], axis=0)
            y = _dot(m, rhs)
            y = y + y_state[:, pp * 2 * P:(pp + 1) * 2 * P] * jnp.exp(se) + dsk_ref[:, sl] * xs_s[:, sl]
            ys.append(y)
        s_last = bcast[L - 1:L, gs]
        xdt_end = xs_s[:, gs] * bcast[L:2 * L, gs] * jnp.exp(s_last - bcast[0:L, gs])
        upd = _dot(bm16, xdt_end.astype(BF16), _TN)
        stT[:, gs] = st_g * jnp.exp(s_last) + upd

        yg = jnp.concatenate(ys, axis=1) * _silu(z_ref[0, :, gs].astype(F32))
        ms = jnp.mean(yg * yg, axis=-1, keepdims=True)
        y_ref[0, :, gs] = (yg * lax.rsqrt(ms + SSD_NORM_EPS) * gain_ref[:, gs]).astype(y_ref.dtype)

    @pl.when(c == n_chunks - 1)
    def _():
        for g in range(G):
            so_ref[0, g * gw:(g + 1) * gw, :] = stT[:, g * gw:(g + 1) * gw].T


def _ssd(proj3, small3, col, prm, conv_state, ssd_state):
    b, t, _ = proj3.shape
    n_heads = prm["n_heads"]
    d_inner = n_heads * SSD_HEAD_DIM
    bcw = 2 * SSD_GROUPS * SSD_D_STATE
    has_state = ssd_state is not None
    L = CHUNK
    const = lambda *shape: pl.BlockSpec(shape, lambda i, j: (0,) * len(shape))
    in_specs = [
        pl.BlockSpec((1, L, d_inner), lambda i, j: (i, j, col["xs"] // d_inner)),
        pl.BlockSpec((1, L, d_inner), lambda i, j: (i, j, col["z"] // d_inner)),
        pl.BlockSpec((1, L, bcw), lambda i, j: (i, j, col["bc"] // bcw)),
        pl.BlockSpec((1, L, SMALL_W), lambda i, j: (i, j, 0)),
        const(SSD_CONV_W, d_inner), const(1, d_inner), const(SSD_CONV_W, bcw), const(1, bcw),
        const(1, SMALL_W), const(1, SMALL_W), const(1, d_inner), const(1, d_inner),
        const(SMALL_W, d_inner), const(L, 3 * L),
    ]
    args = [proj3, proj3, proj3, small3, prm["conv_w_x"], prm["conv_b_x"], prm["conv_w_bc"], prm["conv_b_bc"],
            prm["dt_bias3"], prm["a_log3"], prm["d_skip_e"], prm["ssd_norm_gain"], prm["expand3"], prm["tril3"]]
    if has_state:
        nc = SSD_CONV_W - 1
        in_specs += [
            pl.BlockSpec((1, nc, d_inner), lambda i, j: (i, 0, 0)),
            pl.BlockSpec((1, nc, bcw), lambda i, j: (i, 0, 0)),
            pl.BlockSpec((1, d_inner, SSD_D_STATE), lambda i, j: (i, 0, 0)),
        ]
        args += [conv_state[:, :, :d_inner], conv_state[:, :, d_inner:],
                 ssd_state.reshape(b, d_inner, SSD_D_STATE)]
    y, st = pl.pallas_call(
        functools.partial(_ssd_kernel, has_state, n_heads),
        grid=(b, t // L),
        in_specs=in_specs,
        out_specs=[
            pl.BlockSpec((1, L, d_inner), lambda i, j: (i, j, 0)),
            pl.BlockSpec((1, d_inner, SSD_D_STATE), lambda i, j: (i, 0, 0)),
        ],
        out_shape=[
            jax.ShapeDtypeStruct((b, t, d_inner), BF16),
            jax.ShapeDtypeStruct((b, d_inner, SSD_D_STATE), F32),
        ],
        scratch_shapes=[
            pltpu.VMEM((SUBLANES, d_inner), F32),
            pltpu.VMEM((SUBLANES, bcw), F32),
            pltpu.VMEM((SSD_D_STATE, d_inner), F32),
            pltpu.VMEM((n_heads // 2, 2 * L), F32),
            pltpu.VMEM((n_heads // 2, 2 * L), F32),
            pltpu.VMEM((L, d_inner), F32),
            pltpu.VMEM((L, d_inner), BF16),
            pltpu.VMEM((L, bcw), BF16),
            pltpu.VMEM((2 * L, d_inner), F32),
        ],
        compiler_params=pltpu.CompilerParams(
            dimension_semantics=("arbitrary", "arbitrary"), vmem_limit_bytes=VMEM_LIMIT_BYTES),
    )(*args)
    return y, st.reshape(b, n_heads, SSD_HEAD_DIM, SSD_D_STATE)


def _gla_kernel(has_state, *refs):
    if has_state:
        (q_ref, k_ref, v_ref, gb_ref, sm_ref, wgk_ref, bgk_ref, gain_ref, tril_ref, st_ref,
         o_ref, so_ref, stT, bcs, ks) = refs
    else:
        (q_ref, k_ref, v_ref, gb_ref, sm_ref, wgk_ref, bgk_ref, gain_ref, tril_ref,
         o_ref, so_ref, stT, bcs, ks) = refs
    c = pl.program_id(1)
    n_chunks = pl.num_programs(1)
    L = CHUNK
    H = GLA_HEADS
    kd = q_ref.shape[2]
    vd = v_ref.shape[2]
    hk = kd // H
    hv = vd // H
    nb = L // SUB

    @pl.when(c == 0)
    def _():
        if has_state:
            for h in range(H):
                stT[h] = st_ref[0, h].T
        else:
            stT[...] = jnp.zeros_like(stT)

    sm = sm_ref[0]
    sm_hi = sm.astype(BF16)
    sm_lo = (sm - sm_hi.astype(F32)).astype(BF16)
    x = _dot(jnp.concatenate([sm_hi, sm_hi, sm_lo], axis=1), wgk_ref[...]) + bgk_ref[...]
    glog2 = (jnp.minimum(x, 0.0) - jnp.log(1.0 + jnp.exp(-jnp.abs(x)))) * (LOG2E / GLA_GATE_NORMALIZER)
    bc = _cumsum_rows(tril_ref, glog2)
    bcs[...] = bc
    bc_last = bc[L - 1:L, :]
    q = q_ref[0].astype(F32) * (hk ** -0.5)
    k = k_ref[0].astype(F32)
    ks[...] = k
    qe = (q * jnp.exp2(bc)).astype(BF16)
    k_end = (k * jnp.exp2(bc_last - bc)).astype(BF16)

    nt = SUB // SUBLANES
    lane = lax.broadcasted_iota(jnp.int32, (SUBLANES, L), 1)
    row = lax.broadcasted_iota(jnp.int32, (SUBLANES, L), 0)
    a_rows = [[None] * nb for _ in range(H)]
    for blk in range(nb):
        r0 = blk * SUB
        off = [None] * H
        if blk > 0:
            ref_row = bcs[r0 - 1:r0, :]
            qd = (q[r0:r0 + SUB, :] * jnp.exp2(bc[r0:r0 + SUB, :] - ref_row)).astype(BF16)
            kt = k[0:r0, :] * jnp.exp2(ref_row - bc[0:r0, :])
            kt = jnp.concatenate([kt, jnp.zeros((L - r0, kd), F32)], axis=0).astype(BF16)
            for h in range(H):
                off[h] = _dot(qd[:, h * hk:(h + 1) * hk], kt[:, h * hk:(h + 1) * hk], _NT)
        diag = [[jnp.zeros((SUBLANES, L), F32) for _ in range(nt)] for _ in range(H)]
        for jj in range(SUB):
            j = r0 + jj
            for ti in range(jj // SUBLANES, nt):
                t0 = r0 + ti * SUBLANES
                e = jnp.exp2(bc[t0:t0 + SUBLANES, :] - bcs[j:j + 1, :])
                pr = q[t0:t0 + SUBLANES, :] * e * ks[j:j + 1, :]
                keep = (lane == j) & (row + t0 >= j)
                for h in range(H):
                    col = jnp.sum(pr[:, h * hk:(h + 1) * hk], axis=-1, keepdims=True)
                    diag[h][ti] = jnp.where(keep, col, diag[h][ti])
        for h in range(H):
            d_h = jnp.concatenate(diag[h], axis=0)
            a_rows[h][blk] = d_h if off[h] is None else off[h] + d_h

    eb_last = jnp.exp2(bc_last)
    for h in range(H):
        a_h = jnp.concatenate(a_rows[h], axis=0).astype(BF16)
        v_h = v_ref[0, :, h * hv:(h + 1) * hv]
        st_h = stT[h]
        o = _dot(a_h, v_h) + _dot(qe[:, h * hk:(h + 1) * hk], st_h.astype(BF16), _NT)
        stT[h] = st_h * eb_last[:, h * hk:(h + 1) * hk] + _dot(v_h, k_end[:, h * hk:(h + 1) * hk], _TN)
        ms = jnp.mean(o * o, axis=-1, keepdims=True)
        on = o * lax.rsqrt(ms + NORM_EPS) * gain_ref[...]
        gate = _silu(gb_ref[0, :, h * hv:(h + 1) * hv].astype(F32))
        o_ref[0, :, h * hv:(h + 1) * hv] = (on * gate).astype(o_ref.dtype)

    @pl.when(c == n_chunks - 1)
    def _():
        for h in range(H):
            so_ref[0, h] = stT[h].T


def _gla(proj3, small3, col, prm, gla_state):
    b, t, _ = proj3.shape
    kd = prm["gla_key_dim"]
    vd = prm["gla_val_dim"]
    H = GLA_HEADS
    hk, hv = kd // H, vd // H
    has_state = gla_state is not None
    L = CHUNK
    const = lambda *shape: pl.BlockSpec(shape, lambda i, j: (0,) * len(shape))
    in_specs = [
        pl.BlockSpec((1, L, kd), lambda i, j: (i, j, col["q"] // kd)),
        pl.BlockSpec((1, L, kd), lambda i, j: (i, j, col["k"] // kd)),
        pl.BlockSpec((1, L, vd), lambda i, j: (i, j, col["v"] // vd)),
        pl.BlockSpec((1, L, vd), lambda i, j: (i, j, col["g_b"] // vd)),
        pl.BlockSpec((1, L, SMALL_W), lambda i, j: (i, j, 0)),
        const(3 * SMALL_W, kd), const(1, kd), const(1, hv), const(L, 3 * L),
    ]
    args = [proj3, proj3, proj3, proj3, small3, prm["gk_w3"], prm["gk_b"], prm["gla_norm_gain"], prm["tril3"]]
    if has_state:
        in_specs.append(pl.BlockSpec((1, H, hk, hv), lambda i, j: (i, 0, 0, 0)))
        args.append(gla_state)
    return pl.pallas_call(
        functools.partial(_gla_kernel, has_state),
        grid=(b, t // L),
        in_specs=in_specs,
        out_specs=[
            pl.BlockSpec((1, L, vd), lambda i, j: (i, j, 0)),
            pl.BlockSpec((1, H, hk, hv), lambda i, j: (i, 0, 0, 0)),
        ],
        out_shape=[
            jax.ShapeDtypeStruct((b, t, vd), BF16),
            jax.ShapeDtypeStruct((b, H, hk, hv), F32),
        ],
        scratch_shapes=[
            pltpu.VMEM((H, hv, hk), F32),
            pltpu.VMEM((L, kd), F32),
            pltpu.VMEM((L, kd), F32),
        ],
        compiler_params=pltpu.CompilerParams(
            dimension_semantics=("arbitrary", "arbitrary"), vmem_limit_bytes=VMEM_LIMIT_BYTES),
    )(*args)


def _merge_kernel(tn, ys_ref, yg_ref, g_ref, wbs_ref, wbg_ref, o_ref):
    d = o_ref.shape[1]
    for n in range(d // tn):
        ns = slice(n * tn, (n + 1) * tn)
        a = _dot(ys_ref[...], wbs_ref[:, ns])
        b = _dot(yg_ref[...], wbg_ref[:, ns])
        gate_s = _sigmoid(g_ref[:, ns].astype(F32))
        gate_g = _sigmoid(g_ref[:, d + n * tn:d + (n + 1) * tn].astype(F32))
        o_ref[:, ns] = (gate_s * a + gate_g * b).astype(o_ref.dtype)


def _post_kernel(mx_ref, x_ref, wo_ref, g_ref, o_ref):
    cc = _dot(mx_ref[...], wo_ref[...])
    ms = jnp.mean(cc * cc, axis=-1, keepdims=True)
    o_ref[...] = x_ref[...] + cc * lax.rsqrt(ms + NORM_EPS) * g_ref[...]


def _outproj(y_ssd, y_gla, proj, col, x2, prm, tm_merge, tn, tm_post):
    m, d = x2.shape
    d_inner = y_ssd.shape[1]
    vd = y_gla.shape[1]
    g0 = col["merge"] // (2 * d)
    resident = lambda r, c_: pl.BlockSpec((r, c_), lambda i: (0, 0), pipeline_mode=pl.Buffered(1))
    mixed = pl.pallas_call(
        functools.partial(_merge_kernel, tn),
        grid=(m // tm_merge,),
        in_specs=[
            pl.BlockSpec((tm_merge, d_inner), lambda i: (i, 0)),
            pl.BlockSpec((tm_merge, vd), lambda i: (i, 0)),
            pl.BlockSpec((tm_merge, 2 * d), lambda i: (i, g0)),
            resident(d_inner, d), resident(vd, d),
        ],
        out_specs=pl.BlockSpec((tm_merge, d), lambda i: (i, 0)),
        out_shape=jax.ShapeDtypeStruct((m, d), BF16),
        compiler_params=pltpu.CompilerParams(
            dimension_semantics=("arbitrary",), vmem_limit_bytes=VMEM_LIMIT_BYTES),
    )(y_ssd, y_gla, proj, prm["w_branch_ssd"], prm["w_branch_gla"])
    return pl.pallas_call(
        _post_kernel,
        grid=(m // tm_post,),
        in_specs=[
            pl.BlockSpec((tm_post, d), lambda i: (i, 0)),
            pl.BlockSpec((tm_post, d), lambda i: (i, 0)),
            pl.BlockSpec((d, d), lambda i: (0, 0)),
            pl.BlockSpec((1, d), lambda i: (0, 0)),
        ],
        out_specs=pl.BlockSpec((tm_post, d), lambda i: (i, 0)),
        out_shape=jax.ShapeDtypeStruct((m, d), F32),
        compiler_params=pltpu.CompilerParams(
            dimension_semantics=("arbitrary",), vmem_limit_bytes=VMEM_LIMIT_BYTES),
    )(mixed, x2, prm["w_out"], prm["norm_post_gain"])


def _prepare(norm_pre_gain, w_in, conv_w, conv_b, dt_bias, a_log, d_skip, ssd_norm_gain, gla_gk_w, gla_gk_b,
             gla_norm_gain, w_branch_ssd, w_branch_gla, w_out, norm_post_gain):
    d = w_in.shape[0]
    n_heads = dt_bias.shape[0]
    d_inner = n_heads * SSD_HEAD_DIM
    bcw = 2 * SSD_GROUPS * SSD_D_STATE
    kd = gla_gk_w.shape[1]
    vd = w_branch_gla.shape[0]
    assert DT_COPIES * n_heads + GLA_GATE_RANK <= SMALL_W
    sizes = (d_inner, d_inner + bcw, n_heads, kd, kd, vd, vd, GLA_GATE_RANK, 2 * d)
    offs = np.concatenate([[0], np.cumsum(sizes)])
    seg = lambda i: w_in[:, offs[i]:offs[i + 1]]
    z_w, xbc_w, dt_w, q_w, k_w, v_w, gb_w, gk_w, mg_w = [seg(i) for i in range(9)]
    perm = np.concatenate([np.arange(0, n_heads, 2), np.arange(1, n_heads, 2)])
    dt_w = dt_w[:, perm]
    gk0 = DT_COPIES * n_heads
    pad = SMALL_W - gk0 - GLA_GATE_RANK
    w_small = jnp.concatenate([dt_w] * DT_COPIES + [gk_w, jnp.zeros((d, pad), w_in.dtype)], axis=1).astype(BF16)
    pieces = [("xs", xbc_w[:, :d_inner]), ("z", z_w), ("merge", mg_w), ("bc", xbc_w[:, d_inner:]),
              ("v", v_w), ("g_b", gb_w), ("q", q_w), ("k", k_w)]
    col, o = {}, 0
    for name, p in pieces:
        assert o % p.shape[1] == 0, name
        col[name] = o
        o += p.shape[1]
    w_main = jnp.concatenate([p for _, p in pieces], axis=1).astype(BF16)

    def small_row(v, fill):
        return jnp.concatenate([v[perm]] * DT_COPIES + [jnp.full((SMALL_W - gk0,), fill, v.dtype)])[None, :]

    expand = np.zeros((SMALL_W, d_inner), np.float32)
    for r in range(DT_COPIES):
        for c_ in range(n_heads):
            expand[r * n_heads + c_, perm[c_] * SSD_HEAD_DIM:(perm[c_] + 1) * SSD_HEAD_DIM] = 1.0
    tril = np.tril(np.ones((CHUNK, CHUNK), np.float32))
    gk_w_pad = jnp.zeros((SMALL_W, kd), F32).at[gk0:gk0 + GLA_GATE_RANK].set(gla_gk_w)
    gk_hi = gk_w_pad.astype(BF16)
    gk_lo = (gk_w_pad - gk_hi.astype(F32)).astype(BF16)
    prm = dict(
        n_heads=n_heads, gla_key_dim=kd, gla_val_dim=vd,
        norm_pre_gain=norm_pre_gain[None, :], w_main=w_main, w_small=w_small,
        conv_w_x=conv_w[:, :d_inner], conv_w_bc=conv_w[:, d_inner:],
        conv_b_x=conv_b[None, :d_inner], conv_b_bc=conv_b[None, d_inner:],
        dt_bias3=small_row(dt_bias, 0.0), a_log3=small_row(a_log, 0.0),
        d_skip_e=jnp.repeat(d_skip, SSD_HEAD_DIM)[None, :], ssd_norm_gain=ssd_norm_gain[None, :],
        expand3=jnp.asarray(expand, BF16), tril3=jnp.asarray(np.concatenate([tril] * 3, axis=1), BF16),
        gk_w3=jnp.concatenate([gk_hi, gk_lo, gk_hi], axis=0), gk_b=gla_gk_b[None, :],
        gla_norm_gain=gla_norm_gain[None, :],
        w_branch_ssd=w_branch_ssd.astype(BF16), w_branch_gla=w_branch_gla.astype(BF16),
        w_out=w_out.astype(BF16), norm_post_gain=norm_post_gain[None, :],
    )
    return prm, col


def _row_tile(m, want):
    tm = min(m, want)
    assert m % tm == 0
    return tm


def _layer(x, conv_state, ssd_state, gla_state, prm, col):
    b, t, d = x.shape
    assert t % CHUNK == 0 and t >= SSD_CONV_W - 1
    m = b * t
    x2 = x.reshape(m, d)
    proj, small = _inproj(x2, prm["norm_pre_gain"], prm["w_main"], prm["w_small"], _row_tile(m, 1024), 1024)
    proj3 = proj.reshape(b, t, -1)
    small3 = small.reshape(b, t, -1)
    d_inner = prm["n_heads"] * SSD_HEAD_DIM
    bcw = 2 * SSD_GROUPS * SSD_D_STATE
    nc = SSD_CONV_W - 1
    new_conv = jnp.concatenate(
        [proj3[:, t - nc:, col["xs"]:col["xs"] + d_inner], proj3[:, t - nc:, col["bc"]:col["bc"] + bcw]],
        axis=-1).astype(x.dtype)
    y_ssd, new_ssd = _ssd(proj3, small3, col, prm, conv_state, ssd_state)
    y_gla, new_gla = _gla(proj3, small3, col, prm, gla_state)
    y = _outproj(y_ssd.reshape(m, -1), y_gla.reshape(m, -1), proj, col, x2, prm,
                 _row_tile(m, 256), 1024, _row_tile(m, 512))
    return y.reshape(b, t, d), new_conv, new_ssd, new_gla


def kernel(x_prompt, x_sample, state_conv_ssd, state_ssd, state_gla, norm_pre_gain, w_in, conv_w, conv_b, dt_bias,
           a_log, d_skip, ssd_norm_gain, gla_gk_w, gla_gk_b, gla_norm_gain, w_branch_ssd, w_branch_gla, w_out,
           norm_post_gain):
    depth = w_in.shape[0]
    yp, ys = x_prompt, x_sample
    outs = [[] for _ in range(6)]
    for layer in range(depth):
        prm, col = _prepare(norm_pre_gain[layer], w_in[layer], conv_w[layer], conv_b[layer], dt_bias[layer],
                            a_log[layer], d_skip[layer], ssd_norm_gain[layer], gla_gk_w[layer], gla_gk_b[layer],
                            gla_norm_gain[layer], w_branch_ssd[layer], w_branch_gla[layer], w_out[layer],
                            norm_post_gain[layer])
        yp, c_p, s_p, g_p = _layer(yp, None, None, None, prm, col)
        ys, c_s, s_s, g_s = _layer(ys, state_conv_ssd[layer], state_ssd[layer], state_gla[layer], prm, col)
        for lst, val in zip(outs, (c_p, s_p, g_p, c_s, s_s, g_s)):
            lst.append(val)
    return (yp, ys) + tuple(jnp.stack(o) for o in outs)
```

```python
import functools
import math

import numpy as np
import jax
import jax.numpy as jnp
from jax import lax
from jax.experimental import pallas as pl
from jax.experimental.pallas import tpu as pltpu

F32 = jnp.float32
BF16 = jnp.bfloat16

LANES = 128
SUBLANES = 8
VMEM_LIMIT_BYTES = 56 * 1024 * 1024

CHUNK = 64
SSD_HEAD_DIM = 64
SSD_GROUPS = 8
SSD_D_STATE = 128
SSD_CONV_W = 4
GLA_HEADS = 4
GLA_GATE_RANK = 16
GLA_GATE_NORMALIZER = 16.0
NORM_EPS = 1e-6
SSD_NORM_EPS = 1e-5
SUB = 16
SMALL_W = 256
DT_COPIES = 2
LOG2E = math.log2(math.e)


def _dot(a, b, dims=(((1,), (0,)), ((), ()))):
    return lax.dot_general(a, b, dims, preferred_element_type=F32)


_NT = (((1,), (1,)), ((), ()))
_TN = (((0,), (0,)), ((), ()))


def _sigmoid(x):
    return 0.5 * jnp.tanh(0.5 * x) + 0.5


def _silu(x):
    h = 0.5 * x
    return h * jnp.tanh(h) + h


def _softplus(x):
    return jnp.maximum(x, 0.0) + jnp.log(1.0 + jnp.exp(-jnp.abs(x)))


def _residuals(x):
    r1 = x - x.astype(BF16).astype(F32)
    r2 = r1 - r1.astype(BF16).astype(F32)
    return x, r1, r2


def _cumsum_rows(tril3_ref, x):
    parts = [p.astype(BF16) for p in _residuals(x)]
    return _dot(tril3_ref[...], jnp.concatenate(parts, axis=0))


def _inproj_kernel(x_ref, g_ref, w_ref, ws_ref, o_ref, os_ref, h_ref):
    @pl.when(pl.program_id(1) == 0)
    def _():
        x = x_ref[...]
        ms = jnp.mean(x * x, axis=-1, keepdims=True)
        h_ref[...] = (x * lax.rsqrt(ms + NORM_EPS) * g_ref[...]).astype(BF16)
        os_ref[...] = _dot(h_ref[...], ws_ref[...])

    o_ref[...] = _dot(h_ref[...], w_ref[...]).astype(o_ref.dtype)


def _inproj(x2, gain, w_main, w_small, tm, tn):
    m, d = x2.shape
    n = w_main.shape[1]
    ns = w_small.shape[1]
    return pl.pallas_call(
        _inproj_kernel,
        grid=(m // tm, n // tn),
        in_specs=[
            pl.BlockSpec((tm, d), lambda i, j: (i, 0)),
            pl.BlockSpec((1, d), lambda i, j: (0, 0)),
            pl.BlockSpec((d, tn), lambda i, j: (0, j)),
            pl.BlockSpec((d, ns), lambda i, j: (0, 0)),
        ],
        out_specs=[
            pl.BlockSpec((tm, tn), lambda i, j: (i, j)),
            pl.BlockSpec((tm, ns), lambda i, j: (i, 0)),
        ],
        out_shape=[jax.ShapeDtypeStruct((m, n), BF16), jax.ShapeDtypeStruct((m, ns), F32)],
        scratch_shapes=[pltpu.VMEM((tm, d), BF16)],
        compiler_params=pltpu.CompilerParams(
            dimension_semantics=("arbitrary", "arbitrary"), vmem_limit_bytes=VMEM_LIMIT_BYTES),
    )(x2, gain, w_main, w_small)


def _ssd_kernel(has_state, n_heads, *refs):
    if has_state:
        (xs_ref, z_ref, bc_ref, sm_ref, cwx_ref, cbx_ref, cwb_ref, cbb_ref, dtb_ref, alog_ref, dsk_ref,
         gain_ref, exp_ref, tril_ref, shift_ref, csx_ref, csb_ref, st_ref,
         y_ref, so_ref, tailx, tailb, stT, srow, dtrow, xs_s, xs16, bc16, bcast) = refs
    else:
        (xs_ref, z_ref, bc_ref, sm_ref, cwx_ref, cbx_ref, cwb_ref, cbb_ref, dtb_ref, alog_ref, dsk_ref,
         gain_ref, exp_ref, tril_ref, shift_ref,
         y_ref, so_ref, tailx, tailb, stT, srow, dtrow, xs_s, xs16, bc16, bcast) = refs
    c = pl.program_id(1)
    n_chunks = pl.num_programs(1)
    L = CHUNK
    P = SSD_HEAD_DIM
    N = SSD_D_STATE
    G = SSD_GROUPS
    gw = (n_heads // G) * P
    d_inner = n_heads * P
    nc = SSD_CONV_W - 1

    @pl.when(c == 0)
    def _():
        tailx[...] = jnp.zeros_like(tailx)
        tailb[...] = jnp.zeros_like(tailb)
        if has_state:
            tailx[SUBLANES - nc:SUBLANES, :] = csx_ref[0]
            tailb[SUBLANES - nc:SUBLANES, :] = csb_ref[0]
            for g in range(G):
                stT[:, g * gw:(g + 1) * gw] = st_ref[0, g * gw:(g + 1) * gw, :].T
        else:
            stT[...] = jnp.zeros_like(stT)

    cw = 4 * LANES
    row8 = lax.broadcasted_iota(jnp.int32, (SUBLANES, cw), 0)

    def conv(x_ref, tail, w_ref, b_ref, width, emit):
        for cb in range(width // cw):
            cs = slice(cb * cw, (cb + 1) * cw)
            x16 = x_ref[0, :, cs]
            xf = x16.astype(F32)
            sh = _dot(shift_ref[...], x16)
            acc = b_ref[:, cs] + xf * w_ref[nc:nc + 1, cs]
            prev = tail[:, cs]
            fix = jnp.zeros((SUBLANES, cw), F32)
            for k in range(1, SSD_CONV_W):
                w_k = w_ref[nc - k:nc - k + 1, cs]
                acc = acc + sh[(k - 1) * L:k * L, :] * w_k
                fix = fix + jnp.where(row8 < k, pltpu.roll(prev, k, 0), 0.0) * w_k
            tail[:, cs] = xf[L - SUBLANES:L, :]
            acc = jnp.concatenate([acc[0:SUBLANES, :] + fix, acc[SUBLANES:, :]], axis=0)
            emit(cs, _silu(acc))

    def emit_x(cs, act):
        xs_s[:, cs] = act
        xs16[:, cs] = act.astype(BF16)

    def emit_bc(cs, act):
        bc16[:, cs] = act.astype(BF16)

    conv(xs_ref, tailx, cwx_ref, cbx_ref, d_inner, emit_x)
    conv(bc_ref, tailb, cwb_ref, cbb_ref, 2 * G * N, emit_bc)

    dt3 = _softplus(sm_ref[0] + dtb_ref[...])
    s3 = _cumsum_rows(tril_ref, dt3 * (-jnp.exp(alog_ref[...])))
    w3 = dt3 * jnp.exp(s3[L - 1:L, :] - s3)
    lane_s = lax.broadcasted_iota(jnp.int32, (2 * L, SMALL_W), 1)
    x0, x1, _ = _residuals(jnp.concatenate([s3, w3], axis=0))
    split = jnp.where(lane_s < n_heads, x0, x1)[:, 0:DT_COPIES * n_heads].astype(BF16)
    bcast[...] = _dot(split, exp_ref[...])
    hh = n_heads // 2
    s_t = s3[:, 0:2 * L].T
    dt_t = dt3[:, 0:2 * L].T
    srow[...] = jnp.concatenate([s_t[0:hh], s_t[hh:2 * hh]], axis=1)
    dtrow[...] = jnp.concatenate([dt_t[0:hh], dt_t[hh:2 * hh]], axis=1)

    row = lax.broadcasted_iota(jnp.int32, (L, 2 * P), 0)
    lane = lax.broadcasted_iota(jnp.int32, (L, 2 * P), 1)
    causal2 = (lane % P) <= row
    left = lane < P
    ppg = gw // (2 * P)

    for g in range(G):
        gs = slice(g * gw, (g + 1) * gw)
        bm16 = bc16[:, g * N:(g + 1) * N]
        cm16 = bc16[:, (G + g) * N:(G + g + 1) * N]
        cb2 = _dot(cm16, jnp.concatenate([bm16, bm16], axis=0), _NT)
        st_g = stT[:, gs]
        y_state = _dot(cm16, st_g.astype(BF16))
        ys = []
        for pp in range(ppg):
            p = g * ppg + pp
            sl = slice(p * 2 * P, (p + 1) * 2 * P)
            se = bcast[0:L, sl]
            dec = jnp.exp(jnp.where(causal2, se - srow[p:p + 1, :], -jnp.inf))
            m = (cb2 * dec * dtrow[p:p + 1, :]).astype(BF16)
            xp16 = xs16[:, sl]
            zero = jnp.zeros_like(xp16)
            rhs = jnp.concatenate([jnp.where(left, xp16, zero), jnp.where(left, zero, xp16)], axis=0)
            y = _dot(m, rhs)
            y = y + y_state[:, pp * 2 * P:(pp + 1) * 2 * P] * jnp.exp(se) + dsk_ref[:, sl] * xs_s[:, sl]
            ys.append(y)
        s_last = bcast[L - 1:L, gs]
        xdt_end = xs_s[:, gs] * bcast[L:2 * L, gs]
        upd = _dot(bm16, xdt_end.astype(BF16), _TN)
        stT[:, gs] = st_g * jnp.exp(s_last) + upd

        yg = jnp.concatenate(ys, axis=1) * _silu(z_ref[0, :, gs].astype(F32))
        ms = jnp.mean(yg * yg, axis=-1, keepdims=True)
        y_ref[0, :, gs] = (yg * lax.rsqrt(ms + SSD_NORM_EPS) * gain_ref[:, gs]).astype(y_ref.dtype)

    @pl.when(c == n_chunks - 1)
    def _():
        for g in range(G):
            so_ref[0, g * gw:(g + 1) * gw, :] = stT[:, g * gw:(g + 1) * gw].T


def _ssd(proj3, small3, col, prm, conv_state, ssd_state):
    b, t, _ = proj3.shape
    n_heads = prm["n_heads"]
    d_inner = n_heads * SSD_HEAD_DIM
    bcw = 2 * SSD_GROUPS * SSD_D_STATE
    has_state = ssd_state is not None
    L = CHUNK
    const = lambda *shape: pl.BlockSpec(shape, lambda i, j: (0,) * len(shape))
    in_specs = [
        pl.BlockSpec((1, L, d_inner), lambda i, j: (i, j, col["xs"] // d_inner)),
        pl.BlockSpec((1, L, d_inner), lambda i, j: (i, j, col["z"] // d_inner)),
        pl.BlockSpec((1, L, bcw), lambda i, j: (i, j, col["bc"] // bcw)),
        pl.BlockSpec((1, L, SMALL_W), lambda i, j: (i, j, 0)),
        const(SSD_CONV_W, d_inner), const(1, d_inner), const(SSD_CONV_W, bcw), const(1, bcw),
        const(1, SMALL_W), const(1, SMALL_W), const(1, d_inner), const(1, d_inner),
        const(DT_COPIES * n_heads, d_inner), const(L, 3 * L), const((SSD_CONV_W - 1) * L, L),
    ]
    args = [proj3, proj3, proj3, small3, prm["conv_w_x"], prm["conv_b_x"], prm["conv_w_bc"], prm["conv_b_bc"],
            prm["dt_bias3"], prm["a_log3"], prm["d_skip_e"], prm["ssd_norm_gain"], prm["expand3"], prm["tril3"],
            prm["shift"]]
    if has_state:
        nc = SSD_CONV_W - 1
        in_specs += [
            pl.BlockSpec((1, nc, d_inner), lambda i, j: (i, 0, 0)),
            pl.BlockSpec((1, nc, bcw), lambda i, j: (i, 0, 0)),
            pl.BlockSpec((1, d_inner, SSD_D_STATE), lambda i, j: (i, 0, 0)),
        ]
        args += [conv_state[:, :, :d_inner], conv_state[:, :, d_inner:],
                 ssd_state.reshape(b, d_inner, SSD_D_STATE)]
    y, st = pl.pallas_call(
        functools.partial(_ssd_kernel, has_state, n_heads),
        grid=(b, t // L),
        in_specs=in_specs,
        out_specs=[
            pl.BlockSpec((1, L, d_inner), lambda i, j: (i, j, 0)),
            pl.BlockSpec((1, d_inner, SSD_D_STATE), lambda i, j: (i, 0, 0)),
        ],
        out_shape=[
            jax.ShapeDtypeStruct((b, t, d_inner), BF16),
            jax.ShapeDtypeStruct((b, d_inner, SSD_D_STATE), F32),
        ],
        scratch_shapes=[
            pltpu.VMEM((SUBLANES, d_inner), F32),
            pltpu.VMEM((SUBLANES, bcw), F32),
            pltpu.VMEM((SSD_D_STATE, d_inner), F32),
            pltpu.VMEM((n_heads // 2, 2 * L), F32),
            pltpu.VMEM((n_heads // 2, 2 * L), F32),
            pltpu.VMEM((L, d_inner), F32),
            pltpu.VMEM((L, d_inner), BF16),
            pltpu.VMEM((L, bcw), BF16),
            pltpu.VMEM((2 * L, d_inner), F32),
        ],
        compiler_params=pltpu.CompilerParams(
            dimension_semantics=("arbitrary", "arbitrary"), vmem_limit_bytes=VMEM_LIMIT_BYTES),
    )(*args)
    return y, st.reshape(b, n_heads, SSD_HEAD_DIM, SSD_D_STATE)


def _gla_kernel(has_state, *refs):
    if has_state:
        (q_ref, k_ref, v_ref, gb_ref, sm_ref, wgk_ref, bgk_ref, gain_ref, tril_ref, st_ref,
         o_ref, so_ref, stT, bcs, ks, qs) = refs
    else:
        (q_ref, k_ref, v_ref, gb_ref, sm_ref, wgk_ref, bgk_ref, gain_ref, tril_ref,
         o_ref, so_ref, stT, bcs, ks, qs) = refs
    c = pl.program_id(1)
    n_chunks = pl.num_programs(1)
    L = CHUNK
    H = GLA_HEADS
    kd = q_ref.shape[2]
    vd = v_ref.shape[2]
    hk = kd // H
    hv = vd // H
    nb = L // SUB

    @pl.when(c == 0)
    def _():
        if has_state:
            for h in range(H):
                stT[h] = st_ref[0, h].T
        else:
            stT[...] = jnp.zeros_like(stT)

    sm = sm_ref[0]
    sm_hi = sm.astype(BF16)
    sm_lo = (sm - sm_hi.astype(F32)).astype(BF16)
    x = _dot(jnp.concatenate([sm_hi, sm_hi, sm_lo], axis=1), wgk_ref[...]) + bgk_ref[...]
    glog2 = (jnp.minimum(x, 0.0) - jnp.log(1.0 + jnp.exp(-jnp.abs(x)))) * (LOG2E / GLA_GATE_NORMALIZER)
    bc = _cumsum_rows(tril_ref, glog2)
    bcs[...] = bc
    bc_last = bc[L - 1:L, :]
    q = q_ref[0].astype(F32) * (hk ** -0.5)
    k = k_ref[0].astype(F32)
    ks[...] = k
    qs[...] = q
    qe = (q * jnp.exp2(bc)).astype(BF16)
    k_end = (k * jnp.exp2(bc_last - bc)).astype(BF16)

    nt = SUB // SUBLANES
    lane = lax.broadcasted_iota(jnp.int32, (SUBLANES, L), 1)
    row = lax.broadcasted_iota(jnp.int32, (SUBLANES, L), 0)
    a_rows = [[None] * nb for _ in range(H)]
    for blk in range(nb):
        r0 = blk * SUB
        off = [None] * H
        if blk > 0:
            ref_row = bcs[r0 - 1:r0, :]
            qd = (q[r0:r0 + SUB, :] * jnp.exp2(bc[r0:r0 + SUB, :] - ref_row)).astype(BF16)
            kt = k[0:r0, :] * jnp.exp2(ref_row - bc[0:r0, :])
            kt = jnp.concatenate([kt, jnp.zeros((L - r0, kd), F32)], axis=0).astype(BF16)
            for h in range(H):
                off[h] = _dot(qd[:, h * hk:(h + 1) * hk], kt[:, h * hk:(h + 1) * hk], _NT)
        diag = [[jnp.zeros((SUBLANES, L), F32) for _ in range(nt)] for _ in range(H)]
        for jj in range(SUB):
            j = r0 + jj
            for ti in range(jj // SUBLANES, nt):
                t0 = r0 + ti * SUBLANES
                keep = (lane == j) & (row + t0 >= j)
                for h in range(H):
                    tot = None
                    for lt in range(hk // LANES):
                        ls = slice(h * hk + lt * LANES, h * hk + (lt + 1) * LANES)
                        e = jnp.exp2(bcs[t0:t0 + SUBLANES, ls] - bcs[j:j + 1, ls])
                        pr = qs[t0:t0 + SUBLANES, ls] * e * ks[j:j + 1, ls]
                        tot = pr if tot is None else tot + pr
                    col = jnp.sum(tot, axis=-1, keepdims=True)
                    diag[h][ti] = jnp.where(keep, col, diag[h][ti])
        for h in range(H):
            d_h = jnp.concatenate(diag[h], axis=0)
            a_rows[h][blk] = d_h if off[h] is None else off[h] + d_h

    eb_last = jnp.exp2(bc_last)
    for h in range(H):
        a_h = jnp.concatenate(a_rows[h], axis=0).astype(BF16)
        v_h = v_ref[0, :, h * hv:(h + 1) * hv]
        st_h = stT[h]
        o = _dot(a_h, v_h) + _dot(qe[:, h * hk:(h + 1) * hk], st_h.astype(BF16), _NT)
        stT[h] = st_h * eb_last[:, h * hk:(h + 1) * hk] + _dot(v_h, k_end[:, h * hk:(h + 1) * hk], _TN)
        ms = jnp.mean(o * o, axis=-1, keepdims=True)
        on = o * lax.rsqrt(ms + NORM_EPS) * gain_ref[...]
        gate = _silu(gb_ref[0, :, h * hv:(h + 1) * hv].astype(F32))
        o_ref[0, :, h * hv:(h + 1) * hv] = (on * gate).astype(o_ref.dtype)

    @pl.when(c == n_chunks - 1)
    def _():
        for h in range(H):
            so_ref[0, h] = stT[h].T


def _gla(proj3, small3, col, prm, gla_state):
    b, t, _ = proj3.shape
    kd = prm["gla_key_dim"]
    vd = prm["gla_val_dim"]
    H = GLA_HEADS
    hk, hv = kd // H, vd // H
    has_state = gla_state is not None
    L = CHUNK
    const = lambda *shape: pl.BlockSpec(shape, lambda i, j: (0,) * len(shape))
    in_specs = [
        pl.BlockSpec((1, L, kd), lambda i, j: (i, j, col["q"] // kd)),
        pl.BlockSpec((1, L, kd), lambda i, j: (i, j, col["k"] // kd)),
        pl.BlockSpec((1, L, vd), lambda i, j: (i, j, col["v"] // vd)),
        pl.BlockSpec((1, L, vd), lambda i, j: (i, j, col["g_b"] // vd)),
        pl.BlockSpec((1, L, SMALL_W), lambda i, j: (i, j, 0)),
        const(3 * SMALL_W, kd), const(1, kd), const(1, hv), const(L, 3 * L),
    ]
    args = [proj3, proj3, proj3, proj3, small3, prm["gk_w3"], prm["gk_b"], prm["gla_norm_gain"], prm["tril3"]]
    if has_state:
        in_specs.append(pl.BlockSpec((1, H, hk, hv), lambda i, j: (i, 0, 0, 0)))
        args.append(gla_state)
    return pl.pallas_call(
        functools.partial(_gla_kernel, has_state),
        grid=(b, t // L),
        in_specs=in_specs,
        out_specs=[
            pl.BlockSpec((1, L, vd), lambda i, j: (i, j, 0)),
            pl.BlockSpec((1, H, hk, hv), lambda i, j: (i, 0, 0, 0)),
        ],
        out_shape=[
            jax.ShapeDtypeStruct((b, t, vd), BF16),
            jax.ShapeDtypeStruct((b, H, hk, hv), F32),
        ],
        scratch_shapes=[
            pltpu.VMEM((H, hv, hk), F32),
            pltpu.VMEM((L, kd), F32),
            pltpu.VMEM((L, kd), F32),
            pltpu.VMEM((L, kd), F32),
        ],
        compiler_params=pltpu.CompilerParams(
            dimension_semantics=("arbitrary", "arbitrary"), vmem_limit_bytes=VMEM_LIMIT_BYTES),
    )(*args)


def _merge_kernel(tn, ys_ref, yg_ref, g_ref, wbs_ref, wbg_ref, o_ref):
    d = o_ref.shape[1]
    for n in range(d // tn):
        ns = slice(n * tn, (n + 1) * tn)
        a = _dot(ys_ref[...], wbs_ref[:, ns])
        b = _dot(yg_ref[...], wbg_ref[:, ns])
        gate_s = _sigmoid(g_ref[:, ns].astype(F32))
        gate_g = _sigmoid(g_ref[:, d + n * tn:d + (n + 1) * tn].astype(F32))
        o_ref[:, ns] = (gate_s * a + gate_g * b).astype(o_ref.dtype)


def _post_kernel(mx_ref, x_ref, wo_ref, g_ref, o_ref):
    cc = _dot(mx_ref[...], wo_ref[...])
    ms = jnp.mean(cc * cc, axis=-1, keepdims=True)
    o_ref[...] = x_ref[...] + cc * lax.rsqrt(ms + NORM_EPS) * g_ref[...]


def _outproj(y_ssd, y_gla, proj, col, x2, prm, tm_merge, tn, tm_post):
    m, d = x2.shape
    d_inner = y_ssd.shape[1]
    vd = y_gla.shape[1]
    g0 = col["merge"] // (2 * d)
    resident = lambda r, c_: pl.BlockSpec((r, c_), lambda i: (0, 0), pipeline_mode=pl.Buffered(1))
    mixed = pl.pallas_call(
        functools.partial(_merge_kernel, tn),
        grid=(m // tm_merge,),
        in_specs=[
            pl.BlockSpec((tm_merge, d_inner), lambda i: (i, 0)),
            pl.BlockSpec((tm_merge, vd), lambda i: (i, 0)),
            pl.BlockSpec((tm_merge, 2 * d), lambda i: (i, g0)),
            resident(d_inner, d), resident(vd, d),
        ],
        out_specs=pl.BlockSpec((tm_merge, d), lambda i: (i, 0)),
        out_shape=jax.ShapeDtypeStruct((m, d), BF16),
        compiler_params=pltpu.CompilerParams(
            dimension_semantics=("arbitrary",), vmem_limit_bytes=VMEM_LIMIT_BYTES),
    )(y_ssd, y_gla, proj, prm["w_branch_ssd"], prm["w_branch_gla"])
    return pl.pallas_call(
        _post_kernel,
        grid=(m // tm_post,),
        in_specs=[
            pl.BlockSpec((tm_post, d), lambda i: (i, 0)),
            pl.BlockSpec((tm_post, d), lambda i: (i, 0)),
            pl.BlockSpec((d, d), lambda i: (0, 0)),
            pl.BlockSpec((1, d), lambda i: (0, 0)),
        ],
        out_specs=pl.BlockSpec((tm_post, d), lambda i: (i, 0)),
        out_shape=jax.ShapeDtypeStruct((m, d), F32),
        compiler_params=pltpu.CompilerParams(
            dimension_semantics=("arbitrary",), vmem_limit_bytes=VMEM_LIMIT_BYTES),
    )(mixed, x2, prm["w_out"], prm["norm_post_gain"])


def _prepare(norm_pre_gain, w_in, conv_w, conv_b, dt_bias, a_log, d_skip, ssd_norm_gain, gla_gk_w, gla_gk_b,
             gla_norm_gain, w_branch_ssd, w_branch_gla, w_out, norm_post_gain):
    d = w_in.shape[0]
    n_heads = dt_bias.shape[0]
    d_inner = n_heads * SSD_HEAD_DIM
    bcw = 2 * SSD_GROUPS * SSD_D_STATE
    kd = gla_gk_w.shape[1]
    vd = w_branch_gla.shape[0]
    assert DT_COPIES * n_heads + GLA_GATE_RANK <= SMALL_W
    sizes = (d_inner, d_inner + bcw, n_heads, kd, kd, vd, vd, GLA_GATE_RANK, 2 * d)
    offs = np.concatenate([[0], np.cumsum(sizes)])
    seg = lambda i: w_in[:, offs[i]:offs[i + 1]]
    z_w, xbc_w, dt_w, q_w, k_w, v_w, gb_w, gk_w, mg_w = [seg(i) for i in range(9)]
    perm = np.concatenate([np.arange(0, n_heads, 2), np.arange(1, n_heads, 2)])
    dt_w = dt_w[:, perm]
    gk0 = DT_COPIES * n_heads
    pad = SMALL_W - gk0 - GLA_GATE_RANK
    w_small = jnp.concatenate([dt_w] * DT_COPIES + [gk_w, jnp.zeros((d, pad), w_in.dtype)], axis=1).astype(BF16)
    pieces = [("xs", xbc_w[:, :d_inner]), ("z", z_w), ("merge", mg_w), ("bc", xbc_w[:, d_inner:]),
              ("v", v_w), ("g_b", gb_w), ("q", q_w), ("k", k_w)]
    col, o = {}, 0
    for name, p in pieces:
        assert o % p.shape[1] == 0, name
        col[name] = o
        o += p.shape[1]
    w_main = jnp.concatenate([p for _, p in pieces], axis=1).astype(BF16)

    def small_row(v, fill):
        return jnp.concatenate([v[perm]] * DT_COPIES + [jnp.full((SMALL_W - gk0,), fill, v.dtype)])[None, :]

    expand = np.zeros((DT_COPIES * n_heads, d_inner), np.float32)
    for r in range(DT_COPIES):
        for c_ in range(n_heads):
            expand[r * n_heads + c_, perm[c_] * SSD_HEAD_DIM:(perm[c_] + 1) * SSD_HEAD_DIM] = 1.0
    tril = np.tril(np.ones((CHUNK, CHUNK), np.float32))
    shift = np.concatenate([np.eye(CHUNK, k=-k, dtype=np.float32) for k in range(1, SSD_CONV_W)], axis=0)
    gk_w_pad = jnp.zeros((SMALL_W, kd), F32).at[gk0:gk0 + GLA_GATE_RANK].set(gla_gk_w)
    gk_hi = gk_w_pad.astype(BF16)
    gk_lo = (gk_w_pad - gk_hi.astype(F32)).astype(BF16)
    prm = dict(
        n_heads=n_heads, gla_key_dim=kd, gla_val_dim=vd,
        norm_pre_gain=norm_pre_gain[None, :], w_main=w_main, w_small=w_small,
        conv_w_x=conv_w[:, :d_inner], conv_w_bc=conv_w[:, d_inner:],
        conv_b_x=conv_b[None, :d_inner], conv_b_bc=conv_b[None, d_inner:],
        dt_bias3=small_row(dt_bias, 0.0), a_log3=small_row(a_log, 0.0),
        d_skip_e=jnp.repeat(d_skip, SSD_HEAD_DIM)[None, :], ssd_norm_gain=ssd_norm_gain[None, :],
        shift=jnp.asarray(shift, BF16), expand3=jnp.asarray(expand, BF16), tril3=jnp.asarray(np.concatenate([tril] * 3, axis=1), BF16),
        gk_w3=jnp.concatenate([gk_hi, gk_lo, gk_hi], axis=0), gk_b=gla_gk_b[None, :],
        gla_norm_gain=gla_norm_gain[None, :],
        w_branch_ssd=w_branch_ssd.astype(BF16), w_branch_gla=w_branch_gla.astype(BF16),
        w_out=w_out.astype(BF16), norm_post_gain=norm_post_gain[None, :],
    )
    return prm, col


def _row_tile(m, want):
    tm = min(m, want)
    assert m % tm == 0
    return tm


def _layer(x, conv_state, ssd_state, gla_state, prm, col):
    b, t, d = x.shape
    assert t % CHUNK == 0 and t >= SSD_CONV_W - 1
    m = b * t
    x2 = x.reshape(m, d)
    proj, small = _inproj(x2, prm["norm_pre_gain"], prm["w_main"], prm["w_small"], _row_tile(m, 1024), 1024)
    proj3 = proj.reshape(b, t, -1)
    small3 = small.reshape(b, t, -1)
    d_inner = prm["n_heads"] * SSD_HEAD_DIM
    bcw = 2 * SSD_GROUPS * SSD_D_STATE
    nc = SSD_CONV_W - 1
    new_conv = jnp.concatenate(
        [proj3[:, t - nc:, col["xs"]:col["xs"] + d_inner], proj3[:, t - nc:, col["bc"]:col["bc"] + bcw]],
        axis=-1).astype(x.dtype)
    y_ssd, new_ssd = _ssd(proj3, small3, col, prm, conv_state, ssd_state)
    y_gla, new_gla = _gla(proj3, small3, col, prm, gla_state)
    y = _outproj(y_ssd.reshape(m, -1), y_gla.reshape(m, -1), proj, col, x2, prm,
                 _row_tile(m, 256), 1024, _row_tile(m, 512))
    return y.reshape(b, t, d), new_conv, new_ssd, new_gla


def kernel(x_prompt, x_sample, state_conv_ssd, state_ssd, state_gla, norm_pre_gain, w_in, conv_w, conv_b, dt_bias,
           a_log, d_skip, ssd_norm_gain, gla_gk_w, gla_gk_b, gla_norm_gain, w_branch_ssd, w_branch_gla, w_out,
           norm_post_gain):
    depth = w_in.shape[0]
    yp, ys = x_prompt, x_sample
    outs = [[] for _ in range(6)]
    for layer in range(depth):
        prm, col = _prepare(norm_pre_gain[layer], w_in[layer], conv_w[layer], conv_b[layer], dt_bias[layer],
                            a_log[layer], d_skip[layer], ssd_norm_gain[layer], gla_gk_w[layer], gla_gk_b[layer],
                            gla_norm_gain[layer], w_branch_ssd[layer], w_branch_gla[layer], w_out[layer],
                            norm_post_gain[layer])
        yp, c_p, s_p, g_p = _layer(yp, None, None, None, prm, col)
        ys, c_s, s_s, g_s = _layer(ys, state_conv_ssd[layer], state_ssd[layer], state_gla[layer], prm, col)
        for lst, val in zip(outs, (c_p, s_p, g_p, c_s, s_s, g_s)):
            lst.append(val)
    return (yp, ys) + tuple(jnp.stack(o) for o in outs)
```

```python
import functools
import math

import numpy as np
import jax
import jax.numpy as jnp
from jax import lax
from jax.experimental import pallas as pl
from jax.experimental.pallas import tpu as pltpu

F32 = jnp.float32
BF16 = jnp.bfloat16

LANES = 128
SUBLANES = 8
VMEM_LIMIT_BYTES = 56 * 1024 * 1024

CHUNK = 64
SSD_HEAD_DIM = 64
SSD_GROUPS = 8
SSD_D_STATE = 128
SSD_CONV_W = 4
GLA_HEADS = 4
GLA_GATE_RANK = 16
GLA_GATE_NORMALIZER = 16.0
NORM_EPS = 1e-6
SSD_NORM_EPS = 1e-5
CHUNKS_PER_STEP = 4
SUB = 16
SMALL_W = 256
DT_COPIES = 2
LOG2E = math.log2(math.e)


def _dot(a, b, dims=(((1,), (0,)), ((), ()))):
    return lax.dot_general(a, b, dims, preferred_element_type=F32)


_NT = (((1,), (1,)), ((), ()))
_TN = (((0,), (0,)), ((), ()))


def _sigmoid_half(h):
    return 0.5 * jnp.tanh(h) + 0.5


def _silu_half(h):
    return h * jnp.tanh(h) + h


def _softplus(x):
    return jnp.maximum(x, 0.0) + jnp.log(1.0 + jnp.exp(-jnp.abs(x)))


def _residuals(x):
    r1 = x - x.astype(BF16).astype(F32)
    r2 = r1 - r1.astype(BF16).astype(F32)
    return x, r1, r2


def _cumsum_rows(tril3_ref, x):
    parts = [p.astype(BF16) for p in _residuals(x)]
    return _dot(tril3_ref[...], jnp.concatenate(parts, axis=0))


def _inproj_kernel(x_ref, g_ref, w_ref, ws_ref, o_ref, os_ref, h_ref):
    @pl.when(pl.program_id(1) == 0)
    def _():
        x = x_ref[...]
        ms = jnp.mean(x * x, axis=-1, keepdims=True)
        h_ref[...] = (x * lax.rsqrt(ms + NORM_EPS) * g_ref[...]).astype(BF16)
        os_ref[...] = _dot(h_ref[...], ws_ref[...])

    o_ref[...] = _dot(h_ref[...], w_ref[...]).astype(o_ref.dtype)


def _inproj(x2, gain, w_main, w_small, tm, tn):
    m, d = x2.shape
    n = w_main.shape[1]
    ns = w_small.shape[1]
    return pl.pallas_call(
        _inproj_kernel,
        grid=(m // tm, n // tn),
        in_specs=[
            pl.BlockSpec((tm, d), lambda i, j: (i, 0)),
            pl.BlockSpec((1, d), lambda i, j: (0, 0)),
            pl.BlockSpec((d, tn), lambda i, j: (0, j)),
            pl.BlockSpec((d, ns), lambda i, j: (0, 0)),
        ],
        out_specs=[
            pl.BlockSpec((tm, tn), lambda i, j: (i, j)),
            pl.BlockSpec((tm, ns), lambda i, j: (i, 0)),
        ],
        out_shape=[jax.ShapeDtypeStruct((m, n), BF16), jax.ShapeDtypeStruct((m, ns), F32)],
        scratch_shapes=[pltpu.VMEM((tm, d), BF16)],
        compiler_params=pltpu.CompilerParams(
            dimension_semantics=("arbitrary", "arbitrary"), vmem_limit_bytes=VMEM_LIMIT_BYTES),
    )(x2, gain, w_main, w_small)


def _ssd_kernel(has_state, n_heads, n_sub, *refs):
    if has_state:
        (xs_ref, z_ref, bc_ref, sm_ref, cwx_ref, cbx_ref, cwb_ref, cbb_ref, dtb_ref, alog_ref, dsk_ref,
         gain_ref, exp_ref, tril_ref, shift_ref, csx_ref, csb_ref, st_ref,
         y_ref, so_ref, tailx, tailb, srow, dtrow, xs_s, xs16, bc16, bcast, *stT) = refs
    else:
        (xs_ref, z_ref, bc_ref, sm_ref, cwx_ref, cbx_ref, cwb_ref, cbb_ref, dtb_ref, alog_ref, dsk_ref,
         gain_ref, exp_ref, tril_ref, shift_ref,
         y_ref, so_ref, tailx, tailb, srow, dtrow, xs_s, xs16, bc16, bcast, *stT) = refs
    c = pl.program_id(1)
    n_chunks = pl.num_programs(1)
    L = CHUNK
    P = SSD_HEAD_DIM
    N = SSD_D_STATE
    G = SSD_GROUPS
    gw = (n_heads // G) * P
    d_inner = n_heads * P
    nc = SSD_CONV_W - 1

    @pl.when(c == 0)
    def _():
        tailx[...] = jnp.zeros_like(tailx)
        tailb[...] = jnp.zeros_like(tailb)
        if has_state:
            tailx[SUBLANES - nc:SUBLANES, :] = csx_ref[0]
            tailb[SUBLANES - nc:SUBLANES, :] = csb_ref[0]
            for g in range(G):
                stT[g][...] = st_ref[0, g * gw:(g + 1) * gw, :].T
        else:
            for g in range(G):
                stT[g][...] = jnp.zeros_like(stT[g])

    def chunk(si, carry):
        c0 = pl.multiple_of(si * L, L)
        cw = 4 * LANES
        rb = 2 * SUBLANES
        row8 = lax.broadcasted_iota(jnp.int32, (SUBLANES, cw), 0)

        def conv(x_ref, tail, w_ref, b_ref, width, emit):
            for cb in range(width // cw):
                cs = slice(cb * cw, (cb + 1) * cw)
                sh = _dot(shift_ref[...], x_ref[0, pl.ds(c0, L), cs])
                prev = tail[:, cs]
                fix = jnp.zeros((SUBLANES, cw), F32)
                for k in range(1, SSD_CONV_W):
                    fix = fix + jnp.where(row8 < k, pltpu.roll(prev, k, 0), 0.0) * w_ref[nc - k, 0:SUBLANES, cs]
                for blk in range(L // rb):
                    acc = b_ref[:, cs] + x_ref[0, pl.ds(c0 + blk * rb, rb), cs].astype(F32) * w_ref[nc, :, cs]
                    for k in range(1, SSD_CONV_W):
                        s0 = (blk * nc + k - 1) * rb
                        acc = acc + sh[s0:s0 + rb, :] * w_ref[nc - k, :, cs]
                    if blk == 0:
                        acc = jnp.concatenate([acc[0:SUBLANES, :] + fix, acc[SUBLANES:, :]], axis=0)
                    emit(blk * rb, rb, cs, _silu_half(acc))

        def emit_x(r0, nr, cs, act):
            xs_s[r0:r0 + nr, cs] = act
            xs16[r0:r0 + nr, cs] = act.astype(BF16)

        def emit_bc(r0, nr, cs, act):
            bc16[r0:r0 + nr, cs] = act.astype(BF16)

        conv(xs_ref, tailx, cwx_ref, cbx_ref, d_inner, emit_x)
        conv(bc_ref, tailb, cwb_ref, cbb_ref, 2 * G * N, emit_bc)

        dt3 = _softplus(sm_ref[0, pl.ds(c0, L), :] + dtb_ref[...])
        s3 = _cumsum_rows(tril_ref, dt3 * (-jnp.exp(alog_ref[...])))
        w3 = dt3 * jnp.exp(s3[L - 1:L, :] - s3)
        lane_s = lax.broadcasted_iota(jnp.int32, (2 * L, SMALL_W), 1)
        x0, x1, _ = _residuals(jnp.concatenate([s3, w3], axis=0))
        split = jnp.where(lane_s < n_heads, x0, x1)[:, 0:DT_COPIES * n_heads].astype(BF16)
        bcast[...] = _dot(split, exp_ref[...])
        hh = n_heads // 2
        s_t = s3[:, 0:2 * L].T
        dt_t = dt3[:, 0:2 * L].T
        srow[...] = jnp.concatenate([s_t[0:hh], s_t[hh:2 * hh]], axis=1)
        dtrow[...] = jnp.concatenate([dt_t[0:hh], dt_t[hh:2 * hh]], axis=1)

        row = lax.broadcasted_iota(jnp.int32, (L, 2 * P), 0)
        lane = lax.broadcasted_iota(jnp.int32, (L, 2 * P), 1)
        causal2 = (lane % P) <= row
        left = lane < P
        ppg = gw // (2 * P)

        for g in range(G):
            gs = slice(g * gw, (g + 1) * gw)
            bm16 = bc16[:, g * N:(g + 1) * N]
            cm16 = bc16[:, (G + g) * N:(G + g + 1) * N]
            cb2 = _dot(cm16, jnp.concatenate([bm16, bm16], axis=0), _NT)
            st_g = stT[g][...]
            y_state = _dot(cm16, st_g.astype(BF16))
            ys = []
            for pp in range(ppg):
                p = g * ppg + pp
                sl = slice(p * 2 * P, (p + 1) * 2 * P)
                se = bcast[0:L, sl]
                dec = jnp.exp(jnp.where(causal2, se - srow[p:p + 1, :], -jnp.inf))
                m = (cb2 * dec * dtrow[p:p + 1, :]).astype(BF16)
                xp16 = xs16[:, sl]
                zero = jnp.zeros_like(xp16)
                rhs = jnp.concatenate([jnp.where(left, xp16, zero), jnp.where(left, zero, xp16)], axis=0)
                y = _dot(m, rhs)
                y = y + y_state[:, pp * 2 * P:(pp + 1) * 2 * P] * jnp.exp(se) + dsk_ref[:, sl] * xs_s[:, sl]
                ys.append(y)
            s_last = bcast[L - 1:L, gs]
            xdt_end = xs_s[:, gs] * bcast[L:2 * L, gs]
            upd = _dot(bm16, xdt_end.astype(BF16), _TN)
            stT[g][...] = st_g * jnp.exp(s_last) + upd

            yg = jnp.concatenate(ys, axis=1) * _silu_half(z_ref[0, pl.ds(c0, L), gs].astype(F32))
            ms = jnp.mean(yg * yg, axis=-1, keepdims=True)
            y_ref[0, pl.ds(c0, L), gs] = (yg * lax.rsqrt(ms + SSD_NORM_EPS) * gain_ref[:, gs]).astype(y_ref.dtype)

        tailx[...] = xs_ref[0, pl.ds(c0 + L - 2 * SUBLANES, 2 * SUBLANES), :].astype(F32)[SUBLANES:, :]
        tailb[...] = bc_ref[0, pl.ds(c0 + L - 2 * SUBLANES, 2 * SUBLANES), :].astype(F32)[SUBLANES:, :]

        return carry

    lax.fori_loop(0, n_sub, chunk, 0)

    @pl.when(c == n_chunks - 1)
    def _():
        for g in range(G):
            so_ref[0, g * gw:(g + 1) * gw, :] = stT[g][...].T


def _ssd(proj3, small3, col, prm, conv_state, ssd_state):
    b, t, _ = proj3.shape
    n_heads = prm["n_heads"]
    d_inner = n_heads * SSD_HEAD_DIM
    bcw = 2 * SSD_GROUPS * SSD_D_STATE
    has_state = ssd_state is not None
    L = CHUNK
    tb = _time_block(t)
    const = lambda *shape: pl.BlockSpec(shape, lambda i, j: (0,) * len(shape))
    in_specs = [
        pl.BlockSpec((1, tb, d_inner), lambda i, j: (i, j, col["xs"] // d_inner)),
        pl.BlockSpec((1, tb, d_inner), lambda i, j: (i, j, col["z"] // d_inner)),
        pl.BlockSpec((1, tb, bcw), lambda i, j: (i, j, col["bc"] // bcw)),
        pl.BlockSpec((1, tb, SMALL_W), lambda i, j: (i, j, 0)),
        const(SSD_CONV_W, 2 * SUBLANES, d_inner), const(2 * SUBLANES, d_inner),
        const(SSD_CONV_W, 2 * SUBLANES, bcw), const(2 * SUBLANES, bcw),
        const(1, SMALL_W), const(1, SMALL_W), const(1, d_inner), const(1, d_inner),
        const(DT_COPIES * n_heads, d_inner), const(L, 3 * L), const((SSD_CONV_W - 1) * L, L),
    ]
    args = [proj3, proj3, proj3, small3, prm["conv_w_x"], prm["conv_b_x"], prm["conv_w_bc"], prm["conv_b_bc"],
            prm["dt_bias3"], prm["a_log3"], prm["d_skip_e"], prm["ssd_norm_gain"], prm["expand3"], prm["tril3"],
            prm["shift"]]
    if has_state:
        nc = SSD_CONV_W - 1
        in_specs += [
            pl.BlockSpec((1, nc, d_inner), lambda i, j: (i, 0, 0)),
            pl.BlockSpec((1, nc, bcw), lambda i, j: (i, 0, 0)),
            pl.BlockSpec((1, d_inner, SSD_D_STATE), lambda i, j: (i, 0, 0)),
        ]
        args += [conv_state[:, :, :d_inner], conv_state[:, :, d_inner:],
                 ssd_state.reshape(b, d_inner, SSD_D_STATE)]
    y, st = pl.pallas_call(
        functools.partial(_ssd_kernel, has_state, n_heads, tb // L),
        grid=(b, t // tb),
        in_specs=in_specs,
        out_specs=[
            pl.BlockSpec((1, tb, d_inner), lambda i, j: (i, j, 0)),
            pl.BlockSpec((1, d_inner, SSD_D_STATE), lambda i, j: (i, 0, 0)),
        ],
        out_shape=[
            jax.ShapeDtypeStruct((b, t, d_inner), BF16),
            jax.ShapeDtypeStruct((b, d_inner, SSD_D_STATE), F32),
        ],
        scratch_shapes=[
            pltpu.VMEM((SUBLANES, d_inner), F32),
            pltpu.VMEM((SUBLANES, bcw), F32),
            pltpu.VMEM((n_heads // 2, 2 * L), F32),
            pltpu.VMEM((n_heads // 2, 2 * L), F32),
            pltpu.VMEM((L, d_inner), F32),
            pltpu.VMEM((L, d_inner), BF16),
            pltpu.VMEM((L, bcw), BF16),
            pltpu.VMEM((2 * L, d_inner), F32),
        ] + [pltpu.VMEM((SSD_D_STATE, d_inner // SSD_GROUPS), F32) for _ in range(SSD_GROUPS)],
        compiler_params=pltpu.CompilerParams(
            dimension_semantics=("arbitrary", "arbitrary"), vmem_limit_bytes=VMEM_LIMIT_BYTES),
    )(*args)
    return y, st.reshape(b, n_heads, SSD_HEAD_DIM, SSD_D_STATE)


def _gla_kernel(has_state, n_sub, *refs):
    if has_state:
        (q_ref, k_ref, v_ref, gb_ref, sm_ref, wgk_ref, bgk_ref, gain_ref, tril_ref, st_ref,
         o_ref, so_ref, bcs, ks, qs, *stT) = refs
    else:
        (q_ref, k_ref, v_ref, gb_ref, sm_ref, wgk_ref, bgk_ref, gain_ref, tril_ref,
         o_ref, so_ref, bcs, ks, qs, *stT) = refs
    c = pl.program_id(1)
    n_chunks = pl.num_programs(1)
    L = CHUNK
    H = GLA_HEADS
    kd = q_ref.shape[2]
    vd = v_ref.shape[2]
    hk = kd // H
    hv = vd // H
    nb = L // SUB

    @pl.when(c == 0)
    def _():
        if has_state:
            for h in range(H):
                stT[h][...] = st_ref[0, h].T
        else:
            for h in range(H):
                stT[h][...] = jnp.zeros_like(stT[h])

    def chunk(si, carry):
        c0 = pl.multiple_of(si * L, L)
        sm = sm_ref[0, pl.ds(c0, L), :]
        sm_hi = sm.astype(BF16)
        sm_lo = (sm - sm_hi.astype(F32)).astype(BF16)
        x = _dot(jnp.concatenate([sm_hi, sm_hi, sm_lo], axis=1), wgk_ref[...]) + bgk_ref[...]
        glog2 = (jnp.minimum(x, 0.0) - jnp.log(1.0 + jnp.exp(-jnp.abs(x)))) * (LOG2E / GLA_GATE_NORMALIZER)
        bc = _cumsum_rows(tril_ref, glog2)
        bcs[...] = bc
        bc_last = bc[L - 1:L, :]
        q = q_ref[0, pl.ds(c0, L), :].astype(F32) * (hk ** -0.5)
        k = k_ref[0, pl.ds(c0, L), :].astype(F32)
        ks[...] = k
        qs[...] = q
        qe = (q * jnp.exp2(bc)).astype(BF16)
        k_end = (k * jnp.exp2(bc_last - bc)).astype(BF16)

        nt = SUB // SUBLANES
        lane = lax.broadcasted_iota(jnp.int32, (SUBLANES, L), 1)
        row = lax.broadcasted_iota(jnp.int32, (SUBLANES, L), 0)
        a_rows = [[None] * nb for _ in range(H)]
        for blk in range(nb):
            r0 = blk * SUB
            off = [None] * H
            if blk > 0:
                ref_row = bcs[r0 - 1:r0, :]
                qd = (q[r0:r0 + SUB, :] * jnp.exp2(bc[r0:r0 + SUB, :] - ref_row)).astype(BF16)
                kt = k[0:r0, :] * jnp.exp2(ref_row - bc[0:r0, :])
                kt = jnp.concatenate([kt, jnp.zeros((L - r0, kd), F32)], axis=0).astype(BF16)
                for h in range(H):
                    off[h] = _dot(qd[:, h * hk:(h + 1) * hk], kt[:, h * hk:(h + 1) * hk], _NT)
            diag = [[jnp.zeros((SUBLANES, L), F32) for _ in range(nt)] for _ in range(H)]
            for jj in range(SUB):
                j = r0 + jj
                for ti in range(jj // SUBLANES, nt):
                    t0 = r0 + ti * SUBLANES
                    keep = (lane == j) & (row + t0 >= j)
                    for h in range(H):
                        tot = None
                        for lt in range(hk // LANES):
                            ls = slice(h * hk + lt * LANES, h * hk + (lt + 1) * LANES)
                            e = jnp.exp2(bcs[t0:t0 + SUBLANES, ls] - bcs[j:j + 1, ls])
                            pr = qs[t0:t0 + SUBLANES, ls] * e * ks[j:j + 1, ls]
                            tot = pr if tot is None else tot + pr
                        col = jnp.sum(tot, axis=-1, keepdims=True)
                        diag[h][ti] = jnp.where(keep, col, diag[h][ti])
            for h in range(H):
                d_h = jnp.concatenate(diag[h], axis=0)
                a_rows[h][blk] = d_h if off[h] is None else off[h] + d_h

        eb_last = jnp.exp2(bc_last)
        for h in range(H):
            a_h = jnp.concatenate(a_rows[h], axis=0).astype(BF16)
            v_h = v_ref[0, pl.ds(c0, L), h * hv:(h + 1) * hv]
            st_h = stT[h][...]
            o = _dot(a_h, v_h) + _dot(qe[:, h * hk:(h + 1) * hk], st_h.astype(BF16), _NT)
            stT[h][...] = st_h * eb_last[:, h * hk:(h + 1) * hk] + _dot(v_h, k_end[:, h * hk:(h + 1) * hk], _TN)
            ms = jnp.mean(o * o, axis=-1, keepdims=True)
            on = o * lax.rsqrt(ms + NORM_EPS) * gain_ref[...]
            gate = _silu_half(gb_ref[0, pl.ds(c0, L), h * hv:(h + 1) * hv].astype(F32))
            o_ref[0, pl.ds(c0, L), h * hv:(h + 1) * hv] = (on * gate).astype(o_ref.dtype)

        return carry

    lax.fori_loop(0, n_sub, chunk, 0)

    @pl.when(c == n_chunks - 1)
    def _():
        for h in range(H):
            so_ref[0, h] = stT[h][...].T


def _gla(proj3, small3, col, prm, gla_state):
    b, t, _ = proj3.shape
    kd = prm["gla_key_dim"]
    vd = prm["gla_val_dim"]
    H = GLA_HEADS
    hk, hv = kd // H, vd // H
    has_state = gla_state is not None
    L = CHUNK
    tb = _time_block(t)
    const = lambda *shape: pl.BlockSpec(shape, lambda i, j: (0,) * len(shape))
    in_specs = [
        pl.BlockSpec((1, tb, kd), lambda i, j: (i, j, col["q"] // kd)),
        pl.BlockSpec((1, tb, kd), lambda i, j: (i, j, col["k"] // kd)),
        pl.BlockSpec((1, tb, vd), lambda i, j: (i, j, col["v"] // vd)),
        pl.BlockSpec((1, tb, vd), lambda i, j: (i, j, col["g_b"] // vd)),
        pl.BlockSpec((1, tb, SMALL_W), lambda i, j: (i, j, 0)),
        const(3 * SMALL_W, kd), const(1, kd), const(1, hv), const(L, 3 * L),
    ]
    args = [proj3, proj3, proj3, proj3, small3, prm["gk_w3"], prm["gk_b"], prm["gla_norm_gain"], prm["tril3"]]
    if has_state:
        in_specs.append(pl.BlockSpec((1, H, hk, hv), lambda i, j: (i, 0, 0, 0)))
        args.append(gla_state)
    return pl.pallas_call(
        functools.partial(_gla_kernel, has_state, tb // L),
        grid=(b, t // tb),
        in_specs=in_specs,
        out_specs=[
            pl.BlockSpec((1, tb, vd), lambda i, j: (i, j, 0)),
            pl.BlockSpec((1, H, hk, hv), lambda i, j: (i, 0, 0, 0)),
        ],
        out_shape=[
            jax.ShapeDtypeStruct((b, t, vd), BF16),
            jax.ShapeDtypeStruct((b, H, hk, hv), F32),
        ],
        scratch_shapes=[
            pltpu.VMEM((L, kd), F32),
            pltpu.VMEM((L, kd), F32),
            pltpu.VMEM((L, kd), F32),
        ] + [pltpu.VMEM((hv, hk), F32) for _ in range(H)],
        compiler_params=pltpu.CompilerParams(
            dimension_semantics=("arbitrary", "arbitrary"), vmem_limit_bytes=VMEM_LIMIT_BYTES),
    )(*args)


def _merge_kernel(tn, ys_ref, yg_ref, g_ref, wbs_ref, wbg_ref, o_ref):
    d = o_ref.shape[1]
    for n in range(d // tn):
        ns = slice(n * tn, (n + 1) * tn)
        a = _dot(ys_ref[...], wbs_ref[:, ns])
        b = _dot(yg_ref[...], wbg_ref[:, ns])
        gate_s = _sigmoid_half(g_ref[:, ns].astype(F32))
        gate_g = _sigmoid_half(g_ref[:, d + n * tn:d + (n + 1) * tn].astype(F32))
        o_ref[:, ns] = (gate_s * a + gate_g * b).astype(o_ref.dtype)


def _post_kernel(mx_ref, x_ref, wo_ref, g_ref, o_ref):
    cc = _dot(mx_ref[...], wo_ref[...])
    ms = jnp.mean(cc * cc, axis=-1, keepdims=True)
    o_ref[...] = x_ref[...] + cc * lax.rsqrt(ms + NORM_EPS) * g_ref[...]


def _outproj(y_ssd, y_gla, proj, col, x2, prm, tm_merge, tn, tm_post):
    m, d = x2.shape
    d_inner = y_ssd.shape[1]
    vd = y_gla.shape[1]
    g0 = col["merge"] // (2 * d)
    resident = lambda r, c_: pl.BlockSpec((r, c_), lambda i: (0, 0), pipeline_mode=pl.Buffered(1))
    mixed = pl.pallas_call(
        functools.partial(_merge_kernel, tn),
        grid=(m // tm_merge,),
        in_specs=[
            pl.BlockSpec((tm_merge, d_inner), lambda i: (i, 0)),
            pl.BlockSpec((tm_merge, vd), lambda i: (i, 0)),
            pl.BlockSpec((tm_merge, 2 * d), lambda i: (i, g0)),
            resident(d_inner, d), resident(vd, d),
        ],
        out_specs=pl.BlockSpec((tm_merge, d), lambda i: (i, 0)),
        out_shape=jax.ShapeDtypeStruct((m, d), BF16),
        compiler_params=pltpu.CompilerParams(
            dimension_semantics=("arbitrary",), vmem_limit_bytes=VMEM_LIMIT_BYTES),
    )(y_ssd, y_gla, proj, prm["w_branch_ssd"], prm["w_branch_gla"])
    return pl.pallas_call(
        _post_kernel,
        grid=(m // tm_post,),
        in_specs=[
            pl.BlockSpec((tm_post, d), lambda i: (i, 0)),
            pl.BlockSpec((tm_post, d), lambda i: (i, 0)),
            pl.BlockSpec((d, d), lambda i: (0, 0)),
            pl.BlockSpec((1, d), lambda i: (0, 0)),
        ],
        out_specs=pl.BlockSpec((tm_post, d), lambda i: (i, 0)),
        out_shape=jax.ShapeDtypeStruct((m, d), F32),
        compiler_params=pltpu.CompilerParams(
            dimension_semantics=("arbitrary",), vmem_limit_bytes=VMEM_LIMIT_BYTES),
    )(mixed, x2, prm["w_out"], prm["norm_post_gain"])


def _prepare(norm_pre_gain, w_in, conv_w, conv_b, dt_bias, a_log, d_skip, ssd_norm_gain, gla_gk_w, gla_gk_b,
             gla_norm_gain, w_branch_ssd, w_branch_gla, w_out, norm_post_gain):
    d = w_in.shape[0]
    n_heads = dt_bias.shape[0]
    d_inner = n_heads * SSD_HEAD_DIM
    bcw = 2 * SSD_GROUPS * SSD_D_STATE
    kd = gla_gk_w.shape[1]
    vd = w_branch_gla.shape[0]
    assert DT_COPIES * n_heads + GLA_GATE_RANK <= SMALL_W
    sizes = (d_inner, d_inner + bcw, n_heads, kd, kd, vd, vd, GLA_GATE_RANK, 2 * d)
    offs = np.concatenate([[0], np.cumsum(sizes)])
    seg = lambda i: w_in[:, offs[i]:offs[i + 1]]
    z_w, xbc_w, dt_w, q_w, k_w, v_w, gb_w, gk_w, mg_w = [seg(i) for i in range(9)]
    perm = np.concatenate([np.arange(0, n_heads, 2), np.arange(1, n_heads, 2)])
    dt_w = dt_w[:, perm]
    gk0 = DT_COPIES * n_heads
    pad = SMALL_W - gk0 - GLA_GATE_RANK
    w_small = jnp.concatenate([dt_w] * DT_COPIES + [gk_w, jnp.zeros((d, pad), w_in.dtype)], axis=1).astype(BF16)
    half = 0.5
    pieces = [("xs", xbc_w[:, :d_inner]), ("z", half * z_w), ("merge", half * mg_w), ("bc", xbc_w[:, d_inner:]),
              ("v", v_w), ("g_b", half * gb_w), ("q", q_w), ("k", k_w)]
    col, o = {}, 0
    for name, p in pieces:
        assert o % p.shape[1] == 0, name
        col[name] = o
        o += p.shape[1]
    w_main = jnp.concatenate([p for _, p in pieces], axis=1).astype(BF16)

    def small_row(v, fill):
        return jnp.concatenate([v[perm]] * DT_COPIES + [jnp.full((SMALL_W - gk0,), fill, v.dtype)])[None, :]

    expand = np.zeros((DT_COPIES * n_heads, d_inner), np.float32)
    for r in range(DT_COPIES):
        for c_ in range(n_heads):
            expand[r * n_heads + c_, perm[c_] * SSD_HEAD_DIM:(perm[c_] + 1) * SSD_HEAD_DIM] = 1.0
    tril = np.tril(np.ones((CHUNK, CHUNK), np.float32))
    rb = 2 * SUBLANES
    shift = np.concatenate([np.eye(CHUNK, k=-k, dtype=np.float32)[b * rb:(b + 1) * rb]
                            for b in range(CHUNK // rb) for k in range(1, SSD_CONV_W)], axis=0)
    gk_w_pad = jnp.zeros((SMALL_W, kd), F32).at[gk0:gk0 + GLA_GATE_RANK].set(gla_gk_w)
    gk_hi = gk_w_pad.astype(BF16)
    gk_lo = (gk_w_pad - gk_hi.astype(F32)).astype(BF16)
    rows = lambda a: jnp.broadcast_to(a, a.shape[:-2] + (2 * SUBLANES, a.shape[-1]))
    prm = dict(
        n_heads=n_heads, gla_key_dim=kd, gla_val_dim=vd,
        norm_pre_gain=norm_pre_gain[None, :], w_main=w_main, w_small=w_small,
        conv_w_x=rows(half * conv_w[:, None, :d_inner]), conv_w_bc=rows(half * conv_w[:, None, d_inner:]),
        conv_b_x=rows(half * conv_b[None, :d_inner]), conv_b_bc=rows(half * conv_b[None, d_inner:]),
        dt_bias3=small_row(dt_bias, 0.0), a_log3=small_row(a_log, 0.0),
        d_skip_e=jnp.repeat(d_skip, SSD_HEAD_DIM)[None, :], ssd_norm_gain=ssd_norm_gain[None, :],
        shift=jnp.asarray(shift, BF16), expand3=jnp.asarray(expand, BF16), tril3=jnp.asarray(np.concatenate([tril] * 3, axis=1), BF16),
        gk_w3=jnp.concatenate([gk_hi, gk_lo, gk_hi], axis=0), gk_b=gla_gk_b[None, :],
        gla_norm_gain=gla_norm_gain[None, :],
        w_branch_ssd=w_branch_ssd.astype(BF16), w_branch_gla=w_branch_gla.astype(BF16),
        w_out=w_out.astype(BF16), norm_post_gain=norm_post_gain[None, :],
    )
    return prm, col


def _time_block(t):
    tb = min(t, CHUNKS_PER_STEP * CHUNK)
    assert t % tb == 0
    return tb


def _row_tile(m, want):
    tm = min(m, want)
    assert m % tm == 0
    return tm


def _layer(x, conv_state, ssd_state, gla_state, prm, col):
    b, t, d = x.shape
    assert t % CHUNK == 0 and t >= SSD_CONV_W - 1
    m = b * t
    x2 = x.reshape(m, d)
    proj, small = _inproj(x2, prm["norm_pre_gain"], prm["w_main"], prm["w_small"], _row_tile(m, 1024), 2048)
    proj3 = proj.reshape(b, t, -1)
    small3 = small.reshape(b, t, -1)
    d_inner = prm["n_heads"] * SSD_HEAD_DIM
    bcw = 2 * SSD_GROUPS * SSD_D_STATE
    nc = SSD_CONV_W - 1
    new_conv = jnp.concatenate(
        [proj3[:, t - nc:, col["xs"]:col["xs"] + d_inner], proj3[:, t - nc:, col["bc"]:col["bc"] + bcw]],
        axis=-1).astype(x.dtype)
    y_ssd, new_ssd = _ssd(proj3, small3, col, prm, conv_state, ssd_state)
    y_gla, new_gla = _gla(proj3, small3, col, prm, gla_state)
    y = _outproj(y_ssd.reshape(m, -1), y_gla.reshape(m, -1), proj, col, x2, prm,
                 _row_tile(m, 256), 1024, _row_tile(m, 512))
    return y.reshape(b, t, d), new_conv, new_ssd, new_gla


def kernel(x_prompt, x_sample, state_conv_ssd, state_ssd, state_gla, norm_pre_gain, w_in, conv_w, conv_b, dt_bias,
           a_log, d_skip, ssd_norm_gain, gla_gk_w, gla_gk_b, gla_norm_gain, w_branch_ssd, w_branch_gla, w_out,
           norm_post_gain):
    depth = w_in.shape[0]
    yp, ys = x_prompt, x_sample
    outs = [[] for _ in range(6)]
    for layer in range(depth):
        prm, col = _prepare(norm_pre_gain[layer], w_in[layer], conv_w[layer], conv_b[layer], dt_bias[layer],
                            a_log[layer], d_skip[layer], ssd_norm_gain[layer], gla_gk_w[layer], gla_gk_b[layer],
                            gla_norm_gain[layer], w_branch_ssd[layer], w_branch_gla[layer], w_out[layer],
                            norm_post_gain[layer])
        yp, c_p, s_p, g_p = _layer(yp, None, None, None, prm, col)
        ys, c_s, s_s, g_s = _layer(ys, state_conv_ssd[layer], state_ssd[layer], state_gla[layer], prm, col)
        for lst, val in zip(outs, (c_p, s_p, g_p, c_s, s_s, g_s)):
            lst.append(val)
    return (yp, ys) + tuple(jnp.stack(o) for o in outs)
```

```python
import functools
import math

import numpy as np
import jax
import jax.numpy as jnp
from jax import lax
from jax.experimental import pallas as pl
from jax.experimental.pallas import tpu as pltpu

F32 = jnp.float32
BF16 = jnp.bfloat16

LANES = 128
SUBLANES = 8
VMEM_LIMIT_BYTES = 56 * 1024 * 1024

CHUNK = 64
SSD_HEAD_DIM = 64
SSD_GROUPS = 8
SSD_D_STATE = 128
SSD_CONV_W = 4
GLA_HEADS = 4
GLA_GATE_RANK = 16
GLA_GATE_NORMALIZER = 16.0
NORM_EPS = 1e-6
SSD_NORM_EPS = 1e-5
CHUNKS_PER_STEP = 4
SUB = 16
SMALL_W = 256
DT_COPIES = 2
LOG2E = math.log2(math.e)


def _dot(a, b, dims=(((1,), (0,)), ((), ()))):
    return lax.dot_general(a, b, dims, preferred_element_type=F32)


_NT = (((1,), (1,)), ((), ()))
_TN = (((0,), (0,)), ((), ()))


def _sigmoid_half(h):
    return 0.5 * jnp.tanh(h) + 0.5


def _silu_half(h):
    return h * jnp.tanh(h) + h


def _softplus(x):
    return jnp.maximum(x, 0.0) + jnp.log(1.0 + jnp.exp(-jnp.abs(x)))


def _residuals(x):
    r1 = x - x.astype(BF16).astype(F32)
    r2 = r1 - r1.astype(BF16).astype(F32)
    return x, r1, r2


def _cumsum_rows(tril3_ref, x):
    parts = [p.astype(BF16) for p in _residuals(x)]
    return _dot(tril3_ref[...], jnp.concatenate(parts, axis=0))


def _inproj_kernel(x_ref, g_ref, w_ref, ws_ref, o_ref, os_ref, h_ref):
    @pl.when(pl.program_id(1) == 0)
    def _():
        x = x_ref[...]
        ms = jnp.mean(x * x, axis=-1, keepdims=True)
        h_ref[...] = (x * lax.rsqrt(ms + NORM_EPS) * g_ref[...]).astype(BF16)
        os_ref[...] = _dot(h_ref[...], ws_ref[...])

    o_ref[...] = _dot(h_ref[...], w_ref[...]).astype(o_ref.dtype)


def _inproj(x2, gain, w_main, w_small, tm, tn):
    m, d = x2.shape
    n = w_main.shape[1]
    ns = w_small.shape[1]
    return pl.pallas_call(
        _inproj_kernel,
        grid=(m // tm, n // tn),
        in_specs=[
            pl.BlockSpec((tm, d), lambda i, j: (i, 0)),
            pl.BlockSpec((1, d), lambda i, j: (0, 0)),
            pl.BlockSpec((d, tn), lambda i, j: (0, j)),
            pl.BlockSpec((d, ns), lambda i, j: (0, 0)),
        ],
        out_specs=[
            pl.BlockSpec((tm, tn), lambda i, j: (i, j)),
            pl.BlockSpec((tm, ns), lambda i, j: (i, 0)),
        ],
        out_shape=[jax.ShapeDtypeStruct((m, n), BF16), jax.ShapeDtypeStruct((m, ns), F32)],
        scratch_shapes=[pltpu.VMEM((tm, d), BF16)],
        compiler_params=pltpu.CompilerParams(
            dimension_semantics=("arbitrary", "arbitrary"), vmem_limit_bytes=VMEM_LIMIT_BYTES),
    )(x2, gain, w_main, w_small)


def _ssd_kernel(has_state, n_heads, n_sub, *refs):
    if has_state:
        (xs_ref, z_ref, bc_ref, sm_ref, cwx_ref, cbx_ref, cwb_ref, cbb_ref, dtb_ref, alog_ref, dsk_ref,
         gain_ref, exp_ref, tril_ref, shift_ref, csx_ref, csb_ref, st_ref,
         y_ref, so_ref, tailx, tailb, srow, dtrow, xs_s, xs16, bc16, bcast, *stT) = refs
    else:
        (xs_ref, z_ref, bc_ref, sm_ref, cwx_ref, cbx_ref, cwb_ref, cbb_ref, dtb_ref, alog_ref, dsk_ref,
         gain_ref, exp_ref, tril_ref, shift_ref,
         y_ref, so_ref, tailx, tailb, srow, dtrow, xs_s, xs16, bc16, bcast, *stT) = refs
    c = pl.program_id(1)
    n_chunks = pl.num_programs(1)
    L = CHUNK
    P = SSD_HEAD_DIM
    N = SSD_D_STATE
    G = SSD_GROUPS
    gw = (n_heads // G) * P
    d_inner = n_heads * P
    nc = SSD_CONV_W - 1

    @pl.when(c == 0)
    def _():
        tailx[...] = jnp.zeros_like(tailx)
        tailb[...] = jnp.zeros_like(tailb)
        if has_state:
            tailx[SUBLANES - nc:SUBLANES, :] = csx_ref[0]
            tailb[SUBLANES - nc:SUBLANES, :] = csb_ref[0]
            for g in range(G):
                stT[g][...] = st_ref[0, g * gw:(g + 1) * gw, :].T
        else:
            for g in range(G):
                stT[g][...] = jnp.zeros_like(stT[g])

    def chunk(si, carry):
        c0 = pl.multiple_of(si * L, L)
        cw = 4 * LANES
        rb = 2 * SUBLANES
        row8 = lax.broadcasted_iota(jnp.int32, (SUBLANES, cw), 0)

        def conv(x_ref, tail, w_ref, b_ref, width, emit):
            for cb in range(width // cw):
                cs = slice(cb * cw, (cb + 1) * cw)
                sh = _dot(shift_ref[...], x_ref[0, pl.ds(c0, L), cs])
                prev = tail[:, cs]
                fix = jnp.zeros((SUBLANES, cw), F32)
                for k in range(1, SSD_CONV_W):
                    fix = fix + jnp.where(row8 < k, pltpu.roll(prev, k, 0), 0.0) * w_ref[nc - k, 0:SUBLANES, cs]
                for blk in range(L // rb):
                    acc = b_ref[:, cs] + x_ref[0, pl.ds(c0 + blk * rb, rb), cs].astype(F32) * w_ref[nc, :, cs]
                    for k in range(1, SSD_CONV_W):
                        s0 = (blk * nc + k - 1) * rb
                        acc = acc + sh[s0:s0 + rb, :] * w_ref[nc - k, :, cs]
                    if blk == 0:
                        acc = jnp.concatenate([acc[0:SUBLANES, :] + fix, acc[SUBLANES:, :]], axis=0)
                    emit(blk * rb, rb, cs, _silu_half(acc))

        def emit_x(r0, nr, cs, act):
            xs_s[r0:r0 + nr, cs] = act
            xs16[r0:r0 + nr, cs] = act.astype(BF16)

        def emit_bc(r0, nr, cs, act):
            bc16[r0:r0 + nr, cs] = act.astype(BF16)

        conv(xs_ref, tailx, cwx_ref, cbx_ref, d_inner, emit_x)
        conv(bc_ref, tailb, cwb_ref, cbb_ref, 2 * G * N, emit_bc)

        dt3 = _softplus(sm_ref[0, pl.ds(c0, L), :] + dtb_ref[...])
        s3 = _cumsum_rows(tril_ref, dt3 * (-jnp.exp(alog_ref[...])))
        w3 = dt3 * jnp.exp(s3[L - 1:L, :] - s3)
        lane_s = lax.broadcasted_iota(jnp.int32, (2 * L, SMALL_W), 1)
        x0, x1, _ = _residuals(jnp.concatenate([s3, w3], axis=0))
        split = jnp.where(lane_s < n_heads, x0, x1)[:, 0:DT_COPIES * n_heads].astype(BF16)
        bcast[...] = _dot(split, exp_ref[...])
        hh = n_heads // 2
        s_t = s3[:, 0:2 * L].T
        dt_t = dt3[:, 0:2 * L].T
        srow[...] = jnp.concatenate([s_t[0:hh], s_t[hh:2 * hh]], axis=1)
        dtrow[...] = jnp.concatenate([dt_t[0:hh], dt_t[hh:2 * hh]], axis=1)

        row = lax.broadcasted_iota(jnp.int32, (L, 2 * P), 0)
        lane = lax.broadcasted_iota(jnp.int32, (L, 2 * P), 1)
        causal2 = (lane % P) <= row
        left = lane < P
        ppg = gw // (2 * P)

        for g in range(G):
            gs = slice(g * gw, (g + 1) * gw)
            bm16 = bc16[:, g * N:(g + 1) * N]
            cm16 = bc16[:, (G + g) * N:(G + g + 1) * N]
            cb2 = _dot(cm16, jnp.concatenate([bm16, bm16], axis=0), _NT)
            st_g = stT[g][...]
            y_state = _dot(cm16, st_g.astype(BF16))
            ys = []
            for pp in range(ppg):
                p = g * ppg + pp
                sl = slice(p * 2 * P, (p + 1) * 2 * P)
                se = bcast[0:L, sl]
                dec = jnp.exp(jnp.where(causal2, se - srow[p:p + 1, :], -jnp.inf))
                m = (cb2 * dec * dtrow[p:p + 1, :]).astype(BF16)
                xp16 = xs16[:, sl]
                zero = jnp.zeros_like(xp16)
                rhs = jnp.concatenate([jnp.where(left, xp16, zero), jnp.where(left, zero, xp16)], axis=0)
                y = _dot(m, rhs)
                y = y + y_state[:, pp * 2 * P:(pp + 1) * 2 * P] * jnp.exp(se) + dsk_ref[:, sl] * xs_s[:, sl]
                ys.append(y)
            s_last = bcast[L - 1:L, gs]
            xdt_end = xs_s[:, gs] * bcast[L:2 * L, gs]
            upd = _dot(bm16, xdt_end.astype(BF16), _TN)
            stT[g][...] = st_g * jnp.exp(s_last) + upd

            yg = jnp.concatenate(ys, axis=1) * _silu_half(z_ref[0, pl.ds(c0, L), gs].astype(F32))
            ms = jnp.mean(yg * yg, axis=-1, keepdims=True)
            y_ref[0, pl.ds(c0, L), gs] = (yg * lax.rsqrt(ms + SSD_NORM_EPS) * gain_ref[:, gs]).astype(y_ref.dtype)

        tailx[...] = xs_ref[0, pl.ds(c0 + L - 2 * SUBLANES, 2 * SUBLANES), :].astype(F32)[SUBLANES:, :]
        tailb[...] = bc_ref[0, pl.ds(c0 + L - 2 * SUBLANES, 2 * SUBLANES), :].astype(F32)[SUBLANES:, :]

        return carry

    lax.fori_loop(0, n_sub, chunk, 0)

    @pl.when(c == n_chunks - 1)
    def _():
        for g in range(G):
            so_ref[0, g * gw:(g + 1) * gw, :] = stT[g][...].T


def _ssd(proj3, small3, col, prm, conv_state, ssd_state):
    b, t, _ = proj3.shape
    n_heads = prm["n_heads"]
    d_inner = n_heads * SSD_HEAD_DIM
    bcw = 2 * SSD_GROUPS * SSD_D_STATE
    has_state = ssd_state is not None
    L = CHUNK
    tb = _time_block(t)
    const = lambda *shape: pl.BlockSpec(shape, lambda i, j: (0,) * len(shape))
    in_specs = [
        pl.BlockSpec((1, tb, d_inner), lambda i, j: (i, j, col["xs"] // d_inner)),
        pl.BlockSpec((1, tb, d_inner), lambda i, j: (i, j, col["z"] // d_inner)),
        pl.BlockSpec((1, tb, bcw), lambda i, j: (i, j, col["bc"] // bcw)),
        pl.BlockSpec((1, tb, SMALL_W), lambda i, j: (i, j, 0)),
        const(SSD_CONV_W, 2 * SUBLANES, d_inner), const(2 * SUBLANES, d_inner),
        const(SSD_CONV_W, 2 * SUBLANES, bcw), const(2 * SUBLANES, bcw),
        const(1, SMALL_W), const(1, SMALL_W), const(1, d_inner), const(1, d_inner),
        const(DT_COPIES * n_heads, d_inner), const(L, 3 * L), const((SSD_CONV_W - 1) * L, L),
    ]
    args = [proj3, proj3, proj3, small3, prm["conv_w_x"], prm["conv_b_x"], prm["conv_w_bc"], prm["conv_b_bc"],
            prm["dt_bias3"], prm["a_log3"], prm["d_skip_e"], prm["ssd_norm_gain"], prm["expand3"], prm["tril3"],
            prm["shift"]]
    if has_state:
        nc = SSD_CONV_W - 1
        in_specs += [
            pl.BlockSpec((1, nc, d_inner), lambda i, j: (i, 0, 0)),
            pl.BlockSpec((1, nc, bcw), lambda i, j: (i, 0, 0)),
            pl.BlockSpec((1, d_inner, SSD_D_STATE), lambda i, j: (i, 0, 0)),
        ]
        args += [conv_state[:, :, :d_inner], conv_state[:, :, d_inner:],
                 ssd_state.reshape(b, d_inner, SSD_D_STATE)]
    y, st = pl.pallas_call(
        functools.partial(_ssd_kernel, has_state, n_heads, tb // L),
        grid=(b, t // tb),
        in_specs=in_specs,
        out_specs=[
            pl.BlockSpec((1, tb, d_inner), lambda i, j: (i, j, 0)),
            pl.BlockSpec((1, d_inner, SSD_D_STATE), lambda i, j: (i, 0, 0)),
        ],
        out_shape=[
            jax.ShapeDtypeStruct((b, t, d_inner), BF16),
            jax.ShapeDtypeStruct((b, d_inner, SSD_D_STATE), F32),
        ],
        scratch_shapes=[
            pltpu.VMEM((SUBLANES, d_inner), F32),
            pltpu.VMEM((SUBLANES, bcw), F32),
            pltpu.VMEM((n_heads // 2, 2 * L), F32),
            pltpu.VMEM((n_heads // 2, 2 * L), F32),
            pltpu.VMEM((L, d_inner), F32),
            pltpu.VMEM((L, d_inner), BF16),
            pltpu.VMEM((L, bcw), BF16),
            pltpu.VMEM((2 * L, d_inner), F32),
        ] + [pltpu.VMEM((SSD_D_STATE, d_inner // SSD_GROUPS), F32) for _ in range(SSD_GROUPS)],
        compiler_params=pltpu.CompilerParams(
            dimension_semantics=("arbitrary", "arbitrary"), vmem_limit_bytes=VMEM_LIMIT_BYTES),
    )(*args)
    return y, st.reshape(b, n_heads, SSD_HEAD_DIM, SSD_D_STATE)


def _gla_kernel(has_state, n_sub, *refs):
    if has_state:
        (q_ref, k_ref, v_ref, gb_ref, sm_ref, wgk_ref, bgk_ref, gain_ref, tril_ref, st_ref,
         o_ref, so_ref, bcs, ks, qs, *stT) = refs
    else:
        (q_ref, k_ref, v_ref, gb_ref, sm_ref, wgk_ref, bgk_ref, gain_ref, tril_ref,
         o_ref, so_ref, bcs, ks, qs, *stT) = refs
    c = pl.program_id(1)
    n_chunks = pl.num_programs(1)
    L = CHUNK
    H = GLA_HEADS
    kd = q_ref.shape[2]
    vd = v_ref.shape[2]
    hk = kd // H
    hv = vd // H
    nb = L // SUB

    @pl.when(c == 0)
    def _():
        if has_state:
            for h in range(H):
                stT[h][...] = st_ref[0, h].T
        else:
            for h in range(H):
                stT[h][...] = jnp.zeros_like(stT[h])

    def chunk(si, carry):
        c0 = pl.multiple_of(si * L, L)
        sm = sm_ref[0, pl.ds(c0, L), :]
        sm_hi = sm.astype(BF16)
        sm_lo = (sm - sm_hi.astype(F32)).astype(BF16)
        x = _dot(jnp.concatenate([sm_hi, sm_hi, sm_lo], axis=1), wgk_ref[...]) + bgk_ref[...]
        glog2 = (jnp.minimum(x, 0.0) - jnp.log(1.0 + jnp.exp(-jnp.abs(x)))) * (LOG2E / GLA_GATE_NORMALIZER)
        bc = _cumsum_rows(tril_ref, glog2)
        bcs[...] = bc
        bc_last = bc[L - 1:L, :]
        q = q_ref[0, pl.ds(c0, L), :].astype(F32) * (hk ** -0.5)
        k = k_ref[0, pl.ds(c0, L), :].astype(F32)
        ks[...] = k
        qs[...] = q
        qe = (q * jnp.exp2(bc)).astype(BF16)
        k_end = (k * jnp.exp2(bc_last - bc)).astype(BF16)

        nt = SUB // SUBLANES
        lane = lax.broadcasted_iota(jnp.int32, (SUBLANES, L), 1)
        row = lax.broadcasted_iota(jnp.int32, (SUBLANES, L), 0)
        a_rows = [[None] * nb for _ in range(H)]
        for blk in range(nb):
            r0 = blk * SUB
            off = [None] * H
            if blk > 0:
                ref_row = bcs[r0 - 1:r0, :]
                qd = (q[r0:r0 + SUB, :] * jnp.exp2(bc[r0:r0 + SUB, :] - ref_row)).astype(BF16)
                kt = k[0:r0, :] * jnp.exp2(ref_row - bc[0:r0, :])
                kt = jnp.concatenate([kt, jnp.zeros((L - r0, kd), F32)], axis=0).astype(BF16)
                for h in range(H):
                    off[h] = _dot(qd[:, h * hk:(h + 1) * hk], kt[:, h * hk:(h + 1) * hk], _NT)
            diag = [[jnp.zeros((SUBLANES, L), F32) for _ in range(nt)] for _ in range(H)]
            for jj in range(SUB):
                j = r0 + jj
                for ti in range(jj // SUBLANES, nt):
                    t0 = r0 + ti * SUBLANES
                    keep = (lane == j) & (row + t0 >= j)
                    for h in range(H):
                        tot = None
                        for lt in range(hk // LANES):
                            ls = slice(h * hk + lt * LANES, h * hk + (lt + 1) * LANES)
                            e = jnp.exp2(bcs[t0:t0 + SUBLANES, ls] - bcs[j:j + 1, ls])
                            pr = qs[t0:t0 + SUBLANES, ls] * e * ks[j:j + 1, ls]
                            tot = pr if tot is None else tot + pr
                        col = jnp.sum(tot, axis=-1, keepdims=True)
                        diag[h][ti] = jnp.where(keep, col, diag[h][ti])
            for h in range(H):
                d_h = jnp.concatenate(diag[h], axis=0)
                a_rows[h][blk] = d_h if off[h] is None else off[h] + d_h

        eb_last = jnp.exp2(bc_last)
        for h in range(H):
            a_h = jnp.concatenate(a_rows[h], axis=0).astype(BF16)
            v_h = v_ref[0, pl.ds(c0, L), h * hv:(h + 1) * hv]
            st_h = stT[h][...]
            o = _dot(a_h, v_h) + _dot(qe[:, h * hk:(h + 1) * hk], st_h.astype(BF16), _NT)
            stT[h][...] = st_h * eb_last[:, h * hk:(h + 1) * hk] + _dot(v_h, k_end[:, h * hk:(h + 1) * hk], _TN)
            ms = jnp.mean(o * o, axis=-1, keepdims=True)
            on = o * lax.rsqrt(ms + NORM_EPS) * gain_ref[...]
            gate = _silu_half(gb_ref[0, pl.ds(c0, L), h * hv:(h + 1) * hv].astype(F32))
            o_ref[0, pl.ds(c0, L), h * hv:(h + 1) * hv] = (on * gate).astype(o_ref.dtype)

        return carry

    lax.fori_loop(0, n_sub, chunk, 0)

    @pl.when(c == n_chunks - 1)
    def _():
        for h in range(H):
            so_ref[0, h] = stT[h][...].T


def _gla(proj3, small3, col, prm, gla_state):
    b, t, _ = proj3.shape
    kd = prm["gla_key_dim"]
    vd = prm["gla_val_dim"]
    H = GLA_HEADS
    hk, hv = kd // H, vd // H
    has_state = gla_state is not None
    L = CHUNK
    tb = _time_block(t)
    const = lambda *shape: pl.BlockSpec(shape, lambda i, j: (0,) * len(shape))
    in_specs = [
        pl.BlockSpec((1, tb, kd), lambda i, j: (i, j, col["q"] // kd)),
        pl.BlockSpec((1, tb, kd), lambda i, j: (i, j, col["k"] // kd)),
        pl.BlockSpec((1, tb, vd), lambda i, j: (i, j, col["v"] // vd)),
        pl.BlockSpec((1, tb, vd), lambda i, j: (i, j, col["g_b"] // vd)),
        pl.BlockSpec((1, tb, SMALL_W), lambda i, j: (i, j, 0)),
        const(3 * SMALL_W, kd), const(1, kd), const(1, hv), const(L, 3 * L),
    ]
    args = [proj3, proj3, proj3, proj3, small3, prm["gk_w3"], prm["gk_b"], prm["gla_norm_gain"], prm["tril3"]]
    if has_state:
        in_specs.append(pl.BlockSpec((1, H, hk, hv), lambda i, j: (i, 0, 0, 0)))
        args.append(gla_state)
    return pl.pallas_call(
        functools.partial(_gla_kernel, has_state, tb // L),
        grid=(b, t // tb),
        in_specs=in_specs,
        out_specs=[
            pl.BlockSpec((1, tb, vd), lambda i, j: (i, j, 0)),
            pl.BlockSpec((1, H, hk, hv), lambda i, j: (i, 0, 0, 0)),
        ],
        out_shape=[
            jax.ShapeDtypeStruct((b, t, vd), BF16),
            jax.ShapeDtypeStruct((b, H, hk, hv), F32),
        ],
        scratch_shapes=[
            pltpu.VMEM((L, kd), F32),
            pltpu.VMEM((L, kd), F32),
            pltpu.VMEM((L, kd), F32),
        ] + [pltpu.VMEM((hv, hk), F32) for _ in range(H)],
        compiler_params=pltpu.CompilerParams(
            dimension_semantics=("arbitrary", "arbitrary"), vmem_limit_bytes=VMEM_LIMIT_BYTES),
    )(*args)


def _merge_kernel(tn, ys_ref, yg_ref, g_ref, wbs_ref, wbg_ref, o_ref):
    d = o_ref.shape[1]
    for n in range(d // tn):
        ns = slice(n * tn, (n + 1) * tn)
        a = _dot(ys_ref[...], wbs_ref[:, ns])
        b = _dot(yg_ref[...], wbg_ref[:, ns])
        gate_s = _sigmoid_half(g_ref[:, ns].astype(F32))
        gate_g = _sigmoid_half(g_ref[:, d + n * tn:d + (n + 1) * tn].astype(F32))
        o_ref[:, ns] = (gate_s * a + gate_g * b).astype(o_ref.dtype)


def _post_kernel(mx_ref, x_ref, wo_ref, g_ref, o_ref):
    cc = _dot(mx_ref[...], wo_ref[...])
    ms = jnp.mean(cc * cc, axis=-1, keepdims=True)
    o_ref[...] = x_ref[...] + cc * lax.rsqrt(ms + NORM_EPS) * g_ref[...]


def _outproj(y_ssd, y_gla, proj, col, x2, prm, tm_merge, tn, tm_post):
    m, d = x2.shape
    d_inner = y_ssd.shape[1]
    vd = y_gla.shape[1]
    g0 = col["merge"] // (2 * d)
    resident = lambda r, c_: pl.BlockSpec((r, c_), lambda i: (0, 0), pipeline_mode=pl.Buffered(1))
    mixed = pl.pallas_call(
        functools.partial(_merge_kernel, tn),
        grid=(m // tm_merge,),
        in_specs=[
            pl.BlockSpec((tm_merge, d_inner), lambda i: (i, 0)),
            pl.BlockSpec((tm_merge, vd), lambda i: (i, 0)),
            pl.BlockSpec((tm_merge, 2 * d), lambda i: (i, g0)),
            resident(d_inner, d), resident(vd, d),
        ],
        out_specs=pl.BlockSpec((tm_merge, d), lambda i: (i, 0)),
        out_shape=jax.ShapeDtypeStruct((m, d), BF16),
        compiler_params=pltpu.CompilerParams(
            dimension_semantics=("arbitrary",), vmem_limit_bytes=VMEM_LIMIT_BYTES),
    )(y_ssd, y_gla, proj, prm["w_branch_ssd"], prm["w_branch_gla"])
    return pl.pallas_call(
        _post_kernel,
        grid=(m // tm_post,),
        in_specs=[
            pl.BlockSpec((tm_post, d), lambda i: (i, 0)),
            pl.BlockSpec((tm_post, d), lambda i: (i, 0)),
            pl.BlockSpec((d, d), lambda i: (0, 0)),
            pl.BlockSpec((1, d), lambda i: (0, 0)),
        ],
        out_specs=pl.BlockSpec((tm_post, d), lambda i: (i, 0)),
        out_shape=jax.ShapeDtypeStruct((m, d), F32),
        compiler_params=pltpu.CompilerParams(
            dimension_semantics=("arbitrary",), vmem_limit_bytes=VMEM_LIMIT_BYTES),
    )(mixed, x2, prm["w_out"], prm["norm_post_gain"])


def _regroup_kernel(tn, blk_ref, off_ref, half_ref, a_ref, b_ref, o_ref):
    j = pl.program_id(0)
    x = jnp.concatenate([a_ref[...], b_ref[...]], axis=1)
    width = tn + LANES
    x = pltpu.roll(x, (width - off_ref[j]) % width, 1)
    scale = jnp.where(half_ref[j] == 1, 0.5, 1.0).astype(F32)
    o_ref[...] = (x[:, 0:tn] * scale).astype(o_ref.dtype)


def _regroup_w_in(w_in, segments, tn):
    d = w_in.shape[0]
    blk, off, half = [], [], []
    for src, width, halve in segments:
        assert width % tn == 0 and src % tn < LANES
        for i in range(width // tn):
            blk.append(src // tn + i)
            off.append(src % tn)
            half.append(int(halve))
    n_blocks = len(blk)
    to_i32 = lambda v: jnp.asarray(np.array(v, np.int32))
    return pl.pallas_call(
        functools.partial(_regroup_kernel, tn),
        grid_spec=pltpu.PrefetchScalarGridSpec(
            num_scalar_prefetch=3,
            grid=(n_blocks,),
            in_specs=[
                pl.BlockSpec((d, tn), lambda j, blk, off, half: (0, blk[j])),
                pl.BlockSpec((d, LANES), lambda j, blk, off, half: (0, (blk[j] + 1) * (tn // LANES))),
            ],
            out_specs=pl.BlockSpec((d, tn), lambda j, blk, off, half: (0, j)),
        ),
        out_shape=jax.ShapeDtypeStruct((d, n_blocks * tn), BF16),
        compiler_params=pltpu.CompilerParams(
            dimension_semantics=("arbitrary",), vmem_limit_bytes=VMEM_LIMIT_BYTES),
    )(to_i32(blk), to_i32(off), to_i32(half), w_in, w_in)
def _prepare(norm_pre_gain, w_in, conv_w, conv_b, dt_bias, a_log, d_skip, ssd_norm_gain, gla_gk_w, gla_gk_b,
             gla_norm_gain, w_branch_ssd, w_branch_gla, w_out, norm_post_gain):
    d = w_in.shape[0]
    n_heads = dt_bias.shape[0]
    d_inner = n_heads * SSD_HEAD_DIM
    bcw = 2 * SSD_GROUPS * SSD_D_STATE
    kd = gla_gk_w.shape[1]
    vd = w_branch_gla.shape[0]
    assert DT_COPIES * n_heads + GLA_GATE_RANK <= SMALL_W
    sizes = (d_inner, d_inner + bcw, n_heads, kd, kd, vd, vd, GLA_GATE_RANK, 2 * d)
    offs = np.concatenate([[0], np.cumsum(sizes)])
    z0, xbc0, dt0, q0, k0, v0, gb0, gkl0, mg0 = [int(o_) for o_ in offs[:9]]
    perm = np.concatenate([np.arange(0, n_heads, 2), np.arange(1, n_heads, 2)])
    dt_w = w_in[:, dt0:dt0 + n_heads][:, perm]
    gk0 = DT_COPIES * n_heads
    pad = SMALL_W - gk0 - GLA_GATE_RANK
    w_small = jnp.concatenate(
        [dt_w] * DT_COPIES + [w_in[:, gkl0:gkl0 + GLA_GATE_RANK], jnp.zeros((d, pad), w_in.dtype)], axis=1).astype(BF16)
    half = 0.5
    pieces = [("xs", xbc0, d_inner, False), ("z", z0, d_inner, True), ("merge", mg0, 2 * d, True),
              ("bc", xbc0 + d_inner, bcw, False), ("v", v0, vd, False), ("g_b", gb0, vd, True),
              ("q", q0, kd, False), ("k", k0, kd, False)]
    col, o = {}, 0
    for name, _, width, _ in pieces:
        assert o % width == 0, name
        col[name] = o
        o += width
    w_main = _regroup_w_in(w_in, [p[1:] for p in pieces], 1024)

    def small_row(v, fill):
        return jnp.concatenate([v[perm]] * DT_COPIES + [jnp.full((SMALL_W - gk0,), fill, v.dtype)])[None, :]

    expand = np.zeros((DT_COPIES * n_heads, d_inner), np.float32)
    for r in range(DT_COPIES):
        for c_ in range(n_heads):
            expand[r * n_heads + c_, perm[c_] * SSD_HEAD_DIM:(perm[c_] + 1) * SSD_HEAD_DIM] = 1.0
    tril = np.tril(np.ones((CHUNK, CHUNK), np.float32))
    rb = 2 * SUBLANES
    shift = np.concatenate([np.eye(CHUNK, k=-k, dtype=np.float32)[b * rb:(b + 1) * rb]
                            for b in range(CHUNK // rb) for k in range(1, SSD_CONV_W)], axis=0)
    gk_w_pad = jnp.zeros((SMALL_W, kd), F32).at[gk0:gk0 + GLA_GATE_RANK].set(gla_gk_w)
    gk_hi = gk_w_pad.astype(BF16)
    gk_lo = (gk_w_pad - gk_hi.astype(F32)).astype(BF16)
    rows = lambda a: jnp.broadcast_to(a, a.shape[:-2] + (2 * SUBLANES, a.shape[-1]))
    prm = dict(
        n_heads=n_heads, gla_key_dim=kd, gla_val_dim=vd,
        norm_pre_gain=norm_pre_gain[None, :], w_main=w_main, w_small=w_small,
        conv_w_x=rows(half * conv_w[:, None, :d_inner]), conv_w_bc=rows(half * conv_w[:, None, d_inner:]),
        conv_b_x=rows(half * conv_b[None, :d_inner]), conv_b_bc=rows(half * conv_b[None, d_inner:]),
        dt_bias3=small_row(dt_bias, 0.0), a_log3=small_row(a_log, 0.0),
        d_skip_e=jnp.repeat(d_skip, SSD_HEAD_DIM)[None, :], ssd_norm_gain=ssd_norm_gain[None, :],
        shift=jnp.asarray(shift, BF16), expand3=jnp.asarray(expand, BF16), tril3=jnp.asarray(np.concatenate([tril] * 3, axis=1), BF16),
        gk_w3=jnp.concatenate([gk_hi, gk_lo, gk_hi], axis=0), gk_b=gla_gk_b[None, :],
        gla_norm_gain=gla_norm_gain[None, :],
        w_branch_ssd=w_branch_ssd.astype(BF16), w_branch_gla=w_branch_gla.astype(BF16),
        w_out=w_out.astype(BF16), norm_post_gain=norm_post_gain[None, :],
    )
    return prm, col


def _time_block(t):
    tb = min(t, CHUNKS_PER_STEP * CHUNK)
    assert t % tb == 0
    return tb


def _row_tile(m, want):
    tm = min(m, want)
    assert m % tm == 0
    return tm


def _layer(x, conv_state, ssd_state, gla_state, prm, col):
    b, t, d = x.shape
    assert t % CHUNK == 0 and t >= SSD_CONV_W - 1
    m = b * t
    x2 = x.reshape(m, d)
    proj, small = _inproj(x2, prm["norm_pre_gain"], prm["w_main"], prm["w_small"], _row_tile(m, 1024), 2048)
    proj3 = proj.reshape(b, t, -1)
    small3 = small.reshape(b, t, -1)
    d_inner = prm["n_heads"] * SSD_HEAD_DIM
    bcw = 2 * SSD_GROUPS * SSD_D_STATE
    nc = SSD_CONV_W - 1
    new_conv = jnp.concatenate(
        [proj3[:, t - nc:, col["xs"]:col["xs"] + d_inner], proj3[:, t - nc:, col["bc"]:col["bc"] + bcw]],
        axis=-1).astype(x.dtype)
    y_ssd, new_ssd = _ssd(proj3, small3, col, prm, conv_state, ssd_state)
    y_gla, new_gla = _gla(proj3, small3, col, prm, gla_state)
    y = _outproj(y_ssd.reshape(m, -1), y_gla.reshape(m, -1), proj, col, x2, prm,
                 _row_tile(m, 256), 1024, _row_tile(m, 512))
    return y.reshape(b, t, d), new_conv, new_ssd, new_gla


def kernel(x_prompt, x_sample, state_conv_ssd, state_ssd, state_gla, norm_pre_gain, w_in, conv_w, conv_b, dt_bias,
           a_log, d_skip, ssd_norm_gain, gla_gk_w, gla_gk_b, gla_norm_gain, w_branch_ssd, w_branch_gla, w_out,
           norm_post_gain):
    depth = w_in.shape[0]
    yp, ys = x_prompt, x_sample
    outs = [[] for _ in range(6)]
    for layer in range(depth):
        prm, col = _prepare(norm_pre_gain[layer], w_in[layer], conv_w[layer], conv_b[layer], dt_bias[layer],
                            a_log[layer], d_skip[layer], ssd_norm_gain[layer], gla_gk_w[layer], gla_gk_b[layer],
                            gla_norm_gain[layer], w_branch_ssd[layer], w_branch_gla[layer], w_out[layer],
                            norm_post_gain[layer])
        yp, c_p, s_p, g_p = _layer(yp, None, None, None, prm, col)
        ys, c_s, s_s, g_s = _layer(ys, state_conv_ssd[layer], state_ssd[layer], state_gla[layer], prm, col)
        for lst, val in zip(outs, (c_p, s_p, g_p, c_s, s_s, g_s)):
            lst.append(val)
    return (yp, ys) + tuple(jnp.stack(o) for o in outs)
```

```python
import functools
import math

import numpy as np
import jax
import jax.numpy as jnp
from jax import lax
from jax.experimental import pallas as pl
from jax.experimental.pallas import tpu as pltpu

F32 = jnp.float32
BF16 = jnp.bfloat16

LANES = 128
SUBLANES = 8
VMEM_LIMIT_BYTES = 56 * 1024 * 1024

CHUNK = 64
SSD_HEAD_DIM = 64
SSD_GROUPS = 8
SSD_D_STATE = 128
SSD_CONV_W = 4
GLA_HEADS = 4
GLA_GATE_RANK = 16
GLA_GATE_NORMALIZER = 16.0
NORM_EPS = 1e-6
SSD_NORM_EPS = 1e-5
CHUNKS_PER_STEP = 4
SUB = 16
SMALL_W = 256
DT_COPIES = 2
LOG2E = math.log2(math.e)


def _dot(a, b, dims=(((1,), (0,)), ((), ()))):
    return lax.dot_general(a, b, dims, preferred_element_type=F32)


_NT = (((1,), (1,)), ((), ()))
_TN = (((0,), (0,)), ((), ()))


def _sigmoid_half(h):
    return 0.5 * jnp.tanh(h) + 0.5


def _silu_half(h):
    return h * jnp.tanh(h) + h


def _softplus(x):
    return jnp.maximum(x, 0.0) + jnp.log(1.0 + jnp.exp(-jnp.abs(x)))


def _residuals(x):
    r1 = x - x.astype(BF16).astype(F32)
    r2 = r1 - r1.astype(BF16).astype(F32)
    return x, r1, r2


def _cumsum_rows(tril3_ref, x):
    parts = [p.astype(BF16) for p in _residuals(x)]
    return _dot(tril3_ref[...], jnp.concatenate(parts, axis=0))


def _inproj_kernel(x_ref, g_ref, w_ref, ws_ref, o_ref, os_ref, h_ref):
    @pl.when(pl.program_id(1) == 0)
    def _():
        x = x_ref[...]
        ms = jnp.mean(x * x, axis=-1, keepdims=True)
        h_ref[...] = (x * lax.rsqrt(ms + NORM_EPS) * g_ref[...]).astype(BF16)
        os_ref[...] = _dot(h_ref[...], ws_ref[...])

    o_ref[...] = _dot(h_ref[...], w_ref[...]).astype(o_ref.dtype)


def _inproj(x2, gain, w_main, w_small, tm, tn):
    m, d = x2.shape
    n = w_main.shape[1]
    ns = w_small.shape[1]
    return pl.pallas_call(
        _inproj_kernel,
        grid=(m // tm, n // tn),
        in_specs=[
            pl.BlockSpec((tm, d), lambda i, j: (i, 0)),
            pl.BlockSpec((1, d), lambda i, j: (0, 0)),
            pl.BlockSpec((d, tn), lambda i, j: (0, j)),
            pl.BlockSpec((d, ns), lambda i, j: (0, 0)),
        ],
        out_specs=[
            pl.BlockSpec((tm, tn), lambda i, j: (i, j)),
            pl.BlockSpec((tm, ns), lambda i, j: (i, 0)),
        ],
        out_shape=[jax.ShapeDtypeStruct((m, n), BF16), jax.ShapeDtypeStruct((m, ns), F32)],
        scratch_shapes=[pltpu.VMEM((tm, d), BF16)],
        compiler_params=pltpu.CompilerParams(
            dimension_semantics=("arbitrary", "arbitrary"), vmem_limit_bytes=VMEM_LIMIT_BYTES),
    )(x2, gain, w_main, w_small)


def _ssd_kernel(has_state, n_heads, n_sub, *refs):
    if has_state:
        (xs_ref, z_ref, bc_ref, sm_ref, cwx_ref, cbx_ref, cwb_ref, cbb_ref, dtb_ref, alog_ref, dsk_ref,
         gain_ref, exp_ref, tril_ref, shift_ref, csx_ref, csb_ref, st_ref,
         y_ref, so_ref, tailx, tailb, srow, dtrow, xs_s, xs16, bc16, bcast, *stT) = refs
    else:
        (xs_ref, z_ref, bc_ref, sm_ref, cwx_ref, cbx_ref, cwb_ref, cbb_ref, dtb_ref, alog_ref, dsk_ref,
         gain_ref, exp_ref, tril_ref, shift_ref,
         y_ref, so_ref, tailx, tailb, srow, dtrow, xs_s, xs16, bc16, bcast, *stT) = refs
    c = pl.program_id(1)
    n_chunks = pl.num_programs(1)
    L = CHUNK
    P = SSD_HEAD_DIM
    N = SSD_D_STATE
    G = SSD_GROUPS
    gw = (n_heads // G) * P
    d_inner = n_heads * P
    nc = SSD_CONV_W - 1

    @pl.when(c == 0)
    def _():
        tailx[...] = jnp.zeros_like(tailx)
        tailb[...] = jnp.zeros_like(tailb)
        if has_state:
            tailx[SUBLANES - nc:SUBLANES, :] = csx_ref[0]
            tailb[SUBLANES - nc:SUBLANES, :] = csb_ref[0]
            for g in range(G):
                stT[g][...] = st_ref[0, g * gw:(g + 1) * gw, :].T
        else:
            for g in range(G):
                stT[g][...] = jnp.zeros_like(stT[g])

    def chunk(si, carry):
        c0 = pl.multiple_of(si * L, L)
        cw = 4 * LANES
        rb = 2 * SUBLANES
        row8 = lax.broadcasted_iota(jnp.int32, (SUBLANES, cw), 0)

        def conv(x_ref, tail, w_ref, b_ref, width, emit):
            for cb in range(width // cw):
                cs = slice(cb * cw, (cb + 1) * cw)
                sh = _dot(shift_ref[...], x_ref[0, pl.ds(c0, L), cs])
                prev = tail[:, cs]
                fix = jnp.zeros((SUBLANES, cw), F32)
                for k in range(1, SSD_CONV_W):
                    fix = fix + jnp.where(row8 < k, pltpu.roll(prev, k, 0), 0.0) * w_ref[nc - k, 0:SUBLANES, cs]
                for blk in range(L // rb):
                    acc = b_ref[:, cs] + x_ref[0, pl.ds(c0 + blk * rb, rb), cs].astype(F32) * w_ref[nc, :, cs]
                    for k in range(1, SSD_CONV_W):
                        s0 = (blk * nc + k - 1) * rb
                        acc = acc + sh[s0:s0 + rb, :] * w_ref[nc - k, :, cs]
                    if blk == 0:
                        acc = jnp.concatenate([acc[0:SUBLANES, :] + fix, acc[SUBLANES:, :]], axis=0)
                    emit(blk * rb, rb, cs, _silu_half(acc))

        def emit_x(r0, nr, cs, act):
            xs_s[r0:r0 + nr, cs] = act
            xs16[r0:r0 + nr, cs] = act.astype(BF16)

        def emit_bc(r0, nr, cs, act):
            bc16[r0:r0 + nr, cs] = act.astype(BF16)

        conv(xs_ref, tailx, cwx_ref, cbx_ref, d_inner, emit_x)
        conv(bc_ref, tailb, cwb_ref, cbb_ref, 2 * G * N, emit_bc)

        dt3 = _softplus(sm_ref[0, pl.ds(c0, L), :] + dtb_ref[...])
        s3 = _cumsum_rows(tril_ref, dt3 * (-jnp.exp(alog_ref[...])))
        w3 = dt3 * jnp.exp(s3[L - 1:L, :] - s3)
        lane_s = lax.broadcasted_iota(jnp.int32, (2 * L, SMALL_W), 1)
        x0, x1, _ = _residuals(jnp.concatenate([s3, w3], axis=0))
        split = jnp.where(lane_s < n_heads, x0, x1)[:, 0:DT_COPIES * n_heads].astype(BF16)
        bcast[...] = _dot(split, exp_ref[...])
        hh = n_heads // 2
        s_t = s3[:, 0:2 * L].T
        dt_t = dt3[:, 0:2 * L].T
        srow[...] = jnp.concatenate([s_t[0:hh], s_t[hh:2 * hh]], axis=1)
        dtrow[...] = jnp.concatenate([dt_t[0:hh], dt_t[hh:2 * hh]], axis=1)

        row = lax.broadcasted_iota(jnp.int32, (L, 2 * P), 0)
        lane = lax.broadcasted_iota(jnp.int32, (L, 2 * P), 1)
        causal2 = (lane % P) <= row
        left = lane < P
        ppg = gw // (2 * P)

        for g in range(G):
            gs = slice(g * gw, (g + 1) * gw)
            bm16 = bc16[:, g * N:(g + 1) * N]
            cm16 = bc16[:, (G + g) * N:(G + g + 1) * N]
            cb2 = _dot(cm16, jnp.concatenate([bm16, bm16], axis=0), _NT)
            st_g = stT[g][...]
            y_state = _dot(cm16, st_g.astype(BF16))
            ys = []
            for pp in range(ppg):
                p = g * ppg + pp
                sl = slice(p * 2 * P, (p + 1) * 2 * P)
                se = bcast[0:L, sl]
                dec = jnp.exp(jnp.where(causal2, se - srow[p:p + 1, :], -jnp.inf))
                m = (cb2 * dec * dtrow[p:p + 1, :]).astype(BF16)
                xp16 = xs16[:, sl]
                zero = jnp.zeros_like(xp16)
                rhs = jnp.concatenate([jnp.where(left, xp16, zero), jnp.where(left, zero, xp16)], axis=0)
                y = _dot(m, rhs)
                y = y + y_state[:, pp * 2 * P:(pp + 1) * 2 * P] * jnp.exp(se) + dsk_ref[:, sl] * xs_s[:, sl]
                ys.append(y)
            s_last = bcast[L - 1:L, gs]
            xdt_end = xs_s[:, gs] * bcast[L:2 * L, gs]
            upd = _dot(bm16, xdt_end.astype(BF16), _TN)
            stT[g][...] = st_g * jnp.exp(s_last) + upd

            yg = jnp.concatenate(ys, axis=1) * _silu_half(z_ref[0, pl.ds(c0, L), gs].astype(F32))
            ms = jnp.mean(yg * yg, axis=-1, keepdims=True)
            y_ref[0, pl.ds(c0, L), gs] = (yg * lax.rsqrt(ms + SSD_NORM_EPS) * gain_ref[:, gs]).astype(y_ref.dtype)

        tailx[...] = xs_ref[0, pl.ds(c0 + L - 2 * SUBLANES, 2 * SUBLANES), :].astype(F32)[SUBLANES:, :]
        tailb[...] = bc_ref[0, pl.ds(c0 + L - 2 * SUBLANES, 2 * SUBLANES), :].astype(F32)[SUBLANES:, :]

        return carry

    lax.fori_loop(0, n_sub, chunk, 0)

    @pl.when(c == n_chunks - 1)
    def _():
        for g in range(G):
            so_ref[0, g * gw:(g + 1) * gw, :] = stT[g][...].T


def _ssd(proj3, small3, col, prm, conv_state, ssd_state):
    b, t, _ = proj3.shape
    n_heads = prm["n_heads"]
    d_inner = n_heads * SSD_HEAD_DIM
    bcw = 2 * SSD_GROUPS * SSD_D_STATE
    has_state = ssd_state is not None
    L = CHUNK
    tb = _time_block(t)
    const = lambda *shape: pl.BlockSpec(shape, lambda i, j: (0,) * len(shape))
    in_specs = [
        pl.BlockSpec((1, tb, d_inner), lambda i, j: (i, j, col["xs"] // d_inner)),
        pl.BlockSpec((1, tb, d_inner), lambda i, j: (i, j, col["z"] // d_inner)),
        pl.BlockSpec((1, tb, bcw), lambda i, j: (i, j, col["bc"] // bcw)),
        pl.BlockSpec((1, tb, SMALL_W), lambda i, j: (i, j, 0)),
        const(SSD_CONV_W, 2 * SUBLANES, d_inner), const(2 * SUBLANES, d_inner),
        const(SSD_CONV_W, 2 * SUBLANES, bcw), const(2 * SUBLANES, bcw),
        const(1, SMALL_W), const(1, SMALL_W), const(1, d_inner), const(1, d_inner),
        const(DT_COPIES * n_heads, d_inner), const(L, 3 * L), const((SSD_CONV_W - 1) * L, L),
    ]
    args = [proj3, proj3, proj3, small3, prm["conv_w_x"], prm["conv_b_x"], prm["conv_w_bc"], prm["conv_b_bc"],
            prm["dt_bias3"], prm["a_log3"], prm["d_skip_e"], prm["ssd_norm_gain"], prm["expand3"], prm["tril3"],
            prm["shift"]]
    if has_state:
        nc = SSD_CONV_W - 1
        in_specs += [
            pl.BlockSpec((1, nc, d_inner), lambda i, j: (i, 0, 0)),
            pl.BlockSpec((1, nc, bcw), lambda i, j: (i, 0, 0)),
            pl.BlockSpec((1, d_inner, SSD_D_STATE), lambda i, j: (i, 0, 0)),
        ]
        args += [conv_state[:, :, :d_inner], conv_state[:, :, d_inner:],
                 ssd_state.reshape(b, d_inner, SSD_D_STATE)]
    y, st = pl.pallas_call(
        functools.partial(_ssd_kernel, has_state, n_heads, tb // L),
        grid=(b, t // tb),
        in_specs=in_specs,
        out_specs=[
            pl.BlockSpec((1, tb, d_inner), lambda i, j: (i, j, 0)),
            pl.BlockSpec((1, d_inner, SSD_D_STATE), lambda i, j: (i, 0, 0)),
        ],
        out_shape=[
            jax.ShapeDtypeStruct((b, t, d_inner), BF16),
            jax.ShapeDtypeStruct((b, d_inner, SSD_D_STATE), F32),
        ],
        scratch_shapes=[
            pltpu.VMEM((SUBLANES, d_inner), F32),
            pltpu.VMEM((SUBLANES, bcw), F32),
            pltpu.VMEM((n_heads // 2, 2 * L), F32),
            pltpu.VMEM((n_heads // 2, 2 * L), F32),
            pltpu.VMEM((L, d_inner), F32),
            pltpu.VMEM((L, d_inner), BF16),
            pltpu.VMEM((L, bcw), BF16),
            pltpu.VMEM((2 * L, d_inner), F32),
        ] + [pltpu.VMEM((SSD_D_STATE, d_inner // SSD_GROUPS), F32) for _ in range(SSD_GROUPS)],
        compiler_params=pltpu.CompilerParams(
            dimension_semantics=("arbitrary", "arbitrary"), vmem_limit_bytes=VMEM_LIMIT_BYTES),
    )(*args)
    return y, st.reshape(b, n_heads, SSD_HEAD_DIM, SSD_D_STATE)


def _gla_kernel(has_state, n_sub, *refs):
    if has_state:
        (q_ref, k_ref, v_ref, gb_ref, sm_ref, wgk_ref, bgk_ref, gain_ref, tril_ref, st_ref,
         o_ref, so_ref, bcs, ks, qs, *stT) = refs
    else:
        (q_ref, k_ref, v_ref, gb_ref, sm_ref, wgk_ref, bgk_ref, gain_ref, tril_ref,
         o_ref, so_ref, bcs, ks, qs, *stT) = refs
    c = pl.program_id(1)
    n_chunks = pl.num_programs(1)
    L = CHUNK
    H = GLA_HEADS
    kd = q_ref.shape[2]
    vd = v_ref.shape[2]
    hk = kd // H
    hv = vd // H
    nb = L // SUB

    @pl.when(c == 0)
    def _():
        if has_state:
            for h in range(H):
                stT[h][...] = st_ref[0, h].T
        else:
            for h in range(H):
                stT[h][...] = jnp.zeros_like(stT[h])

    def chunk(si, carry):
        c0 = pl.multiple_of(si * L, L)
        sm = sm_ref[0, pl.ds(c0, L), :]
        sm_hi = sm.astype(BF16)
        sm_lo = (sm - sm_hi.astype(F32)).astype(BF16)
        x = _dot(jnp.concatenate([sm_hi, sm_hi, sm_lo], axis=1), wgk_ref[...]) + bgk_ref[...]
        glog2 = (jnp.minimum(x, 0.0) - jnp.log(1.0 + jnp.exp(-jnp.abs(x)))) * (LOG2E / GLA_GATE_NORMALIZER)
        bc = _cumsum_rows(tril_ref, glog2)
        bcs[...] = bc
        bc_last = bc[L - 1:L, :]
        q = q_ref[0, pl.ds(c0, L), :].astype(F32) * (hk ** -0.5)
        k = k_ref[0, pl.ds(c0, L), :].astype(F32)
        ks[...] = k
        qs[...] = q
        qe = (q * jnp.exp2(bc)).astype(BF16)
        k_end = (k * jnp.exp2(bc_last - bc)).astype(BF16)

        nt = SUB // SUBLANES
        lane = lax.broadcasted_iota(jnp.int32, (SUBLANES, L), 1)
        row = lax.broadcasted_iota(jnp.int32, (SUBLANES, L), 0)
        a_rows = [[None] * nb for _ in range(H)]
        for blk in range(nb):
            r0 = blk * SUB
            off = [None] * H
            if blk > 0:
                ref_row = bcs[r0 - 1:r0, :]
                qd = (q[r0:r0 + SUB, :] * jnp.exp2(bc[r0:r0 + SUB, :] - ref_row)).astype(BF16)
                kt = k[0:r0, :] * jnp.exp2(ref_row - bc[0:r0, :])
                kt = jnp.concatenate([kt, jnp.zeros((L - r0, kd), F32)], axis=0).astype(BF16)
                for h in range(H):
                    off[h] = _dot(qd[:, h * hk:(h + 1) * hk], kt[:, h * hk:(h + 1) * hk], _NT)
            diag = [[jnp.zeros((SUBLANES, L), F32) for _ in range(nt)] for _ in range(H)]
            for jj in range(SUB):
                j = r0 + jj
                for ti in range(jj // SUBLANES, nt):
                    t0 = r0 + ti * SUBLANES
                    keep = (lane == j) & (row + t0 >= j)
                    for h in range(H):
                        tot = None
                        for lt in range(hk // LANES):
                            ls = slice(h * hk + lt * LANES, h * hk + (lt + 1) * LANES)
                            e = jnp.exp2(bcs[t0:t0 + SUBLANES, ls] - bcs[j:j + 1, ls])
                            pr = qs[t0:t0 + SUBLANES, ls] * e * ks[j:j + 1, ls]
                            tot = pr if tot is None else tot + pr
                        col = jnp.sum(tot, axis=-1, keepdims=True)
                        diag[h][ti] = jnp.where(keep, col, diag[h][ti])
            for h in range(H):
                d_h = jnp.concatenate(diag[h], axis=0)
                a_rows[h][blk] = d_h if off[h] is None else off[h] + d_h

        eb_last = jnp.exp2(bc_last)
        for h in range(H):
            a_h = jnp.concatenate(a_rows[h], axis=0).astype(BF16)
            v_h = v_ref[0, pl.ds(c0, L), h * hv:(h + 1) * hv]
            st_h = stT[h][...]
            o = _dot(a_h, v_h) + _dot(qe[:, h * hk:(h + 1) * hk], st_h.astype(BF16), _NT)
            stT[h][...] = st_h * eb_last[:, h * hk:(h + 1) * hk] + _dot(v_h, k_end[:, h * hk:(h + 1) * hk], _TN)
            ms = jnp.mean(o * o, axis=-1, keepdims=True)
            on = o * lax.rsqrt(ms + NORM_EPS) * gain_ref[...]
            gate = _silu_half(gb_ref[0, pl.ds(c0, L), h * hv:(h + 1) * hv].astype(F32))
            o_ref[0, pl.ds(c0, L), h * hv:(h + 1) * hv] = (on * gate).astype(o_ref.dtype)

        return carry

    lax.fori_loop(0, n_sub, chunk, 0)

    @pl.when(c == n_chunks - 1)
    def _():
        for h in range(H):
            so_ref[0, h] = stT[h][...].T


def _gla(proj3, small3, col, prm, gla_state):
    b, t, _ = proj3.shape
    kd = prm["gla_key_dim"]
    vd = prm["gla_val_dim"]
    H = GLA_HEADS
    hk, hv = kd // H, vd // H
    has_state = gla_state is not None
    L = CHUNK
    tb = _time_block(t)
    const = lambda *shape: pl.BlockSpec(shape, lambda i, j: (0,) * len(shape))
    in_specs = [
        pl.BlockSpec((1, tb, kd), lambda i, j: (i, j, col["q"] // kd)),
        pl.BlockSpec((1, tb, kd), lambda i, j: (i, j, col["k"] // kd)),
        pl.BlockSpec((1, tb, vd), lambda i, j: (i, j, col["v"] // vd)),
        pl.BlockSpec((1, tb, vd), lambda i, j: (i, j, col["g_b"] // vd)),
        pl.BlockSpec((1, tb, SMALL_W), lambda i, j: (i, j, 0)),
        const(3 * SMALL_W, kd), const(1, kd), const(1, hv), const(L, 3 * L),
    ]
    args = [proj3, proj3, proj3, proj3, small3, prm["gk_w3"], prm["gk_b"], prm["gla_norm_gain"], prm["tril3"]]
    if has_state:
        in_specs.append(pl.BlockSpec((1, H, hk, hv), lambda i, j: (i, 0, 0, 0)))
        args.append(gla_state)
    return pl.pallas_call(
        functools.partial(_gla_kernel, has_state, tb // L),
        grid=(b, t // tb),
        in_specs=in_specs,
        out_specs=[
            pl.BlockSpec((1, tb, vd), lambda i, j: (i, j, 0)),
            pl.BlockSpec((1, H, hk, hv), lambda i, j: (i, 0, 0, 0)),
        ],
        out_shape=[
            jax.ShapeDtypeStruct((b, t, vd), BF16),
            jax.ShapeDtypeStruct((b, H, hk, hv), F32),
        ],
        scratch_shapes=[
            pltpu.VMEM((L, kd), F32),
            pltpu.VMEM((L, kd), F32),
            pltpu.VMEM((L, kd), F32),
        ] + [pltpu.VMEM((hv, hk), F32) for _ in range(H)],
        compiler_params=pltpu.CompilerParams(
            dimension_semantics=("arbitrary", "arbitrary"), vmem_limit_bytes=VMEM_LIMIT_BYTES),
    )(*args)


def _merge_kernel(tn, ys_ref, yg_ref, g_ref, wbs_ref, wbg_ref, o_ref):
    d = o_ref.shape[1]
    for n in range(d // tn):
        ns = slice(n * tn, (n + 1) * tn)
        a = _dot(ys_ref[...], wbs_ref[:, ns])
        b = _dot(yg_ref[...], wbg_ref[:, ns])
        gate_s = _sigmoid_half(g_ref[:, ns].astype(F32))
        gate_g = _sigmoid_half(g_ref[:, d + n * tn:d + (n + 1) * tn].astype(F32))
        o_ref[:, ns] = (gate_s * a + gate_g * b).astype(o_ref.dtype)


def _post_kernel(mx_ref, x_ref, wo_ref, g_ref, o_ref):
    cc = _dot(mx_ref[...], wo_ref[...])
    ms = jnp.mean(cc * cc, axis=-1, keepdims=True)
    o_ref[...] = x_ref[...] + cc * lax.rsqrt(ms + NORM_EPS) * g_ref[...]


def _outproj(y_ssd, y_gla, proj, col, x2, prm, tm_merge, tn, tm_post):
    m, d = x2.shape
    d_inner = y_ssd.shape[1]
    vd = y_gla.shape[1]
    g0 = col["merge"] // (2 * d)
    resident = lambda r, c_: pl.BlockSpec((r, c_), lambda i: (0, 0), pipeline_mode=pl.Buffered(1))
    mixed = pl.pallas_call(
        functools.partial(_merge_kernel, tn),
        grid=(m // tm_merge,),
        in_specs=[
            pl.BlockSpec((tm_merge, d_inner), lambda i: (i, 0)),
            pl.BlockSpec((tm_merge, vd), lambda i: (i, 0)),
            pl.BlockSpec((tm_merge, 2 * d), lambda i: (i, g0)),
            resident(d_inner, d), resident(vd, d),
        ],
        out_specs=pl.BlockSpec((tm_merge, d), lambda i: (i, 0)),
        out_shape=jax.ShapeDtypeStruct((m, d), BF16),
        compiler_params=pltpu.CompilerParams(
            dimension_semantics=("arbitrary",), vmem_limit_bytes=VMEM_LIMIT_BYTES),
    )(y_ssd, y_gla, proj, prm["w_branch_ssd"], prm["w_branch_gla"])
    return pl.pallas_call(
        _post_kernel,
        grid=(m // tm_post,),
        in_specs=[
            pl.BlockSpec((tm_post, d), lambda i: (i, 0)),
            pl.BlockSpec((tm_post, d), lambda i: (i, 0)),
            pl.BlockSpec((d, d), lambda i: (0, 0)),
            pl.BlockSpec((1, d), lambda i: (0, 0)),
        ],
        out_specs=pl.BlockSpec((tm_post, d), lambda i: (i, 0)),
        out_shape=jax.ShapeDtypeStruct((m, d), F32),
        compiler_params=pltpu.CompilerParams(
            dimension_semantics=("arbitrary",), vmem_limit_bytes=VMEM_LIMIT_BYTES),
    )(mixed, x2, prm["w_out"], prm["norm_post_gain"])


def _regroup_kernel(tn, offsets, blk_ref, cls_ref, half_ref, a_ref, b_ref, o_ref):
    j = pl.program_id(0)
    scale = jnp.where(half_ref[j] == 1, 0.5, 1.0).astype(F32)
    for ci, off in enumerate(offsets):
        @pl.when(cls_ref[j] == ci)
        def _():
            x = a_ref[...] if off == 0 else jnp.concatenate([a_ref[off:tn, :], b_ref[0:off, :]], axis=0)
            o_ref[...] = (x.T * scale).astype(o_ref.dtype)


def _pick_rows_kernel(spans, *refs):
    for (start, size), src, dst in zip(spans, refs[:len(spans)], refs[len(spans):]):
        dst[...] = src[start % LANES:start % LANES + size, :]


def _pick_rows(a, *spans):
    d = a.shape[1]
    assert all(start % LANES + size <= LANES and size % SUBLANES == 0 for start, size in spans)
    return pl.pallas_call(
        functools.partial(_pick_rows_kernel, spans),
        grid=(1,),
        in_specs=[pl.BlockSpec((LANES, d), functools.partial(lambda blk, i: (blk, 0), start // LANES))
                  for start, _ in spans],
        out_specs=[pl.BlockSpec((size, d), lambda i: (0, 0)) for _, size in spans],
        out_shape=[jax.ShapeDtypeStruct((size, d), a.dtype) for _, size in spans],
    )(*([a] * len(spans)))


def _regroup_w_in(w_in_t, segments, tn):
    d = w_in_t.shape[1]
    offsets = sorted({src % tn for src, _, _ in segments})
    blk, cls, half = [], [], []
    for src, width, halve in segments:
        assert width % tn == 0 and src % tn < LANES and (src % tn) % SUBLANES == 0
        for i in range(width // tn):
            blk.append(src // tn + i)
            cls.append(offsets.index(src % tn))
            half.append(int(halve))
    n_blocks = len(blk)
    to_i32 = lambda v: jnp.asarray(np.array(v, np.int32))
    return pl.pallas_call(
        functools.partial(_regroup_kernel, tn, tuple(offsets)),
        grid_spec=pltpu.PrefetchScalarGridSpec(
            num_scalar_prefetch=3,
            grid=(n_blocks,),
            in_specs=[
                pl.BlockSpec((tn, d), lambda j, blk, cls, half: (blk[j], 0)),
                pl.BlockSpec((LANES, d), lambda j, blk, cls, half: ((blk[j] + 1) * (tn // LANES), 0)),
            ],
            out_specs=pl.BlockSpec((d, tn), lambda j, blk, cls, half: (0, j)),
        ),
        out_shape=jax.ShapeDtypeStruct((d, n_blocks * tn), BF16),
        compiler_params=pltpu.CompilerParams(
            dimension_semantics=("arbitrary",), vmem_limit_bytes=VMEM_LIMIT_BYTES),
    )(to_i32(blk), to_i32(cls), to_i32(half), w_in_t, w_in_t)
def _prepare(norm_pre_gain, w_in, conv_w, conv_b, dt_bias, a_log, d_skip, ssd_norm_gain, gla_gk_w, gla_gk_b,
             gla_norm_gain, w_branch_ssd, w_branch_gla, w_out, norm_post_gain):
    d = w_in.shape[0]
    n_heads = dt_bias.shape[0]
    d_inner = n_heads * SSD_HEAD_DIM
    bcw = 2 * SSD_GROUPS * SSD_D_STATE
    kd = gla_gk_w.shape[1]
    vd = w_branch_gla.shape[0]
    assert DT_COPIES * n_heads + GLA_GATE_RANK <= SMALL_W
    sizes = (d_inner, d_inner + bcw, n_heads, kd, kd, vd, vd, GLA_GATE_RANK, 2 * d)
    offs = np.concatenate([[0], np.cumsum(sizes)])
    z0, xbc0, dt0, q0, k0, v0, gb0, gkl0, mg0 = [int(o_) for o_ in offs[:9]]
    perm = np.concatenate([np.arange(0, n_heads, 2), np.arange(1, n_heads, 2)])
    w_in_t = w_in.T
    dt_rows, gk_rows = _pick_rows(w_in_t, (dt0, n_heads), (gkl0, GLA_GATE_RANK))
    dt_w = dt_rows[perm]
    gk0 = DT_COPIES * n_heads
    pad = SMALL_W - gk0 - GLA_GATE_RANK
    w_small = jnp.concatenate(
        [dt_w] * DT_COPIES + [gk_rows, jnp.zeros((pad, d), w_in.dtype)], axis=0).T.astype(BF16)
    half = 0.5
    pieces = [("xs", xbc0, d_inner, False), ("z", z0, d_inner, True), ("merge", mg0, 2 * d, True),
              ("bc", xbc0 + d_inner, bcw, False), ("v", v0, vd, False), ("g_b", gb0, vd, True),
              ("q", q0, kd, False), ("k", k0, kd, False)]
    col, o = {}, 0
    for name, _, width, _ in pieces:
        assert o % width == 0, name
        col[name] = o
        o += width
    w_main = _regroup_w_in(w_in_t, [p[1:] for p in pieces], 1024)

    def small_row(v, fill):
        return jnp.concatenate([v[perm]] * DT_COPIES + [jnp.full((SMALL_W - gk0,), fill, v.dtype)])[None, :]

    expand = np.zeros((DT_COPIES * n_heads, d_inner), np.float32)
    for r in range(DT_COPIES):
        for c_ in range(n_heads):
            expand[r * n_heads + c_, perm[c_] * SSD_HEAD_DIM:(perm[c_] + 1) * SSD_HEAD_DIM] = 1.0
    tril = np.tril(np.ones((CHUNK, CHUNK), np.float32))
    rb = 2 * SUBLANES
    shift = np.concatenate([np.eye(CHUNK, k=-k, dtype=np.float32)[b * rb:(b + 1) * rb]
                            for b in range(CHUNK // rb) for k in range(1, SSD_CONV_W)], axis=0)
    gk_w_pad = jnp.zeros((SMALL_W, kd), F32).at[gk0:gk0 + GLA_GATE_RANK].set(gla_gk_w)
    gk_hi = gk_w_pad.astype(BF16)
    gk_lo = (gk_w_pad - gk_hi.astype(F32)).astype(BF16)
    rows = lambda a: jnp.broadcast_to(a, a.shape[:-2] + (2 * SUBLANES, a.shape[-1]))
    prm = dict(
        n_heads=n_heads, gla_key_dim=kd, gla_val_dim=vd,
        norm_pre_gain=norm_pre_gain[None, :], w_main=w_main, w_small=w_small,
        conv_w_x=rows(half * conv_w[:, None, :d_inner]), conv_w_bc=rows(half * conv_w[:, None, d_inner:]),
        conv_b_x=rows(half * conv_b[None, :d_inner]), conv_b_bc=rows(half * conv_b[None, d_inner:]),
        dt_bias3=small_row(dt_bias, 0.0), a_log3=small_row(a_log, 0.0),
        d_skip_e=jnp.repeat(d_skip, SSD_HEAD_DIM)[None, :], ssd_norm_gain=ssd_norm_gain[None, :],
        shift=jnp.asarray(shift, BF16), expand3=jnp.asarray(expand, BF16), tril3=jnp.asarray(np.concatenate([tril] * 3, axis=1), BF16),
        gk_w3=jnp.concatenate([gk_hi, gk_lo, gk_hi], axis=0), gk_b=gla_gk_b[None, :],
        gla_norm_gain=gla_norm_gain[None, :],
        w_branch_ssd=w_branch_ssd.astype(BF16), w_branch_gla=w_branch_gla.astype(BF16),
        w_out=w_out.astype(BF16), norm_post_gain=norm_post_gain[None, :],
    )
    return prm, col


def _time_block(t):
    tb = min(t, CHUNKS_PER_STEP * CHUNK)
    assert t % tb == 0
    return tb


def _row_tile(m, want):
    tm = min(m, want)
    assert m % tm == 0
    return tm


def _layer(x, conv_state, ssd_state, gla_state, prm, col):
    b, t, d = x.shape
    assert t % CHUNK == 0 and t >= SSD_CONV_W - 1
    m = b * t
    x2 = x.reshape(m, d)
    proj, small = _inproj(x2, prm["norm_pre_gain"], prm["w_main"], prm["w_small"], _row_tile(m, 1024), 2048)
    proj3 = proj.reshape(b, t, -1)
    small3 = small.reshape(b, t, -1)
    d_inner = prm["n_heads"] * SSD_HEAD_DIM
    bcw = 2 * SSD_GROUPS * SSD_D_STATE
    nc = SSD_CONV_W - 1
    new_conv = jnp.concatenate(
        [proj3[:, t - nc:, col["xs"]:col["xs"] + d_inner], proj3[:, t - nc:, col["bc"]:col["bc"] + bcw]],
        axis=-1).astype(x.dtype)
    y_ssd, new_ssd = _ssd(proj3, small3, col, prm, conv_state, ssd_state)
    y_gla, new_gla = _gla(proj3, small3, col, prm, gla_state)
    y = _outproj(y_ssd.reshape(m, -1), y_gla.reshape(m, -1), proj, col, x2, prm,
                 _row_tile(m, 256), 1024, _row_tile(m, 512))
    return y.reshape(b, t, d), new_conv, new_ssd, new_gla


def kernel(x_prompt, x_sample, state_conv_ssd, state_ssd, state_gla, norm_pre_gain, w_in, conv_w, conv_b, dt_bias,
           a_log, d_skip, ssd_norm_gain, gla_gk_w, gla_gk_b, gla_norm_gain, w_branch_ssd, w_branch_gla, w_out,
           norm_post_gain):
    depth = w_in.shape[0]
    yp, ys = x_prompt, x_sample
    outs = [[] for _ in range(6)]
    for layer in range(depth):
        prm, col = _prepare(norm_pre_gain[layer], w_in[layer], conv_w[layer], conv_b[layer], dt_bias[layer],
                            a_log[layer], d_skip[layer], ssd_norm_gain[layer], gla_gk_w[layer], gla_gk_b[layer],
                            gla_norm_gain[layer], w_branch_ssd[layer], w_branch_gla[layer], w_out[layer],
                            norm_post_gain[layer])
        yp, c_p, s_p, g_p = _layer(yp, None, None, None, prm, col)
        ys, c_s, s_s, g_s = _layer(ys, state_conv_ssd[layer], state_ssd[layer], state_gla[layer], prm, col)
        for lst, val in zip(outs, (c_p, s_p, g_p, c_s, s_s, g_s)):
            lst.append(val)
    return (yp, ys) + tuple(jnp.stack(o) for o in outs)
```

```python
import functools
import math

import numpy as np
import jax
import jax.numpy as jnp
from jax import lax
from jax.experimental import pallas as pl
from jax.experimental.pallas import tpu as pltpu

F32 = jnp.float32
BF16 = jnp.bfloat16

LANES = 128
SUBLANES = 8
VMEM_LIMIT_BYTES = 56 * 1024 * 1024

CHUNK = 64
SSD_HEAD_DIM = 64
SSD_GROUPS = 8
SSD_D_STATE = 128
SSD_CONV_W = 4
GLA_HEADS = 4
GLA_GATE_RANK = 16
GLA_GATE_NORMALIZER = 16.0
NORM_EPS = 1e-6
SSD_NORM_EPS = 1e-5
SUB = 16

CHUNKS_PER_STEP = 8
CONV_LANES = 2 * LANES
INPROJ_TM, INPROJ_TN = 1024, 2048
MERGE_TM, MERGE_TN = 256, 1024
POST_TM = 512
REGROUP_TN = 1024
SMALL_W = 256
DT_COPIES = 2
LOG2E = math.log2(math.e)


def _dot(a, b, dims=(((1,), (0,)), ((), ()))):
    return lax.dot_general(a, b, dims, preferred_element_type=F32)


_NT = (((1,), (1,)), ((), ()))
_TN = (((0,), (0,)), ((), ()))


def _sigmoid_half(h):
    return 0.5 * jnp.tanh(h) + 0.5


def _silu_half(h):
    return h * jnp.tanh(h) + h


def _softplus(x):
    return jnp.maximum(x, 0.0) + jnp.log(1.0 + jnp.exp(-jnp.abs(x)))


def _residuals(x):
    r1 = x - x.astype(BF16).astype(F32)
    r2 = r1 - r1.astype(BF16).astype(F32)
    return x, r1, r2


def _cumsum_rows(tril3_ref, x):
    parts = [p.astype(BF16) for p in _residuals(x)]
    return _dot(tril3_ref[...], jnp.concatenate(parts, axis=0))


def _inproj_kernel(x_ref, g_ref, w_ref, ws_ref, o_ref, os_ref, h_ref):
    @pl.when(pl.program_id(1) == 0)
    def _():
        x = x_ref[...]
        ms = jnp.mean(x * x, axis=-1, keepdims=True)
        h_ref[...] = (x * lax.rsqrt(ms + NORM_EPS) * g_ref[...]).astype(BF16)
        os_ref[...] = _dot(h_ref[...], ws_ref[...])

    o_ref[...] = _dot(h_ref[...], w_ref[...]).astype(o_ref.dtype)


def _inproj(x2, gain, w_main, w_small, tm, tn):
    m, d = x2.shape
    n = w_main.shape[1]
    ns = w_small.shape[1]
    return pl.pallas_call(
        _inproj_kernel,
        grid=(m // tm, n // tn),
        in_specs=[
            pl.BlockSpec((tm, d), lambda i, j: (i, 0)),
            pl.BlockSpec((1, d), lambda i, j: (0, 0)),
            pl.BlockSpec((d, tn), lambda i, j: (0, j)),
            pl.BlockSpec((d, ns), lambda i, j: (0, 0)),
        ],
        out_specs=[
            pl.BlockSpec((tm, tn), lambda i, j: (i, j)),
            pl.BlockSpec((tm, ns), lambda i, j: (i, 0)),
        ],
        out_shape=[jax.ShapeDtypeStruct((m, n), BF16), jax.ShapeDtypeStruct((m, ns), F32)],
        scratch_shapes=[pltpu.VMEM((tm, d), BF16)],
        compiler_params=pltpu.CompilerParams(
            dimension_semantics=("arbitrary", "arbitrary"), vmem_limit_bytes=VMEM_LIMIT_BYTES),
    )(x2, gain, w_main, w_small)


def _ssd_kernel(has_state, n_heads, n_sub, *refs):
    if has_state:
        (xs_ref, z_ref, bc_ref, sm_ref, cwx_ref, cbx_ref, cwb_ref, cbb_ref, dtb_ref, alog_ref, dsk_ref,
         gain_ref, exp_ref, tril_ref, shift_ref, csx_ref, csb_ref, st_ref,
         y_ref, so_ref, tailx, tailb, srow, dtrow, xs_s, xs16, bc16, bcast, *stT) = refs
    else:
        (xs_ref, z_ref, bc_ref, sm_ref, cwx_ref, cbx_ref, cwb_ref, cbb_ref, dtb_ref, alog_ref, dsk_ref,
         gain_ref, exp_ref, tril_ref, shift_ref,
         y_ref, so_ref, tailx, tailb, srow, dtrow, xs_s, xs16, bc16, bcast, *stT) = refs
    c = pl.program_id(1)
    n_chunks = pl.num_programs(1)
    L = CHUNK
    P = SSD_HEAD_DIM
    N = SSD_D_STATE
    G = SSD_GROUPS
    gw = (n_heads // G) * P
    d_inner = n_heads * P
    nc = SSD_CONV_W - 1

    @pl.when(c == 0)
    def _():
        tailx[...] = jnp.zeros_like(tailx)
        tailb[...] = jnp.zeros_like(tailb)
        if has_state:
            tailx[SUBLANES - nc:SUBLANES, :] = csx_ref[0]
            tailb[SUBLANES - nc:SUBLANES, :] = csb_ref[0]
            for g in range(G):
                stT[g][...] = st_ref[0, g * gw:(g + 1) * gw, :].T
        else:
            for g in range(G):
                stT[g][...] = jnp.zeros_like(stT[g])

    def chunk(si, carry):
        c0 = pl.multiple_of(si * L, L)
        cw = CONV_LANES
        rb = 2 * SUBLANES
        row8 = lax.broadcasted_iota(jnp.int32, (SUBLANES, cw), 0)

        def conv(x_ref, tail, w_ref, b_ref, width, emit):
            for cb in range(width // cw):
                cs = slice(cb * cw, (cb + 1) * cw)
                sh = _dot(shift_ref[...], x_ref[0, pl.ds(c0, L), cs])
                prev = tail[:, cs]
                fix = jnp.zeros((SUBLANES, cw), F32)
                for k in range(1, SSD_CONV_W):
                    fix = fix + jnp.where(row8 < k, pltpu.roll(prev, k, 0), 0.0) * w_ref[nc - k, 0:SUBLANES, cs]
                for blk in range(L // rb):
                    acc = b_ref[:, cs] + x_ref[0, pl.ds(c0 + blk * rb, rb), cs].astype(F32) * w_ref[nc, :, cs]
                    for k in range(1, SSD_CONV_W):
                        s0 = (blk * nc + k - 1) * rb
                        acc = acc + sh[s0:s0 + rb, :] * w_ref[nc - k, :, cs]
                    if blk == 0:
                        acc = jnp.concatenate([acc[0:SUBLANES, :] + fix, acc[SUBLANES:, :]], axis=0)
                    emit(blk * rb, rb, cs, _silu_half(acc))

        def emit_x(r0, nr, cs, act):
            xs_s[r0:r0 + nr, cs] = act
            xs16[r0:r0 + nr, cs] = act.astype(BF16)

        def emit_bc(r0, nr, cs, act):
            bc16[r0:r0 + nr, cs] = act.astype(BF16)

        conv(xs_ref, tailx, cwx_ref, cbx_ref, d_inner, emit_x)
        conv(bc_ref, tailb, cwb_ref, cbb_ref, 2 * G * N, emit_bc)

        dt3 = _softplus(sm_ref[0, pl.ds(c0, L), :] + dtb_ref[...])
        s3 = _cumsum_rows(tril_ref, dt3 * (-jnp.exp(alog_ref[...])))
        w3 = dt3 * jnp.exp(s3[L - 1:L, :] - s3)
        lane_s = lax.broadcasted_iota(jnp.int32, (2 * L, SMALL_W), 1)
        x0, x1, _ = _residuals(jnp.concatenate([s3, w3], axis=0))
        split = jnp.where(lane_s < n_heads, x0, x1)[:, 0:DT_COPIES * n_heads].astype(BF16)
        bcast[...] = _dot(split, exp_ref[...])
        hh = n_heads // 2
        s_t = s3[:, 0:2 * L].T
        dt_t = dt3[:, 0:2 * L].T
        srow[...] = jnp.concatenate([s_t[0:hh], s_t[hh:2 * hh]], axis=1)
        dtrow[...] = jnp.concatenate([dt_t[0:hh], dt_t[hh:2 * hh]], axis=1)

        row = lax.broadcasted_iota(jnp.int32, (L, 2 * P), 0)
        lane = lax.broadcasted_iota(jnp.int32, (L, 2 * P), 1)
        causal2 = (lane % P) <= row
        left = lane < P
        ppg = gw // (2 * P)

        for g in range(G):
            gs = slice(g * gw, (g + 1) * gw)
            bm16 = bc16[:, g * N:(g + 1) * N]
            cm16 = bc16[:, (G + g) * N:(G + g + 1) * N]
            cb2 = _dot(cm16, jnp.concatenate([bm16, bm16], axis=0), _NT)
            st_g = stT[g][...]
            y_state = _dot(cm16, st_g.astype(BF16))
            ys = []
            for pp in range(ppg):
                p = g * ppg + pp
                sl = slice(p * 2 * P, (p + 1) * 2 * P)
                se = bcast[0:L, sl]
                dec = jnp.exp(jnp.where(causal2, se - srow[p:p + 1, :], -jnp.inf))
                m = (cb2 * dec * dtrow[p:p + 1, :]).astype(BF16)
                xp16 = xs16[:, sl]
                zero = jnp.zeros_like(xp16)
                rhs = jnp.concatenate([jnp.where(left, xp16, zero), jnp.where(left, zero, xp16)], axis=0)
                y = _dot(m, rhs)
                y = y + y_state[:, pp * 2 * P:(pp + 1) * 2 * P] * jnp.exp(se) + dsk_ref[:, sl] * xs_s[:, sl]
                ys.append(y)
            s_last = bcast[L - 1:L, gs]
            xdt_end = xs_s[:, gs] * bcast[L:2 * L, gs]
            upd = _dot(bm16, xdt_end.astype(BF16), _TN)
            stT[g][...] = st_g * jnp.exp(s_last) + upd

            yg = jnp.concatenate(ys, axis=1) * _silu_half(z_ref[0, pl.ds(c0, L), gs].astype(F32))
            ms = jnp.mean(yg * yg, axis=-1, keepdims=True)
            y_ref[0, pl.ds(c0, L), gs] = (yg * lax.rsqrt(ms + SSD_NORM_EPS) * gain_ref[:, gs]).astype(y_ref.dtype)

        tailx[...] = xs_ref[0, pl.ds(c0 + L - 2 * SUBLANES, 2 * SUBLANES), :].astype(F32)[SUBLANES:, :]
        tailb[...] = bc_ref[0, pl.ds(c0 + L - 2 * SUBLANES, 2 * SUBLANES), :].astype(F32)[SUBLANES:, :]

        return carry

    lax.fori_loop(0, n_sub, chunk, 0)

    @pl.when(c == n_chunks - 1)
    def _():
        for g in range(G):
            so_ref[0, g * gw:(g + 1) * gw, :] = stT[g][...].T


def _ssd(proj3, small3, col, prm, conv_state, ssd_state):
    b, t, _ = proj3.shape
    n_heads = prm["n_heads"]
    d_inner = n_heads * SSD_HEAD_DIM
    bcw = 2 * SSD_GROUPS * SSD_D_STATE
    has_state = ssd_state is not None
    L = CHUNK
    tb = _time_block(t)
    const = lambda *shape: pl.BlockSpec(shape, lambda i, j: (0,) * len(shape))
    in_specs = [
        pl.BlockSpec((1, tb, d_inner), lambda i, j: (i, j, col["xs"] // d_inner)),
        pl.BlockSpec((1, tb, d_inner), lambda i, j: (i, j, col["z"] // d_inner)),
        pl.BlockSpec((1, tb, bcw), lambda i, j: (i, j, col["bc"] // bcw)),
        pl.BlockSpec((1, tb, SMALL_W), lambda i, j: (i, j, 0)),
        const(SSD_CONV_W, 2 * SUBLANES, d_inner), const(2 * SUBLANES, d_inner),
        const(SSD_CONV_W, 2 * SUBLANES, bcw), const(2 * SUBLANES, bcw),
        const(1, SMALL_W), const(1, SMALL_W), const(1, d_inner), const(1, d_inner),
        const(DT_COPIES * n_heads, d_inner), const(L, 3 * L), const((SSD_CONV_W - 1) * L, L),
    ]
    args = [proj3, proj3, proj3, small3, prm["conv_w_x"], prm["conv_b_x"], prm["conv_w_bc"], prm["conv_b_bc"],
            prm["dt_bias3"], prm["a_log3"], prm["d_skip_e"], prm["ssd_norm_gain"], prm["expand3"], prm["tril3"],
            prm["shift"]]
    if has_state:
        nc = SSD_CONV_W - 1
        in_specs += [
            pl.BlockSpec((1, nc, d_inner), lambda i, j: (i, 0, 0)),
            pl.BlockSpec((1, nc, bcw), lambda i, j: (i, 0, 0)),
            pl.BlockSpec((1, d_inner, SSD_D_STATE), lambda i, j: (i, 0, 0)),
        ]
        args += [conv_state[:, :, :d_inner], conv_state[:, :, d_inner:],
                 ssd_state.reshape(b, d_inner, SSD_D_STATE)]
    y, st = pl.pallas_call(
        functools.partial(_ssd_kernel, has_state, n_heads, tb // L),
        grid=(b, t // tb),
        in_specs=in_specs,
        out_specs=[
            pl.BlockSpec((1, tb, d_inner), lambda i, j: (i, j, 0)),
            pl.BlockSpec((1, d_inner, SSD_D_STATE), lambda i, j: (i, 0, 0)),
        ],
        out_shape=[
            jax.ShapeDtypeStruct((b, t, d_inner), BF16),
            jax.ShapeDtypeStruct((b, d_inner, SSD_D_STATE), F32),
        ],
        scratch_shapes=[
            pltpu.VMEM((SUBLANES, d_inner), F32),
            pltpu.VMEM((SUBLANES, bcw), F32),
            pltpu.VMEM((n_heads // 2, 2 * L), F32),
            pltpu.VMEM((n_heads // 2, 2 * L), F32),
            pltpu.VMEM((L, d_inner), F32),
            pltpu.VMEM((L, d_inner), BF16),
            pltpu.VMEM((L, bcw), BF16),
            pltpu.VMEM((2 * L, d_inner), F32),
        ] + [pltpu.VMEM((SSD_D_STATE, d_inner // SSD_GROUPS), F32) for _ in range(SSD_GROUPS)],
        compiler_params=pltpu.CompilerParams(
            dimension_semantics=("arbitrary", "arbitrary"), vmem_limit_bytes=VMEM_LIMIT_BYTES),
    )(*args)
    return y, st.reshape(b, n_heads, SSD_HEAD_DIM, SSD_D_STATE)


def _gla_kernel(has_state, n_sub, *refs):
    if has_state:
        (q_ref, k_ref, v_ref, gb_ref, sm_ref, wgk_ref, bgk_ref, gain_ref, tril_ref, st_ref,
         o_ref, so_ref, bcs, ks, qs, *stT) = refs
    else:
        (q_ref, k_ref, v_ref, gb_ref, sm_ref, wgk_ref, bgk_ref, gain_ref, tril_ref,
         o_ref, so_ref, bcs, ks, qs, *stT) = refs
    c = pl.program_id(1)
    n_chunks = pl.num_programs(1)
    L = CHUNK
    H = GLA_HEADS
    kd = q_ref.shape[2]
    vd = v_ref.shape[2]
    hk = kd // H
    hv = vd // H
    nb = L // SUB

    @pl.when(c == 0)
    def _():
        if has_state:
            for h in range(H):
                stT[h][...] = st_ref[0, h].T
        else:
            for h in range(H):
                stT[h][...] = jnp.zeros_like(stT[h])

    def chunk(si, carry):
        c0 = pl.multiple_of(si * L, L)
        sm = sm_ref[0, pl.ds(c0, L), :]
        sm_hi = sm.astype(BF16)
        sm_lo = (sm - sm_hi.astype(F32)).astype(BF16)
        x = _dot(jnp.concatenate([sm_hi, sm_hi, sm_lo], axis=1), wgk_ref[...]) + bgk_ref[...]
        glog2 = (jnp.minimum(x, 0.0) - jnp.log(1.0 + jnp.exp(-jnp.abs(x)))) * (LOG2E / GLA_GATE_NORMALIZER)
        bc = _cumsum_rows(tril_ref, glog2)
        bcs[...] = bc
        bc_last = bc[L - 1:L, :]
        q = q_ref[0, pl.ds(c0, L), :].astype(F32) * (hk ** -0.5)
        k = k_ref[0, pl.ds(c0, L), :].astype(F32)
        ks[...] = k
        qs[...] = q
        qe = (q * jnp.exp2(bc)).astype(BF16)
        k_end = (k * jnp.exp2(bc_last - bc)).astype(BF16)

        nt = SUB // SUBLANES
        lane = lax.broadcasted_iota(jnp.int32, (SUBLANES, L), 1)
        row = lax.broadcasted_iota(jnp.int32, (SUBLANES, L), 0)
        a_rows = [[None] * nb for _ in range(H)]
        for blk in range(nb):
            r0 = blk * SUB
            off = [None] * H
            if blk > 0:
                ref_row = bcs[r0 - 1:r0, :]
                qd = (q[r0:r0 + SUB, :] * jnp.exp2(bc[r0:r0 + SUB, :] - ref_row)).astype(BF16)
                kt = k[0:r0, :] * jnp.exp2(ref_row - bc[0:r0, :])
                kt = jnp.concatenate([kt, jnp.zeros((L - r0, kd), F32)], axis=0).astype(BF16)
                for h in range(H):
                    off[h] = _dot(qd[:, h * hk:(h + 1) * hk], kt[:, h * hk:(h + 1) * hk], _NT)
            diag = [[jnp.zeros((SUBLANES, L), F32) for _ in range(nt)] for _ in range(H)]
            for jj in range(SUB):
                j = r0 + jj
                for ti in range(jj // SUBLANES, nt):
                    t0 = r0 + ti * SUBLANES
                    keep = (lane == j) & (row + t0 >= j)
                    for h in range(H):
                        tot = None
                        for lt in range(hk // LANES):
                            ls = slice(h * hk + lt * LANES, h * hk + (lt + 1) * LANES)
                            e = jnp.exp2(bcs[t0:t0 + SUBLANES, ls] - bcs[j:j + 1, ls])
                            pr = qs[t0:t0 + SUBLANES, ls] * e * ks[j:j + 1, ls]
                            tot = pr if tot is None else tot + pr
                        col = jnp.sum(tot, axis=-1, keepdims=True)
                        diag[h][ti] = jnp.where(keep, col, diag[h][ti])
            for h in range(H):
                d_h = jnp.concatenate(diag[h], axis=0)
                a_rows[h][blk] = d_h if off[h] is None else off[h] + d_h

        eb_last = jnp.exp2(bc_last)
        for h in range(H):
            a_h = jnp.concatenate(a_rows[h], axis=0).astype(BF16)
            v_h = v_ref[0, pl.ds(c0, L), h * hv:(h + 1) * hv]
            st_h = stT[h][...]
            o = _dot(a_h, v_h) + _dot(qe[:, h * hk:(h + 1) * hk], st_h.astype(BF16), _NT)
            stT[h][...] = st_h * eb_last[:, h * hk:(h + 1) * hk] + _dot(v_h, k_end[:, h * hk:(h + 1) * hk], _TN)
            ms = jnp.mean(o * o, axis=-1, keepdims=True)
            on = o * lax.rsqrt(ms + NORM_EPS) * gain_ref[...]
            gate = _silu_half(gb_ref[0, pl.ds(c0, L), h * hv:(h + 1) * hv].astype(F32))
            o_ref[0, pl.ds(c0, L), h * hv:(h + 1) * hv] = (on * gate).astype(o_ref.dtype)

        return carry

    lax.fori_loop(0, n_sub, chunk, 0)

    @pl.when(c == n_chunks - 1)
    def _():
        for h in range(H):
            so_ref[0, h] = stT[h][...].T


def _gla(proj3, small3, col, prm, gla_state):
    b, t, _ = proj3.shape
    kd = prm["gla_key_dim"]
    vd = prm["gla_val_dim"]
    H = GLA_HEADS
    hk, hv = kd // H, vd // H
    has_state = gla_state is not None
    L = CHUNK
    tb = _time_block(t)
    const = lambda *shape: pl.BlockSpec(shape, lambda i, j: (0,) * len(shape))
    in_specs = [
        pl.BlockSpec((1, tb, kd), lambda i, j: (i, j, col["q"] // kd)),
        pl.BlockSpec((1, tb, kd), lambda i, j: (i, j, col["k"] // kd)),
        pl.BlockSpec((1, tb, vd), lambda i, j: (i, j, col["v"] // vd)),
        pl.BlockSpec((1, tb, vd), lambda i, j: (i, j, col["g_b"] // vd)),
        pl.BlockSpec((1, tb, SMALL_W), lambda i, j: (i, j, 0)),
        const(3 * SMALL_W, kd), const(1, kd), const(1, hv), const(L, 3 * L),
    ]
    args = [proj3, proj3, proj3, proj3, small3, prm["gk_w3"], prm["gk_b"], prm["gla_norm_gain"], prm["tril3"]]
    if has_state:
        in_specs.append(pl.BlockSpec((1, H, hk, hv), lambda i, j: (i, 0, 0, 0)))
        args.append(gla_state)
    return pl.pallas_call(
        functools.partial(_gla_kernel, has_state, tb // L),
        grid=(b, t // tb),
        in_specs=in_specs,
        out_specs=[
            pl.BlockSpec((1, tb, vd), lambda i, j: (i, j, 0)),
            pl.BlockSpec((1, H, hk, hv), lambda i, j: (i, 0, 0, 0)),
        ],
        out_shape=[
            jax.ShapeDtypeStruct((b, t, vd), BF16),
            jax.ShapeDtypeStruct((b, H, hk, hv), F32),
        ],
        scratch_shapes=[
            pltpu.VMEM((L, kd), F32),
            pltpu.VMEM((L, kd), F32),
            pltpu.VMEM((L, kd), F32),
        ] + [pltpu.VMEM((hv, hk), F32) for _ in range(H)],
        compiler_params=pltpu.CompilerParams(
            dimension_semantics=("arbitrary", "arbitrary"), vmem_limit_bytes=VMEM_LIMIT_BYTES),
    )(*args)


def _merge_kernel(tn, ys_ref, yg_ref, g_ref, wbs_ref, wbg_ref, o_ref):
    d = o_ref.shape[1]
    for n in range(d // tn):
        ns = slice(n * tn, (n + 1) * tn)
        a = _dot(ys_ref[...], wbs_ref[:, ns])
        b = _dot(yg_ref[...], wbg_ref[:, ns])
        gate_s = _sigmoid_half(g_ref[:, ns].astype(F32))
        gate_g = _sigmoid_half(g_ref[:, d + n * tn:d + (n + 1) * tn].astype(F32))
        o_ref[:, ns] = (gate_s * a + gate_g * b).astype(o_ref.dtype)


def _post_kernel(mx_ref, x_ref, wo_ref, g_ref, o_ref):
    cc = _dot(mx_ref[...], wo_ref[...])
    ms = jnp.mean(cc * cc, axis=-1, keepdims=True)
    o_ref[...] = x_ref[...] + cc * lax.rsqrt(ms + NORM_EPS) * g_ref[...]


def _outproj(y_ssd, y_gla, proj, col, x2, prm, tm_merge, tn, tm_post):
    m, d = x2.shape
    d_inner = y_ssd.shape[1]
    vd = y_gla.shape[1]
    g0 = col["merge"] // (2 * d)
    resident = lambda r, c_: pl.BlockSpec((r, c_), lambda i: (0, 0), pipeline_mode=pl.Buffered(1))
    mixed = pl.pallas_call(
        functools.partial(_merge_kernel, tn),
        grid=(m // tm_merge,),
        in_specs=[
            pl.BlockSpec((tm_merge, d_inner), lambda i: (i, 0)),
            pl.BlockSpec((tm_merge, vd), lambda i: (i, 0)),
            pl.BlockSpec((tm_merge, 2 * d), lambda i: (i, g0)),
            resident(d_inner, d), resident(vd, d),
        ],
        out_specs=pl.BlockSpec((tm_merge, d), lambda i: (i, 0)),
        out_shape=jax.ShapeDtypeStruct((m, d), BF16),
        compiler_params=pltpu.CompilerParams(
            dimension_semantics=("arbitrary",), vmem_limit_bytes=VMEM_LIMIT_BYTES),
    )(y_ssd, y_gla, proj, prm["w_branch_ssd"], prm["w_branch_gla"])
    return pl.pallas_call(
        _post_kernel,
        grid=(m // tm_post,),
        in_specs=[
            pl.BlockSpec((tm_post, d), lambda i: (i, 0)),
            pl.BlockSpec((tm_post, d), lambda i: (i, 0)),
            pl.BlockSpec((d, d), lambda i: (0, 0)),
            pl.BlockSpec((1, d), lambda i: (0, 0)),
        ],
        out_specs=pl.BlockSpec((tm_post, d), lambda i: (i, 0)),
        out_shape=jax.ShapeDtypeStruct((m, d), F32),
        compiler_params=pltpu.CompilerParams(
            dimension_semantics=("arbitrary",), vmem_limit_bytes=VMEM_LIMIT_BYTES),
    )(mixed, x2, prm["w_out"], prm["norm_post_gain"])


def _regroup_kernel(tn, offsets, blk_ref, cls_ref, half_ref, a_ref, b_ref, o_ref):
    j = pl.program_id(0)
    scale = jnp.where(half_ref[j] == 1, 0.5, 1.0).astype(F32)
    for ci, off in enumerate(offsets):
        @pl.when(cls_ref[j] == ci)
        def _():
            x = a_ref[...] if off == 0 else jnp.concatenate([a_ref[off:tn, :], b_ref[0:off, :]], axis=0)
            o_ref[...] = (x.T * scale).astype(o_ref.dtype)


def _pick_rows_kernel(spans, *refs):
    for (start, size), src, dst in zip(spans, refs[:len(spans)], refs[len(spans):]):
        dst[...] = src[start % LANES:start % LANES + size, :]


def _pick_rows(a, *spans):
    d = a.shape[1]
    assert all(start % LANES + size <= LANES and size % SUBLANES == 0 for start, size in spans)
    return pl.pallas_call(
        functools.partial(_pick_rows_kernel, spans),
        grid=(1,),
        in_specs=[pl.BlockSpec((LANES, d), functools.partial(lambda blk, i: (blk, 0), start // LANES))
                  for start, _ in spans],
        out_specs=[pl.BlockSpec((size, d), lambda i: (0, 0)) for _, size in spans],
        out_shape=[jax.ShapeDtypeStruct((size, d), a.dtype) for _, size in spans],
    )(*([a] * len(spans)))


def _regroup_w_in(w_in_t, segments, tn):
    d = w_in_t.shape[1]
    offsets = sorted({src % tn for src, _, _ in segments})
    blk, cls, half = [], [], []
    for src, width, halve in segments:
        assert width % tn == 0 and src % tn < LANES and (src % tn) % SUBLANES == 0
        for i in range(width // tn):
            blk.append(src // tn + i)
            cls.append(offsets.index(src % tn))
            half.append(int(halve))
    n_blocks = len(blk)
    to_i32 = lambda v: jnp.asarray(np.array(v, np.int32))
    return pl.pallas_call(
        functools.partial(_regroup_kernel, tn, tuple(offsets)),
        grid_spec=pltpu.PrefetchScalarGridSpec(
            num_scalar_prefetch=3,
            grid=(n_blocks,),
            in_specs=[
                pl.BlockSpec((tn, d), lambda j, blk, cls, half: (blk[j], 0)),
                pl.BlockSpec((LANES, d), lambda j, blk, cls, half: ((blk[j] + 1) * (tn // LANES), 0)),
            ],
            out_specs=pl.BlockSpec((d, tn), lambda j, blk, cls, half: (0, j)),
        ),
        out_shape=jax.ShapeDtypeStruct((d, n_blocks * tn), BF16),
        compiler_params=pltpu.CompilerParams(
            dimension_semantics=("arbitrary",), vmem_limit_bytes=VMEM_LIMIT_BYTES),
    )(to_i32(blk), to_i32(cls), to_i32(half), w_in_t, w_in_t)
def _prepare(norm_pre_gain, w_in, conv_w, conv_b, dt_bias, a_log, d_skip, ssd_norm_gain, gla_gk_w, gla_gk_b,
             gla_norm_gain, w_branch_ssd, w_branch_gla, w_out, norm_post_gain):
    d = w_in.shape[0]
    n_heads = dt_bias.shape[0]
    d_inner = n_heads * SSD_HEAD_DIM
    bcw = 2 * SSD_GROUPS * SSD_D_STATE
    kd = gla_gk_w.shape[1]
    vd = w_branch_gla.shape[0]
    assert DT_COPIES * n_heads + GLA_GATE_RANK <= SMALL_W
    sizes = (d_inner, d_inner + bcw, n_heads, kd, kd, vd, vd, GLA_GATE_RANK, 2 * d)
    offs = np.concatenate([[0], np.cumsum(sizes)])
    z0, xbc0, dt0, q0, k0, v0, gb0, gkl0, mg0 = [int(o_) for o_ in offs[:9]]
    perm = np.concatenate([np.arange(0, n_heads, 2), np.arange(1, n_heads, 2)])
    w_in_t = w_in.T
    dt_rows, gk_rows = _pick_rows(w_in_t, (dt0, n_heads), (gkl0, GLA_GATE_RANK))
    dt_w = dt_rows[perm]
    gk0 = DT_COPIES * n_heads
    pad = SMALL_W - gk0 - GLA_GATE_RANK
    w_small = jnp.concatenate(
        [dt_w] * DT_COPIES + [gk_rows, jnp.zeros((pad, d), w_in.dtype)], axis=0).T.astype(BF16)
    half = 0.5
    pieces = [("xs", xbc0, d_inner, False), ("z", z0, d_inner, True), ("merge", mg0, 2 * d, True),
              ("bc", xbc0 + d_inner, bcw, False), ("v", v0, vd, False), ("g_b", gb0, vd, True),
              ("q", q0, kd, False), ("k", k0, kd, False)]
    col, o = {}, 0
    for name, _, width, _ in pieces:
        assert o % width == 0, name
        col[name] = o
        o += width
    w_main = _regroup_w_in(w_in_t, [p[1:] for p in pieces], REGROUP_TN)

    def small_row(v, fill):
        return jnp.concatenate([v[perm]] * DT_COPIES + [jnp.full((SMALL_W - gk0,), fill, v.dtype)])[None, :]

    expand = np.zeros((DT_COPIES * n_heads, d_inner), np.float32)
    for r in range(DT_COPIES):
        for c_ in range(n_heads):
            expand[r * n_heads + c_, perm[c_] * SSD_HEAD_DIM:(perm[c_] + 1) * SSD_HEAD_DIM] = 1.0
    tril = np.tril(np.ones((CHUNK, CHUNK), np.float32))
    rb = 2 * SUBLANES
    shift = np.concatenate([np.eye(CHUNK, k=-k, dtype=np.float32)[b * rb:(b + 1) * rb]
                            for b in range(CHUNK // rb) for k in range(1, SSD_CONV_W)], axis=0)
    gk_w_pad = jnp.zeros((SMALL_W, kd), F32).at[gk0:gk0 + GLA_GATE_RANK].set(gla_gk_w)
    gk_hi = gk_w_pad.astype(BF16)
    gk_lo = (gk_w_pad - gk_hi.astype(F32)).astype(BF16)
    rows = lambda a: jnp.broadcast_to(a, a.shape[:-2] + (2 * SUBLANES, a.shape[-1]))
    prm = dict(
        n_heads=n_heads, gla_key_dim=kd, gla_val_dim=vd,
        norm_pre_gain=norm_pre_gain[None, :], w_main=w_main, w_small=w_small,
        conv_w_x=rows(half * conv_w[:, None, :d_inner]), conv_w_bc=rows(half * conv_w[:, None, d_inner:]),
        conv_b_x=rows(half * conv_b[None, :d_inner]), conv_b_bc=rows(half * conv_b[None, d_inner:]),
        dt_bias3=small_row(dt_bias, 0.0), a_log3=small_row(a_log, 0.0),
        d_skip_e=jnp.repeat(d_skip, SSD_HEAD_DIM)[None, :], ssd_norm_gain=ssd_norm_gain[None, :],
        shift=jnp.asarray(shift, BF16), expand3=jnp.asarray(expand, BF16), tril3=jnp.asarray(np.concatenate([tril] * 3, axis=1), BF16),
        gk_w3=jnp.concatenate([gk_hi, gk_lo, gk_hi], axis=0), gk_b=gla_gk_b[None, :],
        gla_norm_gain=gla_norm_gain[None, :],
        w_branch_ssd=w_branch_ssd.astype(BF16), w_branch_gla=w_branch_gla.astype(BF16),
        w_out=w_out.astype(BF16), norm_post_gain=norm_post_gain[None, :],
    )
    return prm, col


def _time_block(t):
    tb = min(t, CHUNKS_PER_STEP * CHUNK)
    assert t % tb == 0
    return tb


def _row_tile(m, want):
    tm = min(m, want)
    assert m % tm == 0
    return tm


def _layer(x, conv_state, ssd_state, gla_state, prm, col):
    b, t, d = x.shape
    assert t % CHUNK == 0 and t >= SSD_CONV_W - 1
    m = b * t
    x2 = x.reshape(m, d)
    proj, small = _inproj(x2, prm["norm_pre_gain"], prm["w_main"], prm["w_small"], _row_tile(m, INPROJ_TM), INPROJ_TN)
    proj3 = proj.reshape(b, t, -1)
    small3 = small.reshape(b, t, -1)
    d_inner = prm["n_heads"] * SSD_HEAD_DIM
    bcw = 2 * SSD_GROUPS * SSD_D_STATE
    nc = SSD_CONV_W - 1
    new_conv = jnp.concatenate(
        [proj3[:, t - nc:, col["xs"]:col["xs"] + d_inner], proj3[:, t - nc:, col["bc"]:col["bc"] + bcw]],
        axis=-1).astype(x.dtype)
    y_ssd, new_ssd = _ssd(proj3, small3, col, prm, conv_state, ssd_state)
    y_gla, new_gla = _gla(proj3, small3, col, prm, gla_state)
    y = _outproj(y_ssd.reshape(m, -1), y_gla.reshape(m, -1), proj, col, x2, prm,
                 _row_tile(m, MERGE_TM), MERGE_TN, _row_tile(m, POST_TM))
    return y.reshape(b, t, d), new_conv, new_ssd, new_gla


def kernel(x_prompt, x_sample, state_conv_ssd, state_ssd, state_gla, norm_pre_gain, w_in, conv_w, conv_b, dt_bias,
           a_log, d_skip, ssd_norm_gain, gla_gk_w, gla_gk_b, gla_norm_gain, w_branch_ssd, w_branch_gla, w_out,
           norm_post_gain):
    depth = w_in.shape[0]
    yp, ys = x_prompt, x_sample
    outs = [[] for _ in range(6)]
    for layer in range(depth):
        prm, col = _prepare(norm_pre_gain[layer], w_in[layer], conv_w[layer], conv_b[layer], dt_bias[layer],
                            a_log[layer], d_skip[layer], ssd_norm_gain[layer], gla_gk_w[layer], gla_gk_b[layer],
                            gla_norm_gain[layer], w_branch_ssd[layer], w_branch_gla[layer], w_out[layer],
                            norm_post_gain[layer])
        yp, c_p, s_p, g_p = _layer(yp, None, None, None, prm, col)
        ys, c_s, s_s, g_s = _layer(ys, state_conv_ssd[layer], state_ssd[layer], state_gla[layer], prm, col)
        for lst, val in zip(outs, (c_p, s_p, g_p, c_s, s_s, g_s)):
            lst.append(val)
    return (yp, ys) + tuple(jnp.stack(o) for o in outs)
```

```python
import functools
import math

import numpy as np
import jax
import jax.numpy as jnp
from jax import lax
from jax.experimental import pallas as pl
from jax.experimental.pallas import tpu as pltpu

F32 = jnp.float32
BF16 = jnp.bfloat16

LANES = 128
SUBLANES = 8
VMEM_LIMIT_BYTES = 56 * 1024 * 1024

CHUNK = 64
SSD_HEAD_DIM = 64
SSD_GROUPS = 8
SSD_D_STATE = 128
SSD_CONV_W = 4
GLA_HEADS = 4
GLA_GATE_RANK = 16
GLA_GATE_NORMALIZER = 16.0
NORM_EPS = 1e-6
SSD_NORM_EPS = 1e-5
SUB = 8

CHUNKS_PER_STEP = 8
CONV_LANES = 2 * LANES
INPROJ_TM, INPROJ_TN = 1024, 2048
MERGE_TM, MERGE_TN = 256, 1024
POST_TM = 512
REGROUP_TN = 1024
SMALL_W = 256
DT_COPIES = 2
LOG2E = math.log2(math.e)


def _dot(a, b, dims=(((1,), (0,)), ((), ()))):
    return lax.dot_general(a, b, dims, preferred_element_type=F32)


_NT = (((1,), (1,)), ((), ()))
_TN = (((0,), (0,)), ((), ()))


def _sigmoid_half(h):
    return 0.5 * jnp.tanh(h) + 0.5


def _silu_half(h):
    return h * jnp.tanh(h) + h


def _softplus(x):
    return jnp.maximum(x, 0.0) + jnp.log(1.0 + jnp.exp(-jnp.abs(x)))


def _residuals(x):
    r1 = x - x.astype(BF16).astype(F32)
    r2 = r1 - r1.astype(BF16).astype(F32)
    return x, r1, r2


def _cumsum_rows(tril3_ref, x):
    parts = [p.astype(BF16) for p in _residuals(x)]
    return _dot(tril3_ref[...], jnp.concatenate(parts, axis=0))


def _inproj_kernel(x_ref, g_ref, w_ref, ws_ref, o_ref, os_ref, h_ref):
    @pl.when(pl.program_id(1) == 0)
    def _():
        x = x_ref[...]
        ms = jnp.mean(x * x, axis=-1, keepdims=True)
        h_ref[...] = (x * lax.rsqrt(ms + NORM_EPS) * g_ref[...]).astype(BF16)
        os_ref[...] = _dot(h_ref[...], ws_ref[...])

    o_ref[...] = _dot(h_ref[...], w_ref[...]).astype(o_ref.dtype)


def _inproj(x2, gain, w_main, w_small, tm, tn):
    m, d = x2.shape
    n = w_main.shape[1]
    ns = w_small.shape[1]
    return pl.pallas_call(
        _inproj_kernel,
        grid=(m // tm, n // tn),
        in_specs=[
            pl.BlockSpec((tm, d), lambda i, j: (i, 0)),
            pl.BlockSpec((1, d), lambda i, j: (0, 0)),
            pl.BlockSpec((d, tn), lambda i, j: (0, j)),
            pl.BlockSpec((d, ns), lambda i, j: (0, 0)),
        ],
        out_specs=[
            pl.BlockSpec((tm, tn), lambda i, j: (i, j)),
            pl.BlockSpec((tm, ns), lambda i, j: (i, 0)),
        ],
        out_shape=[jax.ShapeDtypeStruct((m, n), BF16), jax.ShapeDtypeStruct((m, ns), F32)],
        scratch_shapes=[pltpu.VMEM((tm, d), BF16)],
        compiler_params=pltpu.CompilerParams(
            dimension_semantics=("arbitrary", "arbitrary"), vmem_limit_bytes=VMEM_LIMIT_BYTES),
    )(x2, gain, w_main, w_small)


def _ssd_kernel(has_state, n_heads, n_sub, *refs):
    if has_state:
        (xs_ref, z_ref, bc_ref, sm_ref, cwx_ref, cbx_ref, cwb_ref, cbb_ref, dtb_ref, alog_ref, dsk_ref,
         gain_ref, exp_ref, tril_ref, shift_ref, csx_ref, csb_ref, st_ref,
         y_ref, so_ref, tailx, tailb, srow, dtrow, xs_s, xs16, bc16, bcast, yz_s, *stT) = refs
    else:
        (xs_ref, z_ref, bc_ref, sm_ref, cwx_ref, cbx_ref, cwb_ref, cbb_ref, dtb_ref, alog_ref, dsk_ref,
         gain_ref, exp_ref, tril_ref, shift_ref,
         y_ref, so_ref, tailx, tailb, srow, dtrow, xs_s, xs16, bc16, bcast, yz_s, *stT) = refs
    c = pl.program_id(1)
    n_chunks = pl.num_programs(1)
    L = CHUNK
    P = SSD_HEAD_DIM
    N = SSD_D_STATE
    G = SSD_GROUPS
    gw = (n_heads // G) * P
    d_inner = n_heads * P
    nc = SSD_CONV_W - 1

    @pl.when(c == 0)
    def _():
        tailx[...] = jnp.zeros_like(tailx)
        tailb[...] = jnp.zeros_like(tailb)
        if has_state:
            tailx[SUBLANES - nc:SUBLANES, :] = csx_ref[0]
            tailb[SUBLANES - nc:SUBLANES, :] = csb_ref[0]
            for g in range(G):
                stT[g][...] = st_ref[0, g * gw:(g + 1) * gw, :].T
        else:
            for g in range(G):
                stT[g][...] = jnp.zeros_like(stT[g])

    def chunk(si, carry):
        c0 = pl.multiple_of(si * L, L)
        cw = CONV_LANES
        rb = 2 * SUBLANES
        row8 = lax.broadcasted_iota(jnp.int32, (SUBLANES, cw), 0)

        def conv(x_ref, tail, w_ref, b_ref, width, emit):
            for cb in range(width // cw):
                cs = slice(cb * cw, (cb + 1) * cw)
                sh = _dot(shift_ref[...], x_ref[0, pl.ds(c0, L), cs])
                prev = tail[:, cs]
                fix = jnp.zeros((SUBLANES, cw), F32)
                for k in range(1, SSD_CONV_W):
                    fix = fix + jnp.where(row8 < k, pltpu.roll(prev, k, 0), 0.0) * w_ref[nc - k, 0:SUBLANES, cs]
                for blk in range(L // rb):
                    acc = b_ref[:, cs] + x_ref[0, pl.ds(c0 + blk * rb, rb), cs].astype(F32) * w_ref[nc, :, cs]
                    for k in range(1, SSD_CONV_W):
                        s0 = (blk * nc + k - 1) * rb
                        acc = acc + sh[s0:s0 + rb, :] * w_ref[nc - k, :, cs]
                    if blk == 0:
                        acc = jnp.concatenate([acc[0:SUBLANES, :] + fix, acc[SUBLANES:, :]], axis=0)
                    emit(blk * rb, rb, cs, _silu_half(acc))

        def emit_x(r0, nr, cs, act):
            xs_s[r0:r0 + nr, cs] = act
            xs16[r0:r0 + nr, cs] = act.astype(BF16)

        def emit_bc(r0, nr, cs, act):
            bc16[r0:r0 + nr, cs] = act.astype(BF16)

        conv(xs_ref, tailx, cwx_ref, cbx_ref, d_inner, emit_x)
        conv(bc_ref, tailb, cwb_ref, cbb_ref, 2 * G * N, emit_bc)

        dt3 = _softplus(sm_ref[0, pl.ds(c0, L), :] + dtb_ref[...])
        s3 = _cumsum_rows(tril_ref, dt3 * (-jnp.exp(alog_ref[...])))
        w3 = dt3 * jnp.exp(s3[L - 1:L, :] - s3)
        lane_s = lax.broadcasted_iota(jnp.int32, (2 * L, SMALL_W), 1)
        x0, x1, _ = _residuals(jnp.concatenate([s3, w3], axis=0))
        split = jnp.where(lane_s < n_heads, x0, x1)[:, 0:DT_COPIES * n_heads].astype(BF16)
        bcast[...] = _dot(split, exp_ref[...])
        hh = n_heads // 2
        s_t = s3[:, 0:2 * L].T
        dt_t = dt3[:, 0:2 * L].T
        srow[...] = jnp.concatenate([s_t[0:hh], s_t[hh:2 * hh]], axis=1)
        dtrow[...] = jnp.concatenate([dt_t[0:hh], dt_t[hh:2 * hh]], axis=1)

        row = lax.broadcasted_iota(jnp.int32, (L, 2 * P), 0)
        lane = lax.broadcasted_iota(jnp.int32, (L, 2 * P), 1)
        causal2 = (lane % P) <= row
        left = lane < P
        ppg = gw // (2 * P)

        for g in range(G):
            gs = slice(g * gw, (g + 1) * gw)
            bm16 = bc16[:, g * N:(g + 1) * N]
            cm16 = bc16[:, (G + g) * N:(G + g + 1) * N]
            cb2 = _dot(cm16, jnp.concatenate([bm16, bm16], axis=0), _NT)
            ssq = jnp.zeros((L, 1), F32)
            for pp in range(ppg):
                p = g * ppg + pp
                sl = slice(p * 2 * P, (p + 1) * 2 * P)
                se = bcast[0:L, sl]
                dec = jnp.exp(jnp.where(causal2, se - srow[p:p + 1, :], -jnp.inf))
                m = (cb2 * dec * dtrow[p:p + 1, :]).astype(BF16)
                xp16 = xs16[:, sl]
                zero = jnp.zeros_like(xp16)
                rhs = jnp.concatenate([jnp.where(left, xp16, zero), jnp.where(left, zero, xp16)], axis=0)
                y = _dot(m, rhs)
                y = y + _dot(cm16, stT[g][:, pp * 2 * P:(pp + 1) * 2 * P].astype(BF16)) * jnp.exp(se) + dsk_ref[:, sl] * xs_s[:, sl]
                yz = y * _silu_half(z_ref[0, pl.ds(c0, L), sl].astype(F32))
                ssq = ssq + jnp.sum(yz * yz, axis=-1, keepdims=True)
                yz_s[:, pp * 2 * P:(pp + 1) * 2 * P] = yz
            s_last = bcast[L - 1:L, gs]
            xdt_end = xs_s[:, gs] * bcast[L:2 * L, gs]
            upd = _dot(bm16, xdt_end.astype(BF16), _TN)
            stT[g][...] = stT[g][...] * jnp.exp(s_last) + upd

            scale = lax.rsqrt(ssq * (1.0 / gw) + SSD_NORM_EPS)
            y_ref[0, pl.ds(c0, L), gs] = (yz_s[...] * scale * gain_ref[:, gs]).astype(y_ref.dtype)

        tailx[...] = xs_ref[0, pl.ds(c0 + L - 2 * SUBLANES, 2 * SUBLANES), :].astype(F32)[SUBLANES:, :]
        tailb[...] = bc_ref[0, pl.ds(c0 + L - 2 * SUBLANES, 2 * SUBLANES), :].astype(F32)[SUBLANES:, :]

        return carry

    lax.fori_loop(0, n_sub, chunk, 0)

    @pl.when(c == n_chunks - 1)
    def _():
        for g in range(G):
            so_ref[0, g * gw:(g + 1) * gw, :] = stT[g][...].T


def _ssd(proj3, small3, col, prm, conv_state, ssd_state):
    b, t, _ = proj3.shape
    n_heads = prm["n_heads"]
    d_inner = n_heads * SSD_HEAD_DIM
    bcw = 2 * SSD_GROUPS * SSD_D_STATE
    has_state = ssd_state is not None
    L = CHUNK
    tb = _time_block(t)
    const = lambda *shape: pl.BlockSpec(shape, lambda i, j: (0,) * len(shape))
    in_specs = [
        pl.BlockSpec((1, tb, d_inner), lambda i, j: (i, j, col["xs"] // d_inner)),
        pl.BlockSpec((1, tb, d_inner), lambda i, j: (i, j, col["z"] // d_inner)),
        pl.BlockSpec((1, tb, bcw), lambda i, j: (i, j, col["bc"] // bcw)),
        pl.BlockSpec((1, tb, SMALL_W), lambda i, j: (i, j, 0)),
        const(SSD_CONV_W, 2 * SUBLANES, d_inner), const(2 * SUBLANES, d_inner),
        const(SSD_CONV_W, 2 * SUBLANES, bcw), const(2 * SUBLANES, bcw),
        const(1, SMALL_W), const(1, SMALL_W), const(1, d_inner), const(1, d_inner),
        const(DT_COPIES * n_heads, d_inner), const(L, 3 * L), const((SSD_CONV_W - 1) * L, L),
    ]
    args = [proj3, proj3, proj3, small3, prm["conv_w_x"], prm["conv_b_x"], prm["conv_w_bc"], prm["conv_b_bc"],
            prm["dt_bias3"], prm["a_log3"], prm["d_skip_e"], prm["ssd_norm_gain"], prm["expand3"], prm["tril3"],
            prm["shift"]]
    if has_state:
        nc = SSD_CONV_W - 1
        in_specs += [
            pl.BlockSpec((1, nc, d_inner), lambda i, j: (i, 0, 0)),
            pl.BlockSpec((1, nc, bcw), lambda i, j: (i, 0, 0)),
            pl.BlockSpec((1, d_inner, SSD_D_STATE), lambda i, j: (i, 0, 0)),
        ]
        args += [conv_state[:, :, :d_inner], conv_state[:, :, d_inner:],
                 ssd_state.reshape(b, d_inner, SSD_D_STATE)]
    y, st = pl.pallas_call(
        functools.partial(_ssd_kernel, has_state, n_heads, tb // L),
        grid=(b, t // tb),
        in_specs=in_specs,
        out_specs=[
            pl.BlockSpec((1, tb, d_inner), lambda i, j: (i, j, 0)),
            pl.BlockSpec((1, d_inner, SSD_D_STATE), lambda i, j: (i, 0, 0)),
        ],
        out_shape=[
            jax.ShapeDtypeStruct((b, t, d_inner), BF16),
            jax.ShapeDtypeStruct((b, d_inner, SSD_D_STATE), F32),
        ],
        scratch_shapes=[
            pltpu.VMEM((SUBLANES, d_inner), F32),
            pltpu.VMEM((SUBLANES, bcw), F32),
            pltpu.VMEM((n_heads // 2, 2 * L), F32),
            pltpu.VMEM((n_heads // 2, 2 * L), F32),
            pltpu.VMEM((L, d_inner), F32),
            pltpu.VMEM((L, d_inner), BF16),
            pltpu.VMEM((L, bcw), BF16),
            pltpu.VMEM((2 * L, d_inner), F32),
            pltpu.VMEM((L, d_inner // SSD_GROUPS), F32),
        ] + [pltpu.VMEM((SSD_D_STATE, d_inner // SSD_GROUPS), F32) for _ in range(SSD_GROUPS)],
        compiler_params=pltpu.CompilerParams(
            dimension_semantics=("arbitrary", "arbitrary"), vmem_limit_bytes=VMEM_LIMIT_BYTES),
    )(*args)
    return y, st.reshape(b, n_heads, SSD_HEAD_DIM, SSD_D_STATE)


def _gla_kernel(has_state, n_sub, *refs):
    if has_state:
        (q_ref, k_ref, v_ref, gb_ref, sm_ref, wgk_ref, bgk_ref, gain_ref, tril_ref, st_ref,
         o_ref, so_ref, bcs, ks, qs, *stT) = refs
    else:
        (q_ref, k_ref, v_ref, gb_ref, sm_ref, wgk_ref, bgk_ref, gain_ref, tril_ref,
         o_ref, so_ref, bcs, ks, qs, *stT) = refs
    c = pl.program_id(1)
    n_chunks = pl.num_programs(1)
    L = CHUNK
    H = GLA_HEADS
    kd = q_ref.shape[2]
    vd = v_ref.shape[2]
    hk = kd // H
    hv = vd // H
    nb = L // SUB

    @pl.when(c == 0)
    def _():
        if has_state:
            for h in range(H):
                stT[h][...] = st_ref[0, h].T
        else:
            for h in range(H):
                stT[h][...] = jnp.zeros_like(stT[h])

    def chunk(si, carry):
        c0 = pl.multiple_of(si * L, L)
        sm = sm_ref[0, pl.ds(c0, L), :]
        sm_hi = sm.astype(BF16)
        sm_lo = (sm - sm_hi.astype(F32)).astype(BF16)
        x = _dot(jnp.concatenate([sm_hi, sm_hi, sm_lo], axis=1), wgk_ref[...]) + bgk_ref[...]
        glog2 = (jnp.minimum(x, 0.0) - jnp.log(1.0 + jnp.exp(-jnp.abs(x)))) * (LOG2E / GLA_GATE_NORMALIZER)
        bc = _cumsum_rows(tril_ref, glog2)
        bcs[...] = bc
        bc_last = bc[L - 1:L, :]
        q = q_ref[0, pl.ds(c0, L), :].astype(F32) * (hk ** -0.5)
        k = k_ref[0, pl.ds(c0, L), :].astype(F32)
        ks[...] = k
        qs[...] = q
        qe = (q * jnp.exp2(bc)).astype(BF16)
        k_end = (k * jnp.exp2(bc_last - bc)).astype(BF16)

        nt = SUB // SUBLANES
        lane = lax.broadcasted_iota(jnp.int32, (SUBLANES, L), 1)
        row = lax.broadcasted_iota(jnp.int32, (SUBLANES, L), 0)
        a_rows = [[None] * nb for _ in range(H)]
        for blk in range(nb):
            r0 = blk * SUB
            off = [None] * H
            if blk > 0:
                ref_row = bcs[r0 - 1:r0, :]
                qd = (q[r0:r0 + SUB, :] * jnp.exp2(bc[r0:r0 + SUB, :] - ref_row)).astype(BF16)
                kt = k[0:r0, :] * jnp.exp2(ref_row - bc[0:r0, :])
                kt = jnp.concatenate([kt, jnp.zeros((L - r0, kd), F32)], axis=0).astype(BF16)
                for h in range(H):
                    off[h] = _dot(qd[:, h * hk:(h + 1) * hk], kt[:, h * hk:(h + 1) * hk], _NT)
            diag = [[jnp.zeros((SUBLANES, L), F32) for _ in range(nt)] for _ in range(H)]
            for jj in range(SUB):
                j = r0 + jj
                for ti in range(jj // SUBLANES, nt):
                    t0 = r0 + ti * SUBLANES
                    keep = (lane == j) & (row + t0 >= j)
                    for h in range(H):
                        tot = None
                        for lt in range(hk // LANES):
                            ls = slice(h * hk + lt * LANES, h * hk + (lt + 1) * LANES)
                            e = jnp.exp2(bcs[t0:t0 + SUBLANES, ls] - bcs[j:j + 1, ls])
                            pr = qs[t0:t0 + SUBLANES, ls] * e * ks[j:j + 1, ls]
                            tot = pr if tot is None else tot + pr
                        col = jnp.sum(tot, axis=-1, keepdims=True)
                        diag[h][ti] = jnp.where(keep, col, diag[h][ti])
            for h in range(H):
                d_h = jnp.concatenate(diag[h], axis=0)
                a_rows[h][blk] = d_h if off[h] is None else off[h] + d_h

        eb_last = jnp.exp2(bc_last)
        for h in range(H):
            a_h = jnp.concatenate(a_rows[h], axis=0).astype(BF16)
            v_h = v_ref[0, pl.ds(c0, L), h * hv:(h + 1) * hv]
            st_h = stT[h][...]
            o = _dot(a_h, v_h) + _dot(qe[:, h * hk:(h + 1) * hk], st_h.astype(BF16), _NT)
            stT[h][...] = st_h * eb_last[:, h * hk:(h + 1) * hk] + _dot(v_h, k_end[:, h * hk:(h + 1) * hk], _TN)
            ms = jnp.mean(o * o, axis=-1, keepdims=True)
            on = o * lax.rsqrt(ms + NORM_EPS) * gain_ref[...]
            gate = _silu_half(gb_ref[0, pl.ds(c0, L), h * hv:(h + 1) * hv].astype(F32))
            o_ref[0, pl.ds(c0, L), h * hv:(h + 1) * hv] = (on * gate).astype(o_ref.dtype)

        return carry

    lax.fori_loop(0, n_sub, chunk, 0)

    @pl.when(c == n_chunks - 1)
    def _():
        for h in range(H):
            so_ref[0, h] = stT[h][...].T


def _gla(proj3, small3, col, prm, gla_state):
    b, t, _ = proj3.shape
    kd = prm["gla_key_dim"]
    vd = prm["gla_val_dim"]
    H = GLA_HEADS
    hk, hv = kd // H, vd // H
    has_state = gla_state is not None
    L = CHUNK
    tb = _time_block(t)
    const = lambda *shape: pl.BlockSpec(shape, lambda i, j: (0,) * len(shape))
    in_specs = [
        pl.BlockSpec((1, tb, kd), lambda i, j: (i, j, col["q"] // kd)),
        pl.BlockSpec((1, tb, kd), lambda i, j: (i, j, col["k"] // kd)),
        pl.BlockSpec((1, tb, vd), lambda i, j: (i, j, col["v"] // vd)),
        pl.BlockSpec((1, tb, vd), lambda i, j: (i, j, col["g_b"] // vd)),
        pl.BlockSpec((1, tb, SMALL_W), lambda i, j: (i, j, 0)),
        const(3 * SMALL_W, kd), const(1, kd), const(1, hv), const(L, 3 * L),
    ]
    args = [proj3, proj3, proj3, proj3, small3, prm["gk_w3"], prm["gk_b"], prm["gla_norm_gain"], prm["tril3"]]
    if has_state:
        in_specs.append(pl.BlockSpec((1, H, hk, hv), lambda i, j: (i, 0, 0, 0)))
        args.append(gla_state)
    return pl.pallas_call(
        functools.partial(_gla_kernel, has_state, tb // L),
        grid=(b, t // tb),
        in_specs=in_specs,
        out_specs=[
            pl.BlockSpec((1, tb, vd), lambda i, j: (i, j, 0)),
            pl.BlockSpec((1, H, hk, hv), lambda i, j: (i, 0, 0, 0)),
        ],
        out_shape=[
            jax.ShapeDtypeStruct((b, t, vd), BF16),
            jax.ShapeDtypeStruct((b, H, hk, hv), F32),
        ],
        scratch_shapes=[
            pltpu.VMEM((L, kd), F32),
            pltpu.VMEM((L, kd), F32),
            pltpu.VMEM((L, kd), F32),
        ] + [pltpu.VMEM((hv, hk), F32) for _ in range(H)],
        compiler_params=pltpu.CompilerParams(
            dimension_semantics=("arbitrary", "arbitrary"), vmem_limit_bytes=VMEM_LIMIT_BYTES),
    )(*args)


def _merge_kernel(tn, ys_ref, yg_ref, g_ref, wbs_ref, wbg_ref, o_ref):
    d = o_ref.shape[1]
    for n in range(d // tn):
        ns = slice(n * tn, (n + 1) * tn)
        a = _dot(ys_ref[...], wbs_ref[:, ns])
        b = _dot(yg_ref[...], wbg_ref[:, ns])
        gate_s = _sigmoid_half(g_ref[:, ns].astype(F32))
        gate_g = _sigmoid_half(g_ref[:, d + n * tn:d + (n + 1) * tn].astype(F32))
        o_ref[:, ns] = (gate_s * a + gate_g * b).astype(o_ref.dtype)


def _post_kernel(mx_ref, x_ref, wo_ref, g_ref, o_ref):
    cc = _dot(mx_ref[...], wo_ref[...])
    ms = jnp.mean(cc * cc, axis=-1, keepdims=True)
    o_ref[...] = x_ref[...] + cc * lax.rsqrt(ms + NORM_EPS) * g_ref[...]


def _outproj(y_ssd, y_gla, proj, col, x2, prm, tm_merge, tn, tm_post):
    m, d = x2.shape
    d_inner = y_ssd.shape[1]
    vd = y_gla.shape[1]
    g0 = col["merge"] // (2 * d)
    resident = lambda r, c_: pl.BlockSpec((r, c_), lambda i: (0, 0), pipeline_mode=pl.Buffered(1))
    mixed = pl.pallas_call(
        functools.partial(_merge_kernel, tn),
        grid=(m // tm_merge,),
        in_specs=[
            pl.BlockSpec((tm_merge, d_inner), lambda i: (i, 0)),
            pl.BlockSpec((tm_merge, vd), lambda i: (i, 0)),
            pl.BlockSpec((tm_merge, 2 * d), lambda i: (i, g0)),
            resident(d_inner, d), resident(vd, d),
        ],
        out_specs=pl.BlockSpec((tm_merge, d), lambda i: (i, 0)),
        out_shape=jax.ShapeDtypeStruct((m, d), BF16),
        compiler_params=pltpu.CompilerParams(
            dimension_semantics=("arbitrary",), vmem_limit_bytes=VMEM_LIMIT_BYTES),
    )(y_ssd, y_gla, proj, prm["w_branch_ssd"], prm["w_branch_gla"])
    return pl.pallas_call(
        _post_kernel,
        grid=(m // tm_post,),
        in_specs=[
            pl.BlockSpec((tm_post, d), lambda i: (i, 0)),
            pl.BlockSpec((tm_post, d), lambda i: (i, 0)),
            pl.BlockSpec((d, d), lambda i: (0, 0)),
            pl.BlockSpec((1, d), lambda i: (0, 0)),
        ],
        out_specs=pl.BlockSpec((tm_post, d), lambda i: (i, 0)),
        out_shape=jax.ShapeDtypeStruct((m, d), F32),
        compiler_params=pltpu.CompilerParams(
            dimension_semantics=("arbitrary",), vmem_limit_bytes=VMEM_LIMIT_BYTES),
    )(mixed, x2, prm["w_out"], prm["norm_post_gain"])


def _regroup_kernel(tn, offsets, blk_ref, cls_ref, half_ref, a_ref, b_ref, o_ref):
    j = pl.program_id(0)
    scale = jnp.where(half_ref[j] == 1, 0.5, 1.0).astype(F32)
    for ci, off in enumerate(offsets):
        @pl.when(cls_ref[j] == ci)
        def _():
            x = a_ref[...] if off == 0 else jnp.concatenate([a_ref[off:tn, :], b_ref[0:off, :]], axis=0)
            o_ref[...] = (x.T * scale).astype(o_ref.dtype)


def _pick_rows_kernel(spans, *refs):
    for (start, size), src, dst in zip(spans, refs[:len(spans)], refs[len(spans):]):
        dst[...] = src[start % LANES:start % LANES + size, :]


def _pick_rows(a, *spans):
    d = a.shape[1]
    assert all(start % LANES + size <= LANES and size % SUBLANES == 0 for start, size in spans)
    return pl.pallas_call(
        functools.partial(_pick_rows_kernel, spans),
        grid=(1,),
        in_specs=[pl.BlockSpec((LANES, d), functools.partial(lambda blk, i: (blk, 0), start // LANES))
                  for start, _ in spans],
        out_specs=[pl.BlockSpec((size, d), lambda i: (0, 0)) for _, size in spans],
        out_shape=[jax.ShapeDtypeStruct((size, d), a.dtype) for _, size in spans],
    )(*([a] * len(spans)))


def _regroup_w_in(w_in_t, segments, tn):
    d = w_in_t.shape[1]
    offsets = sorted({src % tn for src, _, _ in segments})
    blk, cls, half = [], [], []
    for src, width, halve in segments:
        assert width % tn == 0 and src % tn < LANES and (src % tn) % SUBLANES == 0
        for i in range(width // tn):
            blk.append(src // tn + i)
            cls.append(offsets.index(src % tn))
            half.append(int(halve))
    n_blocks = len(blk)
    to_i32 = lambda v: jnp.asarray(np.array(v, np.int32))
    return pl.pallas_call(
        functools.partial(_regroup_kernel, tn, tuple(offsets)),
        grid_spec=pltpu.PrefetchScalarGridSpec(
            num_scalar_prefetch=3,
            grid=(n_blocks,),
            in_specs=[
                pl.BlockSpec((tn, d), lambda j, blk, cls, half: (blk[j], 0)),
                pl.BlockSpec((LANES, d), lambda j, blk, cls, half: ((blk[j] + 1) * (tn // LANES), 0)),
            ],
            out_specs=pl.BlockSpec((d, tn), lambda j, blk, cls, half: (0, j)),
        ),
        out_shape=jax.ShapeDtypeStruct((d, n_blocks * tn), BF16),
        compiler_params=pltpu.CompilerParams(
            dimension_semantics=("arbitrary",), vmem_limit_bytes=VMEM_LIMIT_BYTES),
    )(to_i32(blk), to_i32(cls), to_i32(half), w_in_t, w_in_t)
def _prepare(norm_pre_gain, w_in, conv_w, conv_b, dt_bias, a_log, d_skip, ssd_norm_gain, gla_gk_w, gla_gk_b,
             gla_norm_gain, w_branch_ssd, w_branch_gla, w_out, norm_post_gain):
    d = w_in.shape[0]
    n_heads = dt_bias.shape[0]
    d_inner = n_heads * SSD_HEAD_DIM
    bcw = 2 * SSD_GROUPS * SSD_D_STATE
    kd = gla_gk_w.shape[1]
    vd = w_branch_gla.shape[0]
    assert DT_COPIES * n_heads + GLA_GATE_RANK <= SMALL_W
    sizes = (d_inner, d_inner + bcw, n_heads, kd, kd, vd, vd, GLA_GATE_RANK, 2 * d)
    offs = np.concatenate([[0], np.cumsum(sizes)])
    z0, xbc0, dt0, q0, k0, v0, gb0, gkl0, mg0 = [int(o_) for o_ in offs[:9]]
    perm = np.concatenate([np.arange(0, n_heads, 2), np.arange(1, n_heads, 2)])
    w_in_t = w_in.T
    dt_rows, gk_rows = _pick_rows(w_in_t, (dt0, n_heads), (gkl0, GLA_GATE_RANK))
    dt_w = dt_rows[perm]
    gk0 = DT_COPIES * n_heads
    pad = SMALL_W - gk0 - GLA_GATE_RANK
    w_small = jnp.concatenate(
        [dt_w] * DT_COPIES + [gk_rows, jnp.zeros((pad, d), w_in.dtype)], axis=0).T.astype(BF16)
    half = 0.5
    pieces = [("xs", xbc0, d_inner, False), ("z", z0, d_inner, True), ("merge", mg0, 2 * d, True),
              ("bc", xbc0 + d_inner, bcw, False), ("v", v0, vd, False), ("g_b", gb0, vd, True),
              ("q", q0, kd, False), ("k", k0, kd, False)]
    col, o = {}, 0
    for name, _, width, _ in pieces:
        assert o % width == 0, name
        col[name] = o
        o += width
    w_main = _regroup_w_in(w_in_t, [p[1:] for p in pieces], REGROUP_TN)

    def small_row(v, fill):
        return jnp.concatenate([v[perm]] * DT_COPIES + [jnp.full((SMALL_W - gk0,), fill, v.dtype)])[None, :]

    expand = np.zeros((DT_COPIES * n_heads, d_inner), np.float32)
    for r in range(DT_COPIES):
        for c_ in range(n_heads):
            expand[r * n_heads + c_, perm[c_] * SSD_HEAD_DIM:(perm[c_] + 1) * SSD_HEAD_DIM] = 1.0
    tril = np.tril(np.ones((CHUNK, CHUNK), np.float32))
    rb = 2 * SUBLANES
    shift = np.concatenate([np.eye(CHUNK, k=-k, dtype=np.float32)[b * rb:(b + 1) * rb]
                            for b in range(CHUNK // rb) for k in range(1, SSD_CONV_W)], axis=0)
    gk_w_pad = jnp.zeros((SMALL_W, kd), F32).at[gk0:gk0 + GLA_GATE_RANK].set(gla_gk_w)
    gk_hi = gk_w_pad.astype(BF16)
    gk_lo = (gk_w_pad - gk_hi.astype(F32)).astype(BF16)
    rows = lambda a: jnp.broadcast_to(a, a.shape[:-2] + (2 * SUBLANES, a.shape[-1]))
    prm = dict(
        n_heads=n_heads, gla_key_dim=kd, gla_val_dim=vd,
        norm_pre_gain=norm_pre_gain[None, :], w_main=w_main, w_small=w_small,
        conv_w_x=rows(half * conv_w[:, None, :d_inner]), conv_w_bc=rows(half * conv_w[:, None, d_inner:]),
        conv_b_x=rows(half * conv_b[None, :d_inner]), conv_b_bc=rows(half * conv_b[None, d_inner:]),
        dt_bias3=small_row(dt_bias, 0.0), a_log3=small_row(a_log, 0.0),
        d_skip_e=jnp.repeat(d_skip, SSD_HEAD_DIM)[None, :], ssd_norm_gain=ssd_norm_gain[None, :],
        shift=jnp.asarray(shift, BF16), expand3=jnp.asarray(expand, BF16), tril3=jnp.asarray(np.concatenate([tril] * 3, axis=1), BF16),
        gk_w3=jnp.concatenate([gk_hi, gk_lo, gk_hi], axis=0), gk_b=gla_gk_b[None, :],
        gla_norm_gain=gla_norm_gain[None, :],
        w_branch_ssd=w_branch_ssd.astype(BF16), w_branch_gla=w_branch_gla.astype(BF16),
        w_out=w_out.astype(BF16), norm_post_gain=norm_post_gain[None, :],
    )
    return prm, col


def _time_block(t):
    tb = min(t, CHUNKS_PER_STEP * CHUNK)
    assert t % tb == 0
    return tb


def _row_tile(m, want):
    tm = min(m, want)
    assert m % tm == 0
    return tm


def _layer(x, conv_state, ssd_state, gla_state, prm, col):
    b, t, d = x.shape
    assert t % CHUNK == 0 and t >= SSD_CONV_W - 1
    m = b * t
    x2 = x.reshape(m, d)
    proj, small = _inproj(x2, prm["norm_pre_gain"], prm["w_main"], prm["w_small"], _row_tile(m, INPROJ_TM), INPROJ_TN)
    proj3 = proj.reshape(b, t, -1)
    small3 = small.reshape(b, t, -1)
    d_inner = prm["n_heads"] * SSD_HEAD_DIM
    bcw = 2 * SSD_GROUPS * SSD_D_STATE
    nc = SSD_CONV_W - 1
    new_conv = jnp.concatenate(
        [proj3[:, t - nc:, col["xs"]:col["xs"] + d_inner], proj3[:, t - nc:, col["bc"]:col["bc"] + bcw]],
        axis=-1).astype(x.dtype)
    y_ssd, new_ssd = _ssd(proj3, small3, col, prm, conv_state, ssd_state)
    y_gla, new_gla = _gla(proj3, small3, col, prm, gla_state)
    y = _outproj(y_ssd.reshape(m, -1), y_gla.reshape(m, -1), proj, col, x2, prm,
                 _row_tile(m, MERGE_TM), MERGE_TN, _row_tile(m, POST_TM))
    return y.reshape(b, t, d), new_conv, new_ssd, new_gla


def kernel(x_prompt, x_sample, state_conv_ssd, state_ssd, state_gla, norm_pre_gain, w_in, conv_w, conv_b, dt_bias,
           a_log, d_skip, ssd_norm_gain, gla_gk_w, gla_gk_b, gla_norm_gain, w_branch_ssd, w_branch_gla, w_out,
           norm_post_gain):
    depth = w_in.shape[0]
    yp, ys = x_prompt, x_sample
    outs = [[] for _ in range(6)]
    for layer in range(depth):
        prm, col = _prepare(norm_pre_gain[layer], w_in[layer], conv_w[layer], conv_b[layer], dt_bias[layer],
                            a_log[layer], d_skip[layer], ssd_norm_gain[layer], gla_gk_w[layer], gla_gk_b[layer],
                            gla_norm_gain[layer], w_branch_ssd[layer], w_branch_gla[layer], w_out[layer],
                            norm_post_gain[layer])
        yp, c_p, s_p, g_p = _layer(yp, None, None, None, prm, col)
        ys, c_s, s_s, g_s = _layer(ys, state_conv_ssd[layer], state_ssd[layer], state_gla[layer], prm, col)
        for lst, val in zip(outs, (c_p, s_p, g_p, c_s, s_s, g_s)):
            lst.append(val)
    return (yp, ys) + tuple(jnp.stack(o) for o in outs)
```

```python
import functools
import math

import numpy as np
import jax
import jax.numpy as jnp
from jax import lax
from jax.experimental import pallas as pl
from jax.experimental.pallas import tpu as pltpu

F32 = jnp.float32
BF16 = jnp.bfloat16

LANES = 128
SUBLANES = 8
VMEM_LIMIT_BYTES = 56 * 1024 * 1024

CHUNK = 64
SSD_HEAD_DIM = 64
SSD_GROUPS = 8
SSD_D_STATE = 128
SSD_CONV_W = 4
GLA_HEADS = 4
GLA_GATE_RANK = 16
GLA_GATE_NORMALIZER = 16.0
NORM_EPS = 1e-6
SSD_NORM_EPS = 1e-5
SUB = 8

CHUNKS_PER_STEP = 8
CONV_LANES = 2 * LANES
INPROJ_TM, INPROJ_TN = 1024, 2048
MERGE_TM, MERGE_TN = 256, 1024
POST_TM = 512
REGROUP_TN = 1024
SMALL_W = 256
DT_COPIES = 2
LOG2E = math.log2(math.e)


def _dot(a, b, dims=(((1,), (0,)), ((), ()))):
    return lax.dot_general(a, b, dims, preferred_element_type=F32)


_NT = (((1,), (1,)), ((), ()))
_TN = (((0,), (0,)), ((), ()))


def _sigmoid_half(h):
    return 0.5 * jnp.tanh(h) + 0.5


def _silu_half(h):
    return h * jnp.tanh(h) + h


def _softplus(x):
    return jnp.maximum(x, 0.0) + jnp.log(1.0 + jnp.exp(-jnp.abs(x)))


def _residuals(x):
    r1 = x - x.astype(BF16).astype(F32)
    r2 = r1 - r1.astype(BF16).astype(F32)
    return x, r1, r2


def _cumsum_rows(tril3_ref, x):
    parts = [p.astype(BF16) for p in _residuals(x)]
    return _dot(tril3_ref[...], jnp.concatenate(parts, axis=0))


def _inproj_kernel(x_ref, g_ref, w_ref, ws_ref, o_ref, os_ref, h_ref):
    @pl.when(pl.program_id(1) == 0)
    def _():
        x = x_ref[...]
        ms = jnp.mean(x * x, axis=-1, keepdims=True)
        h_ref[...] = (x * lax.rsqrt(ms + NORM_EPS) * g_ref[...]).astype(BF16)
        os_ref[...] = _dot(h_ref[...], ws_ref[...])

    o_ref[...] = _dot(h_ref[...], w_ref[...]).astype(o_ref.dtype)


def _inproj(x2, gain, w_main, w_small, tm, tn):
    m, d = x2.shape
    n = w_main.shape[1]
    ns = w_small.shape[1]
    return pl.pallas_call(
        _inproj_kernel,
        grid=(m // tm, n // tn),
        in_specs=[
            pl.BlockSpec((tm, d), lambda i, j: (i, 0)),
            pl.BlockSpec((1, d), lambda i, j: (0, 0)),
            pl.BlockSpec((d, tn), lambda i, j: (0, j)),
            pl.BlockSpec((d, ns), lambda i, j: (0, 0)),
        ],
        out_specs=[
            pl.BlockSpec((tm, tn), lambda i, j: (i, j)),
            pl.BlockSpec((tm, ns), lambda i, j: (i, 0)),
        ],
        out_shape=[jax.ShapeDtypeStruct((m, n), BF16), jax.ShapeDtypeStruct((m, ns), F32)],
        scratch_shapes=[pltpu.VMEM((tm, d), BF16)],
        compiler_params=pltpu.CompilerParams(
            dimension_semantics=("arbitrary", "arbitrary"), vmem_limit_bytes=VMEM_LIMIT_BYTES),
    )(x2, gain, w_main, w_small)


def _ssd_kernel(has_state, n_heads, n_sub, *refs):
    if has_state:
        (xs_ref, z_ref, bc_ref, sm_ref, cwx_ref, cbx_ref, cwb_ref, cbb_ref, dtb_ref, alog_ref, dsk_ref,
         gain_ref, exp_ref, tril_ref, shift_ref, csx_ref, csb_ref, st_ref,
         y_ref, so_ref, tailx, tailb, srow, dtrow, xs_s, xs16, bc16, bcast, yz_s, *stT) = refs
    else:
        (xs_ref, z_ref, bc_ref, sm_ref, cwx_ref, cbx_ref, cwb_ref, cbb_ref, dtb_ref, alog_ref, dsk_ref,
         gain_ref, exp_ref, tril_ref, shift_ref,
         y_ref, so_ref, tailx, tailb, srow, dtrow, xs_s, xs16, bc16, bcast, yz_s, *stT) = refs
    c = pl.program_id(1)
    n_chunks = pl.num_programs(1)
    L = CHUNK
    P = SSD_HEAD_DIM
    N = SSD_D_STATE
    G = SSD_GROUPS
    gw = (n_heads // G) * P
    d_inner = n_heads * P
    nc = SSD_CONV_W - 1

    @pl.when(c == 0)
    def _():
        tailx[...] = jnp.zeros_like(tailx)
        tailb[...] = jnp.zeros_like(tailb)
        if has_state:
            tailx[SUBLANES - nc:SUBLANES, :] = csx_ref[0]
            tailb[SUBLANES - nc:SUBLANES, :] = csb_ref[0]
            for g in range(G):
                stT[g][...] = st_ref[0, g * gw:(g + 1) * gw, :].T
        else:
            for g in range(G):
                stT[g][...] = jnp.zeros_like(stT[g])

    def chunk(si, carry):
        c0 = pl.multiple_of(si * L, L)
        cw = CONV_LANES
        rb = 2 * SUBLANES
        row8 = lax.broadcasted_iota(jnp.int32, (SUBLANES, cw), 0)

        def conv(x_ref, tail, w_ref, b_ref, width, emit):
            for cb in range(width // cw):
                cs = slice(cb * cw, (cb + 1) * cw)
                sh = _dot(shift_ref[...], x_ref[0, pl.ds(c0, L), cs])
                prev = tail[:, cs]
                fix = jnp.zeros((SUBLANES, cw), F32)
                for k in range(1, SSD_CONV_W):
                    fix = fix + jnp.where(row8 < k, pltpu.roll(prev, k, 0), 0.0) * w_ref[nc - k, 0:SUBLANES, cs]
                for blk in range(L // rb):
                    acc = sh[blk * nc * rb:(blk * nc + 1) * rb, :] * w_ref[nc - 1, :, cs]
                    for k in range(2, SSD_CONV_W):
                        s0 = (blk * nc + k - 1) * rb
                        acc = acc + sh[s0:s0 + rb, :] * w_ref[nc - k, :, cs]
                    acc = acc + x_ref[0, pl.ds(c0 + blk * rb, rb), cs].astype(F32) * w_ref[nc, :, cs] + b_ref[:, cs]
                    if blk == 0:
                        acc = jnp.concatenate([acc[0:SUBLANES, :] + fix, acc[SUBLANES:, :]], axis=0)
                    emit(blk * rb, rb, cs, _silu_half(acc))

        def emit_x(r0, nr, cs, act):
            xs_s[r0:r0 + nr, cs] = act
            xs16[r0:r0 + nr, cs] = act.astype(BF16)

        def emit_bc(r0, nr, cs, act):
            bc16[r0:r0 + nr, cs] = act.astype(BF16)

        conv(xs_ref, tailx, cwx_ref, cbx_ref, d_inner, emit_x)
        conv(bc_ref, tailb, cwb_ref, cbb_ref, 2 * G * N, emit_bc)

        dt3 = _softplus(sm_ref[0, pl.ds(c0, L), :] + dtb_ref[...])
        s3 = _cumsum_rows(tril_ref, dt3 * (-jnp.exp(alog_ref[...])))
        w3 = dt3 * jnp.exp(s3[L - 1:L, :] - s3)
        lane_s = lax.broadcasted_iota(jnp.int32, (2 * L, SMALL_W), 1)
        x0, x1, _ = _residuals(jnp.concatenate([s3, w3], axis=0))
        split = jnp.where(lane_s < n_heads, x0, x1)[:, 0:DT_COPIES * n_heads].astype(BF16)
        bcast[...] = _dot(split, exp_ref[...])
        hh = n_heads // 2
        s_t = s3[:, 0:2 * L].T
        dt_t = dt3[:, 0:2 * L].T
        srow[...] = jnp.concatenate([s_t[0:hh], s_t[hh:2 * hh]], axis=1)
        dtrow[...] = jnp.concatenate([dt_t[0:hh], dt_t[hh:2 * hh]], axis=1)

        row = lax.broadcasted_iota(jnp.int32, (L, 2 * P), 0)
        lane = lax.broadcasted_iota(jnp.int32, (L, 2 * P), 1)
        causal2 = (lane % P) <= row
        left = lane < P
        ppg = gw // (2 * P)

        for g in range(G):
            gs = slice(g * gw, (g + 1) * gw)
            bm16 = bc16[:, g * N:(g + 1) * N]
            cm16 = bc16[:, (G + g) * N:(G + g + 1) * N]
            cb2 = _dot(cm16, jnp.concatenate([bm16, bm16], axis=0), _NT)
            ssq = jnp.zeros((L, 1), F32)
            for pp in range(ppg):
                p = g * ppg + pp
                sl = slice(p * 2 * P, (p + 1) * 2 * P)
                se = bcast[0:L, sl]
                dec = jnp.exp(jnp.where(causal2, se - srow[p:p + 1, :], -jnp.inf))
                m = (cb2 * dec * dtrow[p:p + 1, :]).astype(BF16)
                xp16 = xs16[:, sl]
                zero = jnp.zeros_like(xp16)
                rhs = jnp.concatenate([jnp.where(left, xp16, zero), jnp.where(left, zero, xp16)], axis=0)
                y = _dot(m, rhs)
                y = y + _dot(cm16, stT[g][:, pp * 2 * P:(pp + 1) * 2 * P].astype(BF16)) * jnp.exp(se) + dsk_ref[:, sl] * xs_s[:, sl]
                yz = y * _silu_half(z_ref[0, pl.ds(c0, L), sl].astype(F32))
                ssq = ssq + jnp.sum(yz * yz, axis=-1, keepdims=True)
                yz_s[:, pp * 2 * P:(pp + 1) * 2 * P] = yz
            s_last = bcast[L - 1:L, gs]
            xdt_end = xs_s[:, gs] * bcast[L:2 * L, gs]
            upd = _dot(bm16, xdt_end.astype(BF16), _TN)
            stT[g][...] = stT[g][...] * jnp.exp(s_last) + upd

            scale = lax.rsqrt(ssq * (1.0 / gw) + SSD_NORM_EPS)
            y_ref[0, pl.ds(c0, L), gs] = (yz_s[...] * scale * gain_ref[:, gs]).astype(y_ref.dtype)

        tailx[...] = xs_ref[0, pl.ds(c0 + L - 2 * SUBLANES, 2 * SUBLANES), :].astype(F32)[SUBLANES:, :]
        tailb[...] = bc_ref[0, pl.ds(c0 + L - 2 * SUBLANES, 2 * SUBLANES), :].astype(F32)[SUBLANES:, :]

        return carry

    lax.fori_loop(0, n_sub, chunk, 0)

    @pl.when(c == n_chunks - 1)
    def _():
        for g in range(G):
            so_ref[0, g * gw:(g + 1) * gw, :] = stT[g][...].T


def _ssd(proj3, small3, col, prm, conv_state, ssd_state):
    b, t, _ = proj3.shape
    n_heads = prm["n_heads"]
    d_inner = n_heads * SSD_HEAD_DIM
    bcw = 2 * SSD_GROUPS * SSD_D_STATE
    has_state = ssd_state is not None
    L = CHUNK
    tb = _time_block(t)
    const = lambda *shape: pl.BlockSpec(shape, lambda i, j: (0,) * len(shape))
    in_specs = [
        pl.BlockSpec((1, tb, d_inner), lambda i, j: (i, j, col["xs"] // d_inner)),
        pl.BlockSpec((1, tb, d_inner), lambda i, j: (i, j, col["z"] // d_inner)),
        pl.BlockSpec((1, tb, bcw), lambda i, j: (i, j, col["bc"] // bcw)),
        pl.BlockSpec((1, tb, SMALL_W), lambda i, j: (i, j, 0)),
        const(SSD_CONV_W, 2 * SUBLANES, d_inner), const(2 * SUBLANES, d_inner),
        const(SSD_CONV_W, 2 * SUBLANES, bcw), const(2 * SUBLANES, bcw),
        const(1, SMALL_W), const(1, SMALL_W), const(1, d_inner), const(1, d_inner),
        const(DT_COPIES * n_heads, d_inner), const(L, 3 * L), const((SSD_CONV_W - 1) * L, L),
    ]
    args = [proj3, proj3, proj3, small3, prm["conv_w_x"], prm["conv_b_x"], prm["conv_w_bc"], prm["conv_b_bc"],
            prm["dt_bias3"], prm["a_log3"], prm["d_skip_e"], prm["ssd_norm_gain"], prm["expand3"], prm["tril3"],
            prm["shift"]]
    if has_state:
        nc = SSD_CONV_W - 1
        in_specs += [
            pl.BlockSpec((1, nc, d_inner), lambda i, j: (i, 0, 0)),
            pl.BlockSpec((1, nc, bcw), lambda i, j: (i, 0, 0)),
            pl.BlockSpec((1, d_inner, SSD_D_STATE), lambda i, j: (i, 0, 0)),
        ]
        args += [conv_state[:, :, :d_inner], conv_state[:, :, d_inner:],
                 ssd_state.reshape(b, d_inner, SSD_D_STATE)]
    y, st = pl.pallas_call(
        functools.partial(_ssd_kernel, has_state, n_heads, tb // L),
        grid=(b, t // tb),
        in_specs=in_specs,
        out_specs=[
            pl.BlockSpec((1, tb, d_inner), lambda i, j: (i, j, 0)),
            pl.BlockSpec((1, d_inner, SSD_D_STATE), lambda i, j: (i, 0, 0)),
        ],
        out_shape=[
            jax.ShapeDtypeStruct((b, t, d_inner), BF16),
            jax.ShapeDtypeStruct((b, d_inner, SSD_D_STATE), F32),
        ],
        scratch_shapes=[
            pltpu.VMEM((SUBLANES, d_inner), F32),
            pltpu.VMEM((SUBLANES, bcw), F32),
            pltpu.VMEM((n_heads // 2, 2 * L), F32),
            pltpu.VMEM((n_heads // 2, 2 * L), F32),
            pltpu.VMEM((L, d_inner), F32),
            pltpu.VMEM((L, d_inner), BF16),
            pltpu.VMEM((L, bcw), BF16),
            pltpu.VMEM((2 * L, d_inner), F32),
            pltpu.VMEM((L, d_inner // SSD_GROUPS), F32),
        ] + [pltpu.VMEM((SSD_D_STATE, d_inner // SSD_GROUPS), F32) for _ in range(SSD_GROUPS)],
        compiler_params=pltpu.CompilerParams(
            dimension_semantics=("arbitrary", "arbitrary"), vmem_limit_bytes=VMEM_LIMIT_BYTES),
    )(*args)
    return y, st.reshape(b, n_heads, SSD_HEAD_DIM, SSD_D_STATE)


def _gla_kernel(has_state, n_sub, *refs):
    if has_state:
        (q_ref, k_ref, v_ref, gb_ref, sm_ref, wgk_ref, bgk_ref, gain_ref, tril_ref, st_ref,
         o_ref, so_ref, bcs, ks, qs, a_s, *stT) = refs
    else:
        (q_ref, k_ref, v_ref, gb_ref, sm_ref, wgk_ref, bgk_ref, gain_ref, tril_ref,
         o_ref, so_ref, bcs, ks, qs, a_s, *stT) = refs
    c = pl.program_id(1)
    n_chunks = pl.num_programs(1)
    L = CHUNK
    H = GLA_HEADS
    kd = q_ref.shape[2]
    vd = v_ref.shape[2]
    hk = kd // H
    hv = vd // H
    nb = L // SUB

    @pl.when(c == 0)
    def _():
        if has_state:
            for h in range(H):
                stT[h][...] = st_ref[0, h].T
        else:
            for h in range(H):
                stT[h][...] = jnp.zeros_like(stT[h])

    def chunk(si, carry):
        c0 = pl.multiple_of(si * L, L)
        sm = sm_ref[0, pl.ds(c0, L), :]
        sm_hi = sm.astype(BF16)
        sm_lo = (sm - sm_hi.astype(F32)).astype(BF16)
        x = _dot(jnp.concatenate([sm_hi, sm_hi, sm_lo], axis=1), wgk_ref[...]) + bgk_ref[...]
        glog2 = (jnp.minimum(x, 0.0) - jnp.log(1.0 + jnp.exp(-jnp.abs(x)))) * (LOG2E / GLA_GATE_NORMALIZER)
        bcs[...] = _cumsum_rows(tril_ref, glog2)
        qs[...] = q_ref[0, pl.ds(c0, L), :].astype(F32) * (hk ** -0.5)
        ks[...] = k_ref[0, pl.ds(c0, L), :].astype(F32)

        nt = SUB // SUBLANES
        lane = lax.broadcasted_iota(jnp.int32, (SUBLANES, L), 1)
        row = lax.broadcasted_iota(jnp.int32, (SUBLANES, L), 0)
        for blk in range(nb):
            r0 = blk * SUB
            off = [None] * H
            if blk > 0:
                ref_row = bcs[r0 - 1:r0, :]
                qd = (qs[r0:r0 + SUB, :] * jnp.exp2(bcs[r0:r0 + SUB, :] - ref_row)).astype(BF16)
                kt = ks[0:r0, :] * jnp.exp2(ref_row - bcs[0:r0, :])
                kt = jnp.concatenate([kt, jnp.zeros((L - r0, kd), F32)], axis=0).astype(BF16)
                for h in range(H):
                    off[h] = _dot(qd[:, h * hk:(h + 1) * hk], kt[:, h * hk:(h + 1) * hk], _NT)
            diag = [[jnp.zeros((SUBLANES, L), F32) for _ in range(nt)] for _ in range(H)]
            for jj in range(SUB):
                j = r0 + jj
                for ti in range(jj // SUBLANES, nt):
                    t0 = r0 + ti * SUBLANES
                    keep = (lane == j) & (row + t0 >= j)
                    for h in range(H):
                        tot = None
                        for lt in range(hk // LANES):
                            ls = slice(h * hk + lt * LANES, h * hk + (lt + 1) * LANES)
                            e = jnp.exp2(bcs[t0:t0 + SUBLANES, ls] - bcs[j:j + 1, ls])
                            pr = qs[t0:t0 + SUBLANES, ls] * e * ks[j:j + 1, ls]
                            tot = pr if tot is None else tot + pr
                        col = jnp.sum(tot, axis=-1, keepdims=True)
                        diag[h][ti] = jnp.where(keep, col, diag[h][ti])
            for h in range(H):
                d_h = jnp.concatenate(diag[h], axis=0)
                a_s[h, r0:r0 + SUB, :] = d_h if off[h] is None else off[h] + d_h

        for h in range(H):
            hs = slice(h * hk, (h + 1) * hk)
            bc_h = bcs[:, hs]
            bc_last = bcs[L - 1:L, hs]
            qe = (qs[:, hs] * jnp.exp2(bc_h)).astype(BF16)
            k_end = (ks[:, hs] * jnp.exp2(bc_last - bc_h)).astype(BF16)
            v_h = v_ref[0, pl.ds(c0, L), h * hv:(h + 1) * hv]
            o = _dot(a_s[h].astype(BF16), v_h) + _dot(qe, stT[h][...].astype(BF16), _NT)
            stT[h][...] = stT[h][...] * jnp.exp2(bc_last) + _dot(v_h, k_end, _TN)
            ms = jnp.mean(o * o, axis=-1, keepdims=True)
            on = o * lax.rsqrt(ms + NORM_EPS) * gain_ref[...]
            gate = _silu_half(gb_ref[0, pl.ds(c0, L), h * hv:(h + 1) * hv].astype(F32))
            o_ref[0, pl.ds(c0, L), h * hv:(h + 1) * hv] = (on * gate).astype(o_ref.dtype)

        return carry

    lax.fori_loop(0, n_sub, chunk, 0)

    @pl.when(c == n_chunks - 1)
    def _():
        for h in range(H):
            so_ref[0, h] = stT[h][...].T


def _gla(proj3, small3, col, prm, gla_state):
    b, t, _ = proj3.shape
    kd = prm["gla_key_dim"]
    vd = prm["gla_val_dim"]
    H = GLA_HEADS
    hk, hv = kd // H, vd // H
    has_state = gla_state is not None
    L = CHUNK
    tb = _time_block(t)
    const = lambda *shape: pl.BlockSpec(shape, lambda i, j: (0,) * len(shape))
    in_specs = [
        pl.BlockSpec((1, tb, kd), lambda i, j: (i, j, col["q"] // kd)),
        pl.BlockSpec((1, tb, kd), lambda i, j: (i, j, col["k"] // kd)),
        pl.BlockSpec((1, tb, vd), lambda i, j: (i, j, col["v"] // vd)),
        pl.BlockSpec((1, tb, vd), lambda i, j: (i, j, col["g_b"] // vd)),
        pl.BlockSpec((1, tb, SMALL_W), lambda i, j: (i, j, 0)),
        const(3 * SMALL_W, kd), const(1, kd), const(1, hv), const(L, 3 * L),
    ]
    args = [proj3, proj3, proj3, proj3, small3, prm["gk_w3"], prm["gk_b"], prm["gla_norm_gain"], prm["tril3"]]
    if has_state:
        in_specs.append(pl.BlockSpec((1, H, hk, hv), lambda i, j: (i, 0, 0, 0)))
        args.append(gla_state)
    return pl.pallas_call(
        functools.partial(_gla_kernel, has_state, tb // L),
        grid=(b, t // tb),
        in_specs=in_specs,
        out_specs=[
            pl.BlockSpec((1, tb, vd), lambda i, j: (i, j, 0)),
            pl.BlockSpec((1, H, hk, hv), lambda i, j: (i, 0, 0, 0)),
        ],
        out_shape=[
            jax.ShapeDtypeStruct((b, t, vd), BF16),
            jax.ShapeDtypeStruct((b, H, hk, hv), F32),
        ],
        scratch_shapes=[
            pltpu.VMEM((L, kd), F32),
            pltpu.VMEM((L, kd), F32),
            pltpu.VMEM((L, kd), F32),
            pltpu.VMEM((H, L, L), F32),
        ] + [pltpu.VMEM((hv, hk), F32) for _ in range(H)],
        compiler_params=pltpu.CompilerParams(
            dimension_semantics=("arbitrary", "arbitrary"), vmem_limit_bytes=VMEM_LIMIT_BYTES),
    )(*args)


def _merge_kernel(tn, ys_ref, yg_ref, g_ref, wbs_ref, wbg_ref, o_ref):
    d = o_ref.shape[1]
    for n in range(d // tn):
        ns = slice(n * tn, (n + 1) * tn)
        a = _dot(ys_ref[...], wbs_ref[:, ns])
        b = _dot(yg_ref[...], wbg_ref[:, ns])
        gate_s = _sigmoid_half(g_ref[:, ns].astype(F32))
        gate_g = _sigmoid_half(g_ref[:, d + n * tn:d + (n + 1) * tn].astype(F32))
        o_ref[:, ns] = (gate_s * a + gate_g * b).astype(o_ref.dtype)


def _post_kernel(mx_ref, x_ref, wo_ref, g_ref, o_ref):
    cc = _dot(mx_ref[...], wo_ref[...])
    ms = jnp.mean(cc * cc, axis=-1, keepdims=True)
    o_ref[...] = x_ref[...] + cc * lax.rsqrt(ms + NORM_EPS) * g_ref[...]


def _outproj(y_ssd, y_gla, proj, col, x2, prm, tm_merge, tn, tm_post):
    m, d = x2.shape
    d_inner = y_ssd.shape[1]
    vd = y_gla.shape[1]
    g0 = col["merge"] // (2 * d)
    resident = lambda r, c_: pl.BlockSpec((r, c_), lambda i: (0, 0), pipeline_mode=pl.Buffered(1))
    mixed = pl.pallas_call(
        functools.partial(_merge_kernel, tn),
        grid=(m // tm_merge,),
        in_specs=[
            pl.BlockSpec((tm_merge, d_inner), lambda i: (i, 0)),
            pl.BlockSpec((tm_merge, vd), lambda i: (i, 0)),
            pl.BlockSpec((tm_merge, 2 * d), lambda i: (i, g0)),
            resident(d_inner, d), resident(vd, d),
        ],
        out_specs=pl.BlockSpec((tm_merge, d), lambda i: (i, 0)),
        out_shape=jax.ShapeDtypeStruct((m, d), BF16),
        compiler_params=pltpu.CompilerParams(
            dimension_semantics=("arbitrary",), vmem_limit_bytes=VMEM_LIMIT_BYTES),
    )(y_ssd, y_gla, proj, prm["w_branch_ssd"], prm["w_branch_gla"])
    return pl.pallas_call(
        _post_kernel,
        grid=(m // tm_post,),
        in_specs=[
            pl.BlockSpec((tm_post, d), lambda i: (i, 0)),
            pl.BlockSpec((tm_post, d), lambda i: (i, 0)),
            pl.BlockSpec((d, d), lambda i: (0, 0)),
            pl.BlockSpec((1, d), lambda i: (0, 0)),
        ],
        out_specs=pl.BlockSpec((tm_post, d), lambda i: (i, 0)),
        out_shape=jax.ShapeDtypeStruct((m, d), F32),
        compiler_params=pltpu.CompilerParams(
            dimension_semantics=("arbitrary",), vmem_limit_bytes=VMEM_LIMIT_BYTES),
    )(mixed, x2, prm["w_out"], prm["norm_post_gain"])


def _regroup_kernel(tn, offsets, blk_ref, cls_ref, half_ref, a_ref, b_ref, o_ref):
    j = pl.program_id(0)
    scale = jnp.where(half_ref[j] == 1, 0.5, 1.0).astype(F32)
    for ci, off in enumerate(offsets):
        @pl.when(cls_ref[j] == ci)
        def _():
            x = a_ref[...] if off == 0 else jnp.concatenate([a_ref[off:tn, :], b_ref[0:off, :]], axis=0)
            o_ref[...] = (x.T * scale).astype(o_ref.dtype)


def _pick_rows_kernel(spans, *refs):
    for (start, size), src, dst in zip(spans, refs[:len(spans)], refs[len(spans):]):
        dst[...] = src[start % LANES:start % LANES + size, :]


def _pick_rows(a, *spans):
    d = a.shape[1]
    assert all(start % LANES + size <= LANES and size % SUBLANES == 0 for start, size in spans)
    return pl.pallas_call(
        functools.partial(_pick_rows_kernel, spans),
        grid=(1,),
        in_specs=[pl.BlockSpec((LANES, d), functools.partial(lambda blk, i: (blk, 0), start // LANES))
                  for start, _ in spans],
        out_specs=[pl.BlockSpec((size, d), lambda i: (0, 0)) for _, size in spans],
        out_shape=[jax.ShapeDtypeStruct((size, d), a.dtype) for _, size in spans],
    )(*([a] * len(spans)))


def _regroup_w_in(w_in_t, segments, tn):
    d = w_in_t.shape[1]
    offsets = sorted({src % tn for src, _, _ in segments})
    blk, cls, half = [], [], []
    for src, width, halve in segments:
        assert width % tn == 0 and src % tn < LANES and (src % tn) % SUBLANES == 0
        for i in range(width // tn):
            blk.append(src // tn + i)
            cls.append(offsets.index(src % tn))
            half.append(int(halve))
    n_blocks = len(blk)
    to_i32 = lambda v: jnp.asarray(np.array(v, np.int32))
    return pl.pallas_call(
        functools.partial(_regroup_kernel, tn, tuple(offsets)),
        grid_spec=pltpu.PrefetchScalarGridSpec(
            num_scalar_prefetch=3,
            grid=(n_blocks,),
            in_specs=[
                pl.BlockSpec((tn, d), lambda j, blk, cls, half: (blk[j], 0)),
                pl.BlockSpec((LANES, d), lambda j, blk, cls, half: ((blk[j] + 1) * (tn // LANES), 0)),
            ],
            out_specs=pl.BlockSpec((d, tn), lambda j, blk, cls, half: (0, j)),
        ),
        out_shape=jax.ShapeDtypeStruct((d, n_blocks * tn), BF16),
        compiler_params=pltpu.CompilerParams(
            dimension_semantics=("arbitrary",), vmem_limit_bytes=VMEM_LIMIT_BYTES),
    )(to_i32(blk), to_i32(cls), to_i32(half), w_in_t, w_in_t)
def _prepare(norm_pre_gain, w_in, conv_w, conv_b, dt_bias, a_log, d_skip, ssd_norm_gain, gla_gk_w, gla_gk_b,
             gla_norm_gain, w_branch_ssd, w_branch_gla, w_out, norm_post_gain):
    d = w_in.shape[0]
    n_heads = dt_bias.shape[0]
    d_inner = n_heads * SSD_HEAD_DIM
    bcw = 2 * SSD_GROUPS * SSD_D_STATE
    kd = gla_gk_w.shape[1]
    vd = w_branch_gla.shape[0]
    assert DT_COPIES * n_heads + GLA_GATE_RANK <= SMALL_W
    sizes = (d_inner, d_inner + bcw, n_heads, kd, kd, vd, vd, GLA_GATE_RANK, 2 * d)
    offs = np.concatenate([[0], np.cumsum(sizes)])
    z0, xbc0, dt0, q0, k0, v0, gb0, gkl0, mg0 = [int(o_) for o_ in offs[:9]]
    perm = np.concatenate([np.arange(0, n_heads, 2), np.arange(1, n_heads, 2)])
    w_in_t = w_in.T
    dt_rows, gk_rows = _pick_rows(w_in_t, (dt0, n_heads), (gkl0, GLA_GATE_RANK))
    dt_w = dt_rows[perm]
    gk0 = DT_COPIES * n_heads
    pad = SMALL_W - gk0 - GLA_GATE_RANK
    w_small = jnp.concatenate(
        [dt_w] * DT_COPIES + [gk_rows, jnp.zeros((pad, d), w_in.dtype)], axis=0).T.astype(BF16)
    half = 0.5
    pieces = [("xs", xbc0, d_inner, False), ("z", z0, d_inner, True), ("merge", mg0, 2 * d, True),
              ("bc", xbc0 + d_inner, bcw, False), ("v", v0, vd, False), ("g_b", gb0, vd, True),
              ("q", q0, kd, False), ("k", k0, kd, False)]
    col, o = {}, 0
    for name, _, width, _ in pieces:
        assert o % width == 0, name
        col[name] = o
        o += width
    w_main = _regroup_w_in(w_in_t, [p[1:] for p in pieces], REGROUP_TN)

    def small_row(v, fill):
        return jnp.concatenate([v[perm]] * DT_COPIES + [jnp.full((SMALL_W - gk0,), fill, v.dtype)])[None, :]

    expand = np.zeros((DT_COPIES * n_heads, d_inner), np.float32)
    for r in range(DT_COPIES):
        for c_ in range(n_heads):
            expand[r * n_heads + c_, perm[c_] * SSD_HEAD_DIM:(perm[c_] + 1) * SSD_HEAD_DIM] = 1.0
    tril = np.tril(np.ones((CHUNK, CHUNK), np.float32))
    rb = 2 * SUBLANES
    shift = np.concatenate([np.eye(CHUNK, k=-k, dtype=np.float32)[b * rb:(b + 1) * rb]
                            for b in range(CHUNK // rb) for k in range(1, SSD_CONV_W)], axis=0)
    gk_w_pad = jnp.zeros((SMALL_W, kd), F32).at[gk0:gk0 + GLA_GATE_RANK].set(gla_gk_w)
    gk_hi = gk_w_pad.astype(BF16)
    gk_lo = (gk_w_pad - gk_hi.astype(F32)).astype(BF16)
    rows = lambda a: jnp.broadcast_to(a, a.shape[:-2] + (2 * SUBLANES, a.shape[-1]))
    prm = dict(
        n_heads=n_heads, gla_key_dim=kd, gla_val_dim=vd,
        norm_pre_gain=norm_pre_gain[None, :], w_main=w_main, w_small=w_small,
        conv_w_x=rows(half * conv_w[:, None, :d_inner]), conv_w_bc=rows(half * conv_w[:, None, d_inner:]),
        conv_b_x=rows(half * conv_b[None, :d_inner]), conv_b_bc=rows(half * conv_b[None, d_inner:]),
        dt_bias3=small_row(dt_bias, 0.0), a_log3=small_row(a_log, 0.0),
        d_skip_e=jnp.repeat(d_skip, SSD_HEAD_DIM)[None, :], ssd_norm_gain=ssd_norm_gain[None, :],
        shift=jnp.asarray(shift, BF16), expand3=jnp.asarray(expand, BF16), tril3=jnp.asarray(np.concatenate([tril] * 3, axis=1), BF16),
        gk_w3=jnp.concatenate([gk_hi, gk_lo, gk_hi], axis=0), gk_b=gla_gk_b[None, :],
        gla_norm_gain=gla_norm_gain[None, :],
        w_branch_ssd=w_branch_ssd.astype(BF16), w_branch_gla=w_branch_gla.astype(BF16),
        w_out=w_out.astype(BF16), norm_post_gain=norm_post_gain[None, :],
    )
    return prm, col


def _time_block(t):
    tb = min(t, CHUNKS_PER_STEP * CHUNK)
    assert t % tb == 0
    return tb


def _row_tile(m, want):
    tm = min(m, want)
    assert m % tm == 0
    return tm


def _layer(x, conv_state, ssd_state, gla_state, prm, col):
    b, t, d = x.shape
    assert t % CHUNK == 0 and t >= SSD_CONV_W - 1
    m = b * t
    x2 = x.reshape(m, d)
    proj, small = _inproj(x2, prm["norm_pre_gain"], prm["w_main"], prm["w_small"], _row_tile(m, INPROJ_TM), INPROJ_TN)
    proj3 = proj.reshape(b, t, -1)
    small3 = small.reshape(b, t, -1)
    d_inner = prm["n_heads"] * SSD_HEAD_DIM
    bcw = 2 * SSD_GROUPS * SSD_D_STATE
    nc = SSD_CONV_W - 1
    new_conv = jnp.concatenate(
        [proj3[:, t - nc:, col["xs"]:col["xs"] + d_inner], proj3[:, t - nc:, col["bc"]:col["bc"] + bcw]],
        axis=-1).astype(x.dtype)
    y_ssd, new_ssd = _ssd(proj3, small3, col, prm, conv_state, ssd_state)
    y_gla, new_gla = _gla(proj3, small3, col, prm, gla_state)
    y = _outproj(y_ssd.reshape(m, -1), y_gla.reshape(m, -1), proj, col, x2, prm,
                 _row_tile(m, MERGE_TM), MERGE_TN, _row_tile(m, POST_TM))
    return y.reshape(b, t, d), new_conv, new_ssd, new_gla


def kernel(x_prompt, x_sample, state_conv_ssd, state_ssd, state_gla, norm_pre_gain, w_in, conv_w, conv_b, dt_bias,
           a_log, d_skip, ssd_norm_gain, gla_gk_w, gla_gk_b, gla_norm_gain, w_branch_ssd, w_branch_gla, w_out,
           norm_post_gain):
    depth = w_in.shape[0]
    yp, ys = x_prompt, x_sample
    outs = [[] for _ in range(6)]
    for layer in range(depth):
        prm, col = _prepare(norm_pre_gain[layer], w_in[layer], conv_w[layer], conv_b[layer], dt_bias[layer],
                            a_log[layer], d_skip[layer], ssd_norm_gain[layer], gla_gk_w[layer], gla_gk_b[layer],
                            gla_norm_gain[layer], w_branch_ssd[layer], w_branch_gla[layer], w_out[layer],
                            norm_post_gain[layer])
        yp, c_p, s_p, g_p = _layer(yp, None, None, None, prm, col)
        ys, c_s, s_s, g_s = _layer(ys, state_conv_ssd[layer], state_ssd[layer], state_gla[layer], prm, col)
        for lst, val in zip(outs, (c_p, s_p, g_p, c_s, s_s, g_s)):
            lst.append(val)
    return (yp, ys) + tuple(jnp.stack(o) for o in outs)
```

```python
import functools
import math

import numpy as np
import jax
import jax.numpy as jnp
from jax import lax
from jax.experimental import pallas as pl
from jax.experimental.pallas import tpu as pltpu

F32 = jnp.float32
BF16 = jnp.bfloat16

LANES = 128
SUBLANES = 8
VMEM_LIMIT_BYTES = 56 * 1024 * 1024

CHUNK = 64
SSD_HEAD_DIM = 64
SSD_GROUPS = 8
SSD_D_STATE = 128
SSD_CONV_W = 4
GLA_HEADS = 4
GLA_GATE_RANK = 16
GLA_GATE_NORMALIZER = 16.0
NORM_EPS = 1e-6
SSD_NORM_EPS = 1e-5
SUB = 8

CHUNKS_PER_STEP = 8
CONV_LANES = 2 * LANES
INPROJ_TM, INPROJ_TN = 1024, 2048
MERGE_TM, MERGE_TN = 256, 1024
POST_TM = 512
REGROUP_TN = 1024
SMALL_W = 256
DT_COPIES = 2
LOG2E = math.log2(math.e)


def _dot(a, b, dims=(((1,), (0,)), ((), ()))):
    return lax.dot_general(a, b, dims, preferred_element_type=F32)


_NT = (((1,), (1,)), ((), ()))
_TN = (((0,), (0,)), ((), ()))


def _sigmoid_half(h):
    return 0.5 * jnp.tanh(h) + 0.5


def _silu_half(h):
    return h * jnp.tanh(h) + h


def _softplus(x):
    return jnp.maximum(x, 0.0) + jnp.log(1.0 + jnp.exp(-jnp.abs(x)))


def _residuals(x):
    r1 = x - x.astype(BF16).astype(F32)
    r2 = r1 - r1.astype(BF16).astype(F32)
    return x, r1, r2


def _cumsum_rows(tril3_ref, x):
    parts = [p.astype(BF16) for p in _residuals(x)]
    return _dot(tril3_ref[...], jnp.concatenate(parts, axis=0))


def _inproj_kernel(x_ref, g_ref, w_ref, ws_ref, o_ref, os_ref, h_ref):
    @pl.when(pl.program_id(1) == 0)
    def _():
        x = x_ref[...]
        ms = jnp.mean(x * x, axis=-1, keepdims=True)
        h_ref[...] = (x * lax.rsqrt(ms + NORM_EPS) * g_ref[...]).astype(BF16)
        os_ref[...] = _dot(h_ref[...], ws_ref[...])

    o_ref[...] = _dot(h_ref[...], w_ref[...]).astype(o_ref.dtype)


def _inproj(x2, gain, w_main, w_small, tm, tn):
    m, d = x2.shape
    n = w_main.shape[1]
    ns = w_small.shape[1]
    return pl.pallas_call(
        _inproj_kernel,
        grid=(m // tm, n // tn),
        in_specs=[
            pl.BlockSpec((tm, d), lambda i, j: (i, 0)),
            pl.BlockSpec((1, d), lambda i, j: (0, 0)),
            pl.BlockSpec((d, tn), lambda i, j: (0, j)),
            pl.BlockSpec((d, ns), lambda i, j: (0, 0)),
        ],
        out_specs=[
            pl.BlockSpec((tm, tn), lambda i, j: (i, j)),
            pl.BlockSpec((tm, ns), lambda i, j: (i, 0)),
        ],
        out_shape=[jax.ShapeDtypeStruct((m, n), BF16), jax.ShapeDtypeStruct((m, ns), F32)],
        scratch_shapes=[pltpu.VMEM((tm, d), BF16)],
        compiler_params=pltpu.CompilerParams(
            dimension_semantics=("arbitrary", "arbitrary"), vmem_limit_bytes=VMEM_LIMIT_BYTES),
    )(x2, gain, w_main, w_small)


def _ssd_kernel(has_state, n_heads, n_sub, *refs):
    if has_state:
        (xs_ref, z_ref, bc_ref, sm_ref, cwx_ref, cbx_ref, cwb_ref, cbb_ref, dtb_ref, alog_ref, dsk_ref,
         gain_ref, exp_ref, tril_ref, shift_ref, csx_ref, csb_ref, st_ref,
         y_ref, so_ref, tailx, tailb, srow, dtrow, xs_s, xs16, bc16, bcast, yz_s, *stT) = refs
    else:
        (xs_ref, z_ref, bc_ref, sm_ref, cwx_ref, cbx_ref, cwb_ref, cbb_ref, dtb_ref, alog_ref, dsk_ref,
         gain_ref, exp_ref, tril_ref, shift_ref,
         y_ref, so_ref, tailx, tailb, srow, dtrow, xs_s, xs16, bc16, bcast, yz_s, *stT) = refs
    c = pl.program_id(1)
    n_chunks = pl.num_programs(1)
    L = CHUNK
    P = SSD_HEAD_DIM
    N = SSD_D_STATE
    G = SSD_GROUPS
    gw = (n_heads // G) * P
    d_inner = n_heads * P
    nc = SSD_CONV_W - 1

    @pl.when(c == 0)
    def _():
        tailx[...] = jnp.zeros_like(tailx)
        tailb[...] = jnp.zeros_like(tailb)
        if has_state:
            tailx[SUBLANES - nc:SUBLANES, :] = csx_ref[0]
            tailb[SUBLANES - nc:SUBLANES, :] = csb_ref[0]
            for g in range(G):
                stT[g][...] = st_ref[0, g * gw:(g + 1) * gw, :].T
        else:
            for g in range(G):
                stT[g][...] = jnp.zeros_like(stT[g])

    def chunk(si, carry):
        c0 = pl.multiple_of(si * L, L)
        cw = CONV_LANES
        rb = 2 * SUBLANES
        row8 = lax.broadcasted_iota(jnp.int32, (SUBLANES, cw), 0)

        def conv(x_ref, tail, w_ref, b_ref, width, emit):
            for cb in range(width // cw):
                cs = slice(cb * cw, (cb + 1) * cw)
                sh = _dot(shift_ref[...], x_ref[0, pl.ds(c0, L), cs])
                prev = tail[:, cs]
                fix = jnp.zeros((SUBLANES, cw), F32)
                for k in range(1, SSD_CONV_W):
                    fix = fix + jnp.where(row8 < k, pltpu.roll(prev, k, 0), 0.0) * w_ref[nc - k, 0:SUBLANES, cs]
                for blk in range(L // rb):
                    acc = sh[blk * SSD_CONV_W * rb:(blk * SSD_CONV_W + 1) * rb, :] * w_ref[nc, :, cs] + b_ref[:, cs]
                    for k in range(1, SSD_CONV_W):
                        s0 = (blk * SSD_CONV_W + k) * rb
                        acc = acc + sh[s0:s0 + rb, :] * w_ref[nc - k, :, cs]
                    if blk == 0:
                        acc = jnp.concatenate([acc[0:SUBLANES, :] + fix, acc[SUBLANES:, :]], axis=0)
                    emit(blk * rb, rb, cs, _silu_half(acc))

        def emit_x(r0, nr, cs, act):
            xs_s[r0:r0 + nr, cs] = act
            xs16[r0:r0 + nr, cs] = act.astype(BF16)

        def emit_bc(r0, nr, cs, act):
            bc16[r0:r0 + nr, cs] = act.astype(BF16)

        conv(xs_ref, tailx, cwx_ref, cbx_ref, d_inner, emit_x)
        conv(bc_ref, tailb, cwb_ref, cbb_ref, 2 * G * N, emit_bc)

        dt3 = _softplus(sm_ref[0, pl.ds(c0, L), :] + dtb_ref[...])
        s3 = _cumsum_rows(tril_ref, dt3 * (-jnp.exp(alog_ref[...])))
        w3 = dt3 * jnp.exp(s3[L - 1:L, :] - s3)
        lane_s = lax.broadcasted_iota(jnp.int32, (2 * L, SMALL_W), 1)
        x0, x1, _ = _residuals(jnp.concatenate([s3, w3], axis=0))
        split = jnp.where(lane_s < n_heads, x0, x1)[:, 0:DT_COPIES * n_heads].astype(BF16)
        bcast[...] = _dot(split, exp_ref[...])
        hh = n_heads // 2
        s_t = s3[:, 0:2 * L].T
        dt_t = dt3[:, 0:2 * L].T
        srow[...] = jnp.concatenate([s_t[0:hh], s_t[hh:2 * hh]], axis=1)
        dtrow[...] = jnp.concatenate([dt_t[0:hh], dt_t[hh:2 * hh]], axis=1)

        row = lax.broadcasted_iota(jnp.int32, (L, 2 * P), 0)
        lane = lax.broadcasted_iota(jnp.int32, (L, 2 * P), 1)
        causal2 = (lane % P) <= row
        left = lane < P
        ppg = gw // (2 * P)

        for g in range(G):
            gs = slice(g * gw, (g + 1) * gw)
            bm16 = bc16[:, g * N:(g + 1) * N]
            cm16 = bc16[:, (G + g) * N:(G + g + 1) * N]
            cb2 = _dot(cm16, jnp.concatenate([bm16, bm16], axis=0), _NT)
            ssq = jnp.zeros((L, 1), F32)
            for pp in range(ppg):
                p = g * ppg + pp
                sl = slice(p * 2 * P, (p + 1) * 2 * P)
                se = bcast[0:L, sl]
                dec = jnp.exp(jnp.where(causal2, se - srow[p:p + 1, :], -jnp.inf))
                m = (cb2 * dec * dtrow[p:p + 1, :]).astype(BF16)
                xp16 = xs16[:, sl]
                zero = jnp.zeros_like(xp16)
                rhs = jnp.concatenate([jnp.where(left, xp16, zero), jnp.where(left, zero, xp16)], axis=0)
                y = _dot(m, rhs)
                y = y + _dot(cm16, stT[g][:, pp * 2 * P:(pp + 1) * 2 * P].astype(BF16)) * jnp.exp(se) + dsk_ref[:, sl] * xs_s[:, sl]
                yz = y * _silu_half(z_ref[0, pl.ds(c0, L), sl].astype(F32))
                ssq = ssq + jnp.sum(yz * yz, axis=-1, keepdims=True)
                yz_s[:, pp * 2 * P:(pp + 1) * 2 * P] = yz
            s_last = bcast[L - 1:L, gs]
            xdt_end = xs_s[:, gs] * bcast[L:2 * L, gs]
            upd = _dot(bm16, xdt_end.astype(BF16), _TN)
            stT[g][...] = stT[g][...] * jnp.exp(s_last) + upd

            scale = lax.rsqrt(ssq * (1.0 / gw) + SSD_NORM_EPS)
            y_ref[0, pl.ds(c0, L), gs] = (yz_s[...] * scale * gain_ref[:, gs]).astype(y_ref.dtype)

        tailx[...] = xs_ref[0, pl.ds(c0 + L - 2 * SUBLANES, 2 * SUBLANES), :].astype(F32)[SUBLANES:, :]
        tailb[...] = bc_ref[0, pl.ds(c0 + L - 2 * SUBLANES, 2 * SUBLANES), :].astype(F32)[SUBLANES:, :]

        return carry

    lax.fori_loop(0, n_sub, chunk, 0)

    @pl.when(c == n_chunks - 1)
    def _():
        for g in range(G):
            so_ref[0, g * gw:(g + 1) * gw, :] = stT[g][...].T


def _ssd(proj3, small3, col, prm, conv_state, ssd_state):
    b, t, _ = proj3.shape
    n_heads = prm["n_heads"]
    d_inner = n_heads * SSD_HEAD_DIM
    bcw = 2 * SSD_GROUPS * SSD_D_STATE
    has_state = ssd_state is not None
    L = CHUNK
    tb = _time_block(t)
    const = lambda *shape: pl.BlockSpec(shape, lambda i, j: (0,) * len(shape))
    in_specs = [
        pl.BlockSpec((1, tb, d_inner), lambda i, j: (i, j, col["xs"] // d_inner)),
        pl.BlockSpec((1, tb, d_inner), lambda i, j: (i, j, col["z"] // d_inner)),
        pl.BlockSpec((1, tb, bcw), lambda i, j: (i, j, col["bc"] // bcw)),
        pl.BlockSpec((1, tb, SMALL_W), lambda i, j: (i, j, 0)),
        const(SSD_CONV_W, 2 * SUBLANES, d_inner), const(2 * SUBLANES, d_inner),
        const(SSD_CONV_W, 2 * SUBLANES, bcw), const(2 * SUBLANES, bcw),
        const(1, SMALL_W), const(1, SMALL_W), const(1, d_inner), const(1, d_inner),
        const(DT_COPIES * n_heads, d_inner), const(L, 3 * L), const(SSD_CONV_W * L, L),
    ]
    args = [proj3, proj3, proj3, small3, prm["conv_w_x"], prm["conv_b_x"], prm["conv_w_bc"], prm["conv_b_bc"],
            prm["dt_bias3"], prm["a_log3"], prm["d_skip_e"], prm["ssd_norm_gain"], prm["expand3"], prm["tril3"],
            prm["shift"]]
    if has_state:
        nc = SSD_CONV_W - 1
        in_specs += [
            pl.BlockSpec((1, nc, d_inner), lambda i, j: (i, 0, 0)),
            pl.BlockSpec((1, nc, bcw), lambda i, j: (i, 0, 0)),
            pl.BlockSpec((1, d_inner, SSD_D_STATE), lambda i, j: (i, 0, 0)),
        ]
        args += [conv_state[:, :, :d_inner], conv_state[:, :, d_inner:],
                 ssd_state.reshape(b, d_inner, SSD_D_STATE)]
    y, st = pl.pallas_call(
        functools.partial(_ssd_kernel, has_state, n_heads, tb // L),
        grid=(b, t // tb),
        in_specs=in_specs,
        out_specs=[
            pl.BlockSpec((1, tb, d_inner), lambda i, j: (i, j, 0)),
            pl.BlockSpec((1, d_inner, SSD_D_STATE), lambda i, j: (i, 0, 0)),
        ],
        out_shape=[
            jax.ShapeDtypeStruct((b, t, d_inner), BF16),
            jax.ShapeDtypeStruct((b, d_inner, SSD_D_STATE), F32),
        ],
        scratch_shapes=[
            pltpu.VMEM((SUBLANES, d_inner), F32),
            pltpu.VMEM((SUBLANES, bcw), F32),
            pltpu.VMEM((n_heads // 2, 2 * L), F32),
            pltpu.VMEM((n_heads // 2, 2 * L), F32),
            pltpu.VMEM((L, d_inner), F32),
            pltpu.VMEM((L, d_inner), BF16),
            pltpu.VMEM((L, bcw), BF16),
            pltpu.VMEM((2 * L, d_inner), F32),
            pltpu.VMEM((L, d_inner // SSD_GROUPS), F32),
        ] + [pltpu.VMEM((SSD_D_STATE, d_inner // SSD_GROUPS), F32) for _ in range(SSD_GROUPS)],
        compiler_params=pltpu.CompilerParams(
            dimension_semantics=("arbitrary", "arbitrary"), vmem_limit_bytes=VMEM_LIMIT_BYTES),
    )(*args)
    return y, st.reshape(b, n_heads, SSD_HEAD_DIM, SSD_D_STATE)


def _gla_kernel(has_state, n_sub, *refs):
    if has_state:
        (q_ref, k_ref, v_ref, gb_ref, sm_ref, wgk_ref, bgk_ref, gain_ref, tril_ref, st_ref,
         o_ref, so_ref, bcs, ks, qs, a_s, *stT) = refs
    else:
        (q_ref, k_ref, v_ref, gb_ref, sm_ref, wgk_ref, bgk_ref, gain_ref, tril_ref,
         o_ref, so_ref, bcs, ks, qs, a_s, *stT) = refs
    c = pl.program_id(1)
    n_chunks = pl.num_programs(1)
    L = CHUNK
    H = GLA_HEADS
    kd = q_ref.shape[2]
    vd = v_ref.shape[2]
    hk = kd // H
    hv = vd // H
    nb = L // SUB

    @pl.when(c == 0)
    def _():
        if has_state:
            for h in range(H):
                stT[h][...] = st_ref[0, h].T
        else:
            for h in range(H):
                stT[h][...] = jnp.zeros_like(stT[h])

    def chunk(si, carry):
        c0 = pl.multiple_of(si * L, L)
        sm = sm_ref[0, pl.ds(c0, L), :]
        sm_hi = sm.astype(BF16)
        sm_lo = (sm - sm_hi.astype(F32)).astype(BF16)
        x = _dot(jnp.concatenate([sm_hi, sm_hi, sm_lo], axis=1), wgk_ref[...]) + bgk_ref[...]
        glog2 = (jnp.minimum(x, 0.0) - jnp.log(1.0 + jnp.exp(-jnp.abs(x)))) * (LOG2E / GLA_GATE_NORMALIZER)
        bcs[...] = _cumsum_rows(tril_ref, glog2)
        qs[...] = q_ref[0, pl.ds(c0, L), :].astype(F32) * (hk ** -0.5)
        ks[...] = k_ref[0, pl.ds(c0, L), :].astype(F32)

        nt = SUB // SUBLANES
        lane = lax.broadcasted_iota(jnp.int32, (SUBLANES, L), 1)
        row = lax.broadcasted_iota(jnp.int32, (SUBLANES, L), 0)
        for blk in range(nb):
            r0 = blk * SUB
            off = [None] * H
            if blk > 0:
                ref_row = bcs[r0 - 1:r0, :]
                qd = (qs[r0:r0 + SUB, :] * jnp.exp2(bcs[r0:r0 + SUB, :] - ref_row)).astype(BF16)
                kt = ks[0:r0, :] * jnp.exp2(ref_row - bcs[0:r0, :])
                kt = jnp.concatenate([kt, jnp.zeros((L - r0, kd), F32)], axis=0).astype(BF16)
                for h in range(H):
                    off[h] = _dot(qd[:, h * hk:(h + 1) * hk], kt[:, h * hk:(h + 1) * hk], _NT)
            diag = [[jnp.zeros((SUBLANES, L), F32) for _ in range(nt)] for _ in range(H)]
            for jj in range(SUB):
                j = r0 + jj
                for ti in range(jj // SUBLANES, nt):
                    t0 = r0 + ti * SUBLANES
                    keep = (lane == j) & (row + t0 >= j)
                    for h in range(H):
                        tot = None
                        for lt in range(hk // LANES):
                            ls = slice(h * hk + lt * LANES, h * hk + (lt + 1) * LANES)
                            e = jnp.exp2(bcs[t0:t0 + SUBLANES, ls] - bcs[j:j + 1, ls])
                            pr = qs[t0:t0 + SUBLANES, ls] * e * ks[j:j + 1, ls]
                            tot = pr if tot is None else tot + pr
                        col = jnp.sum(tot, axis=-1, keepdims=True)
                        diag[h][ti] = jnp.where(keep, col, diag[h][ti])
            for h in range(H):
                d_h = jnp.concatenate(diag[h], axis=0)
                a_s[h, r0:r0 + SUB, :] = d_h if off[h] is None else off[h] + d_h

        for h in range(H):
            hs = slice(h * hk, (h + 1) * hk)
            bc_h = bcs[:, hs]
            bc_last = bcs[L - 1:L, hs]
            qe = (qs[:, hs] * jnp.exp2(bc_h)).astype(BF16)
            k_end = (ks[:, hs] * jnp.exp2(bc_last - bc_h)).astype(BF16)
            v_h = v_ref[0, pl.ds(c0, L), h * hv:(h + 1) * hv]
            o = _dot(a_s[h].astype(BF16), v_h) + _dot(qe, stT[h][...].astype(BF16), _NT)
            stT[h][...] = stT[h][...] * jnp.exp2(bc_last) + _dot(v_h, k_end, _TN)
            ms = jnp.mean(o * o, axis=-1, keepdims=True)
            on = o * lax.rsqrt(ms + NORM_EPS) * gain_ref[...]
            gate = _silu_half(gb_ref[0, pl.ds(c0, L), h * hv:(h + 1) * hv].astype(F32))
            o_ref[0, pl.ds(c0, L), h * hv:(h + 1) * hv] = (on * gate).astype(o_ref.dtype)

        return carry

    lax.fori_loop(0, n_sub, chunk, 0)

    @pl.when(c == n_chunks - 1)
    def _():
        for h in range(H):
            so_ref[0, h] = stT[h][...].T


def _gla(proj3, small3, col, prm, gla_state):
    b, t, _ = proj3.shape
    kd = prm["gla_key_dim"]
    vd = prm["gla_val_dim"]
    H = GLA_HEADS
    hk, hv = kd // H, vd // H
    has_state = gla_state is not None
    L = CHUNK
    tb = _time_block(t)
    const = lambda *shape: pl.BlockSpec(shape, lambda i, j: (0,) * len(shape))
    in_specs = [
        pl.BlockSpec((1, tb, kd), lambda i, j: (i, j, col["q"] // kd)),
        pl.BlockSpec((1, tb, kd), lambda i, j: (i, j, col["k"] // kd)),
        pl.BlockSpec((1, tb, vd), lambda i, j: (i, j, col["v"] // vd)),
        pl.BlockSpec((1, tb, vd), lambda i, j: (i, j, col["g_b"] // vd)),
        pl.BlockSpec((1, tb, SMALL_W), lambda i, j: (i, j, 0)),
        const(3 * SMALL_W, kd), const(1, kd), const(1, hv), const(L, 3 * L),
    ]
    args = [proj3, proj3, proj3, proj3, small3, prm["gk_w3"], prm["gk_b"], prm["gla_norm_gain"], prm["tril3"]]
    if has_state:
        in_specs.append(pl.BlockSpec((1, H, hk, hv), lambda i, j: (i, 0, 0, 0)))
        args.append(gla_state)
    return pl.pallas_call(
        functools.partial(_gla_kernel, has_state, tb // L),
        grid=(b, t // tb),
        in_specs=in_specs,
        out_specs=[
            pl.BlockSpec((1, tb, vd), lambda i, j: (i, j, 0)),
            pl.BlockSpec((1, H, hk, hv), lambda i, j: (i, 0, 0, 0)),
        ],
        out_shape=[
            jax.ShapeDtypeStruct((b, t, vd), BF16),
            jax.ShapeDtypeStruct((b, H, hk, hv), F32),
        ],
        scratch_shapes=[
            pltpu.VMEM((L, kd), F32),
            pltpu.VMEM((L, kd), F32),
            pltpu.VMEM((L, kd), F32),
            pltpu.VMEM((H, L, L), F32),
        ] + [pltpu.VMEM((hv, hk), F32) for _ in range(H)],
        compiler_params=pltpu.CompilerParams(
            dimension_semantics=("arbitrary", "arbitrary"), vmem_limit_bytes=VMEM_LIMIT_BYTES),
    )(*args)


def _merge_kernel(tn, ys_ref, yg_ref, g_ref, wbs_ref, wbg_ref, o_ref):
    d = o_ref.shape[1]
    for n in range(d // tn):
        ns = slice(n * tn, (n + 1) * tn)
        a = _dot(ys_ref[...], wbs_ref[:, ns])
        b = _dot(yg_ref[...], wbg_ref[:, ns])
        gate_s = _sigmoid_half(g_ref[:, ns].astype(F32))
        gate_g = _sigmoid_half(g_ref[:, d + n * tn:d + (n + 1) * tn].astype(F32))
        o_ref[:, ns] = (gate_s * a + gate_g * b).astype(o_ref.dtype)


def _post_kernel(mx_ref, x_ref, wo_ref, g_ref, o_ref):
    cc = _dot(mx_ref[...], wo_ref[...])
    ms = jnp.mean(cc * cc, axis=-1, keepdims=True)
    o_ref[...] = x_ref[...] + cc * lax.rsqrt(ms + NORM_EPS) * g_ref[...]


def _outproj(y_ssd, y_gla, proj, col, x2, prm, tm_merge, tn, tm_post):
    m, d = x2.shape
    d_inner = y_ssd.shape[1]
    vd = y_gla.shape[1]
    g0 = col["merge"] // (2 * d)
    resident = lambda r, c_: pl.BlockSpec((r, c_), lambda i: (0, 0), pipeline_mode=pl.Buffered(1))
    mixed = pl.pallas_call(
        functools.partial(_merge_kernel, tn),
        grid=(m // tm_merge,),
        in_specs=[
            pl.BlockSpec((tm_merge, d_inner), lambda i: (i, 0)),
            pl.BlockSpec((tm_merge, vd), lambda i: (i, 0)),
            pl.BlockSpec((tm_merge, 2 * d), lambda i: (i, g0)),
            resident(d_inner, d), resident(vd, d),
        ],
        out_specs=pl.BlockSpec((tm_merge, d), lambda i: (i, 0)),
        out_shape=jax.ShapeDtypeStruct((m, d), BF16),
        compiler_params=pltpu.CompilerParams(
            dimension_semantics=("arbitrary",), vmem_limit_bytes=VMEM_LIMIT_BYTES),
    )(y_ssd, y_gla, proj, prm["w_branch_ssd"], prm["w_branch_gla"])
    return pl.pallas_call(
        _post_kernel,
        grid=(m // tm_post,),
        in_specs=[
            pl.BlockSpec((tm_post, d), lambda i: (i, 0)),
            pl.BlockSpec((tm_post, d), lambda i: (i, 0)),
            pl.BlockSpec((d, d), lambda i: (0, 0)),
            pl.BlockSpec((1, d), lambda i: (0, 0)),
        ],
        out_specs=pl.BlockSpec((tm_post, d), lambda i: (i, 0)),
        out_shape=jax.ShapeDtypeStruct((m, d), F32),
        compiler_params=pltpu.CompilerParams(
            dimension_semantics=("arbitrary",), vmem_limit_bytes=VMEM_LIMIT_BYTES),
    )(mixed, x2, prm["w_out"], prm["norm_post_gain"])


def _regroup_kernel(tn, offsets, blk_ref, cls_ref, half_ref, a_ref, b_ref, o_ref):
    j = pl.program_id(0)
    scale = jnp.where(half_ref[j] == 1, 0.5, 1.0).astype(F32)
    for ci, off in enumerate(offsets):
        @pl.when(cls_ref[j] == ci)
        def _():
            x = a_ref[...] if off == 0 else jnp.concatenate([a_ref[off:tn, :], b_ref[0:off, :]], axis=0)
            o_ref[...] = (x.T * scale).astype(o_ref.dtype)


def _pick_rows_kernel(spans, *refs):
    for (start, size), src, dst in zip(spans, refs[:len(spans)], refs[len(spans):]):
        dst[...] = src[start % LANES:start % LANES + size, :]


def _pick_rows(a, *spans):
    d = a.shape[1]
    assert all(start % LANES + size <= LANES and size % SUBLANES == 0 for start, size in spans)
    return pl.pallas_call(
        functools.partial(_pick_rows_kernel, spans),
        grid=(1,),
        in_specs=[pl.BlockSpec((LANES, d), functools.partial(lambda blk, i: (blk, 0), start // LANES))
                  for start, _ in spans],
        out_specs=[pl.BlockSpec((size, d), lambda i: (0, 0)) for _, size in spans],
        out_shape=[jax.ShapeDtypeStruct((size, d), a.dtype) for _, size in spans],
    )(*([a] * len(spans)))


def _regroup_w_in(w_in_t, segments, tn):
    d = w_in_t.shape[1]
    offsets = sorted({src % tn for src, _, _ in segments})
    blk, cls, half = [], [], []
    for src, width, halve in segments:
        assert width % tn == 0 and src % tn < LANES and (src % tn) % SUBLANES == 0
        for i in range(width // tn):
            blk.append(src // tn + i)
            cls.append(offsets.index(src % tn))
            half.append(int(halve))
    n_blocks = len(blk)
    to_i32 = lambda v: jnp.asarray(np.array(v, np.int32))
    return pl.pallas_call(
        functools.partial(_regroup_kernel, tn, tuple(offsets)),
        grid_spec=pltpu.PrefetchScalarGridSpec(
            num_scalar_prefetch=3,
            grid=(n_blocks,),
            in_specs=[
                pl.BlockSpec((tn, d), lambda j, blk, cls, half: (blk[j], 0)),
                pl.BlockSpec((LANES, d), lambda j, blk, cls, half: ((blk[j] + 1) * (tn // LANES), 0)),
            ],
            out_specs=pl.BlockSpec((d, tn), lambda j, blk, cls, half: (0, j)),
        ),
        out_shape=jax.ShapeDtypeStruct((d, n_blocks * tn), BF16),
        compiler_params=pltpu.CompilerParams(
            dimension_semantics=("arbitrary",), vmem_limit_bytes=VMEM_LIMIT_BYTES),
    )(to_i32(blk), to_i32(cls), to_i32(half), w_in_t, w_in_t)
def _prepare(norm_pre_gain, w_in, conv_w, conv_b, dt_bias, a_log, d_skip, ssd_norm_gain, gla_gk_w, gla_gk_b,
             gla_norm_gain, w_branch_ssd, w_branch_gla, w_out, norm_post_gain):
    d = w_in.shape[0]
    n_heads = dt_bias.shape[0]
    d_inner = n_heads * SSD_HEAD_DIM
    bcw = 2 * SSD_GROUPS * SSD_D_STATE
    kd = gla_gk_w.shape[1]
    vd = w_branch_gla.shape[0]
    assert DT_COPIES * n_heads + GLA_GATE_RANK <= SMALL_W
    sizes = (d_inner, d_inner + bcw, n_heads, kd, kd, vd, vd, GLA_GATE_RANK, 2 * d)
    offs = np.concatenate([[0], np.cumsum(sizes)])
    z0, xbc0, dt0, q0, k0, v0, gb0, gkl0, mg0 = [int(o_) for o_ in offs[:9]]
    perm = np.concatenate([np.arange(0, n_heads, 2), np.arange(1, n_heads, 2)])
    w_in_t = w_in.T
    dt_rows, gk_rows = _pick_rows(w_in_t, (dt0, n_heads), (gkl0, GLA_GATE_RANK))
    dt_w = dt_rows[perm]
    gk0 = DT_COPIES * n_heads
    pad = SMALL_W - gk0 - GLA_GATE_RANK
    w_small = jnp.concatenate(
        [dt_w] * DT_COPIES + [gk_rows, jnp.zeros((pad, d), w_in.dtype)], axis=0).T.astype(BF16)
    half = 0.5
    pieces = [("xs", xbc0, d_inner, False), ("z", z0, d_inner, True), ("merge", mg0, 2 * d, True),
              ("bc", xbc0 + d_inner, bcw, False), ("v", v0, vd, False), ("g_b", gb0, vd, True),
              ("q", q0, kd, False), ("k", k0, kd, False)]
    col, o = {}, 0
    for name, _, width, _ in pieces:
        assert o % width == 0, name
        col[name] = o
        o += width
    w_main = _regroup_w_in(w_in_t, [p[1:] for p in pieces], REGROUP_TN)

    def small_row(v, fill):
        return jnp.concatenate([v[perm]] * DT_COPIES + [jnp.full((SMALL_W - gk0,), fill, v.dtype)])[None, :]

    expand = np.zeros((DT_COPIES * n_heads, d_inner), np.float32)
    for r in range(DT_COPIES):
        for c_ in range(n_heads):
            expand[r * n_heads + c_, perm[c_] * SSD_HEAD_DIM:(perm[c_] + 1) * SSD_HEAD_DIM] = 1.0
    tril = np.tril(np.ones((CHUNK, CHUNK), np.float32))
    rb = 2 * SUBLANES
    shift = np.concatenate([np.eye(CHUNK, k=-k, dtype=np.float32)[b * rb:(b + 1) * rb]
                            for b in range(CHUNK // rb) for k in range(SSD_CONV_W)], axis=0)
    gk_w_pad = jnp.zeros((SMALL_W, kd), F32).at[gk0:gk0 + GLA_GATE_RANK].set(gla_gk_w)
    gk_hi = gk_w_pad.astype(BF16)
    gk_lo = (gk_w_pad - gk_hi.astype(F32)).astype(BF16)
    rows = lambda a: jnp.broadcast_to(a, a.shape[:-2] + (2 * SUBLANES, a.shape[-1]))
    prm = dict(
        n_heads=n_heads, gla_key_dim=kd, gla_val_dim=vd,
        norm_pre_gain=norm_pre_gain[None, :], w_main=w_main, w_small=w_small,
        conv_w_x=rows(half * conv_w[:, None, :d_inner]), conv_w_bc=rows(half * conv_w[:, None, d_inner:]),
        conv_b_x=rows(half * conv_b[None, :d_inner]), conv_b_bc=rows(half * conv_b[None, d_inner:]),
        dt_bias3=small_row(dt_bias, 0.0), a_log3=small_row(a_log, 0.0),
        d_skip_e=jnp.repeat(d_skip, SSD_HEAD_DIM)[None, :], ssd_norm_gain=ssd_norm_gain[None, :],
        shift=jnp.asarray(shift, BF16), expand3=jnp.asarray(expand, BF16), tril3=jnp.asarray(np.concatenate([tril] * 3, axis=1), BF16),
        gk_w3=jnp.concatenate([gk_hi, gk_lo, gk_hi], axis=0), gk_b=gla_gk_b[None, :],
        gla_norm_gain=gla_norm_gain[None, :],
        w_branch_ssd=w_branch_ssd.astype(BF16), w_branch_gla=w_branch_gla.astype(BF16),
        w_out=w_out.astype(BF16), norm_post_gain=norm_post_gain[None, :],
    )
    return prm, col


def _time_block(t):
    tb = min(t, CHUNKS_PER_STEP * CHUNK)
    assert t % tb == 0
    return tb


def _row_tile(m, want):
    tm = min(m, want)
    assert m % tm == 0
    return tm


def _layer(x, conv_state, ssd_state, gla_state, prm, col):
    b, t, d = x.shape
    assert t % CHUNK == 0 and t >= SSD_CONV_W - 1
    m = b * t
    x2 = x.reshape(m, d)
    proj, small = _inproj(x2, prm["norm_pre_gain"], prm["w_main"], prm["w_small"], _row_tile(m, INPROJ_TM), INPROJ_TN)
    proj3 = proj.reshape(b, t, -1)
    small3 = small.reshape(b, t, -1)
    d_inner = prm["n_heads"] * SSD_HEAD_DIM
    bcw = 2 * SSD_GROUPS * SSD_D_STATE
    nc = SSD_CONV_W - 1
    new_conv = jnp.concatenate(
        [proj3[:, t - nc:, col["xs"]:col["xs"] + d_inner], proj3[:, t - nc:, col["bc"]:col["bc"] + bcw]],
        axis=-1).astype(x.dtype)
    y_ssd, new_ssd = _ssd(proj3, small3, col, prm, conv_state, ssd_state)
    y_gla, new_gla = _gla(proj3, small3, col, prm, gla_state)
    y = _outproj(y_ssd.reshape(m, -1), y_gla.reshape(m, -1), proj, col, x2, prm,
                 _row_tile(m, MERGE_TM), MERGE_TN, _row_tile(m, POST_TM))
    return y.reshape(b, t, d), new_conv, new_ssd, new_gla


def kernel(x_prompt, x_sample, state_conv_ssd, state_ssd, state_gla, norm_pre_gain, w_in, conv_w, conv_b, dt_bias,
           a_log, d_skip, ssd_norm_gain, gla_gk_w, gla_gk_b, gla_norm_gain, w_branch_ssd, w_branch_gla, w_out,
           norm_post_gain):
    depth = w_in.shape[0]
    yp, ys = x_prompt, x_sample
    outs = [[] for _ in range(6)]
    for layer in range(depth):
        prm, col = _prepare(norm_pre_gain[layer], w_in[layer], conv_w[layer], conv_b[layer], dt_bias[layer],
                            a_log[layer], d_skip[layer], ssd_norm_gain[layer], gla_gk_w[layer], gla_gk_b[layer],
                            gla_norm_gain[layer], w_branch_ssd[layer], w_branch_gla[layer], w_out[layer],
                            norm_post_gain[layer])
        yp, c_p, s_p, g_p = _layer(yp, None, None, None, prm, col)
        ys, c_s, s_s, g_s = _layer(ys, state_conv_ssd[layer], state_ssd[layer], state_gla[layer], prm, col)
        for lst, val in zip(outs, (c_p, s_p, g_p, c_s, s_s, g_s)):
            lst.append(val)
    return (yp, ys) + tuple(jnp.stack(o) for o in outs)
```

```python
import functools
import math

import numpy as np
import jax
import jax.numpy as jnp
from jax import lax
from jax.experimental import pallas as pl
from jax.experimental.pallas import tpu as pltpu

F32 = jnp.float32
BF16 = jnp.bfloat16

LANES = 128
SUBLANES = 8
VMEM_LIMIT_BYTES = 56 * 1024 * 1024

CHUNK = 64
SSD_HEAD_DIM = 64
SSD_GROUPS = 8
SSD_D_STATE = 128
SSD_CONV_W = 4
GLA_HEADS = 4
GLA_GATE_RANK = 16
GLA_GATE_NORMALIZER = 16.0
NORM_EPS = 1e-6
SSD_NORM_EPS = 1e-5
SUB = 8

CHUNKS_PER_STEP = 8
CONV_LANES = 2 * LANES
INPROJ_TM, INPROJ_TN = 1024, 2048
MERGE_TM, MERGE_TN = 256, 1024
POST_TM = 512
REGROUP_TN = 1024
SMALL_W = 256
DT_COPIES = 2
LOG2E = math.log2(math.e)


def _dot(a, b, dims=(((1,), (0,)), ((), ()))):
    return lax.dot_general(a, b, dims, preferred_element_type=F32)


_NT = (((1,), (1,)), ((), ()))
_TN = (((0,), (0,)), ((), ()))


def _sigmoid_half(h):
    return 0.5 * jnp.tanh(h) + 0.5


def _silu_half(h):
    return h * jnp.tanh(h) + h


def _softplus(x):
    return jnp.maximum(x, 0.0) + jnp.log(1.0 + jnp.exp(-jnp.abs(x)))


def _residuals(x):
    r1 = x - x.astype(BF16).astype(F32)
    r2 = r1 - r1.astype(BF16).astype(F32)
    return x, r1, r2


def _cumsum_rows(tril3_ref, x):
    parts = [p.astype(BF16) for p in _residuals(x)]
    return _dot(tril3_ref[...], jnp.concatenate(parts, axis=0))


def _inproj_kernel(x_ref, g_ref, w_ref, ws_ref, o_ref, os_ref, h_ref):
    @pl.when(pl.program_id(1) == 0)
    def _():
        x = x_ref[...]
        ms = jnp.mean(x * x, axis=-1, keepdims=True)
        h_ref[...] = (x * lax.rsqrt(ms + NORM_EPS) * g_ref[...]).astype(BF16)
        os_ref[...] = _dot(h_ref[...], ws_ref[...])

    o_ref[...] = _dot(h_ref[...], w_ref[...]).astype(o_ref.dtype)


def _inproj(x2, gain, w_main, w_small, tm, tn):
    m, d = x2.shape
    n = w_main.shape[1]
    ns = w_small.shape[1]
    return pl.pallas_call(
        _inproj_kernel,
        grid=(m // tm, n // tn),
        in_specs=[
            pl.BlockSpec((tm, d), lambda i, j: (i, 0)),
            pl.BlockSpec((1, d), lambda i, j: (0, 0)),
            pl.BlockSpec((d, tn), lambda i, j: (0, j)),
            pl.BlockSpec((d, ns), lambda i, j: (0, 0)),
        ],
        out_specs=[
            pl.BlockSpec((tm, tn), lambda i, j: (i, j)),
            pl.BlockSpec((tm, ns), lambda i, j: (i, 0)),
        ],
        out_shape=[jax.ShapeDtypeStruct((m, n), BF16), jax.ShapeDtypeStruct((m, ns), F32)],
        scratch_shapes=[pltpu.VMEM((tm, d), BF16)],
        compiler_params=pltpu.CompilerParams(
            dimension_semantics=("arbitrary", "arbitrary"), vmem_limit_bytes=VMEM_LIMIT_BYTES),
    )(x2, gain, w_main, w_small)


def _ssd_kernel(has_state, n_heads, n_sub, *refs):
    if has_state:
        (xs_ref, z_ref, bc_ref, sm_ref, cwx_ref, cbx_ref, cwb_ref, cbb_ref, dtb_ref, alog_ref, dsk_ref,
         gain_ref, exp_ref, tril_ref, shift_ref, csx_ref, csb_ref, st_ref,
         y_ref, so_ref, tailx, tailb, srow, dtrow, xs_s, xs16, bc16, bcast, yz_s, *stT) = refs
    else:
        (xs_ref, z_ref, bc_ref, sm_ref, cwx_ref, cbx_ref, cwb_ref, cbb_ref, dtb_ref, alog_ref, dsk_ref,
         gain_ref, exp_ref, tril_ref, shift_ref,
         y_ref, so_ref, tailx, tailb, srow, dtrow, xs_s, xs16, bc16, bcast, yz_s, *stT) = refs
    c = pl.program_id(1)
    n_chunks = pl.num_programs(1)
    L = CHUNK
    P = SSD_HEAD_DIM
    N = SSD_D_STATE
    G = SSD_GROUPS
    gw = (n_heads // G) * P
    d_inner = n_heads * P
    nc = SSD_CONV_W - 1

    @pl.when(c == 0)
    def _():
        tailx[...] = jnp.zeros_like(tailx)
        tailb[...] = jnp.zeros_like(tailb)
        if has_state:
            tailx[SUBLANES - nc:SUBLANES, :] = csx_ref[0]
            tailb[SUBLANES - nc:SUBLANES, :] = csb_ref[0]
            for g in range(G):
                stT[g][...] = st_ref[0, g * gw:(g + 1) * gw, :].T
        else:
            for g in range(G):
                stT[g][...] = jnp.zeros_like(stT[g])

    def chunk(si, carry):
        c0 = pl.multiple_of(si * L, L)
        cw = CONV_LANES
        rb = 2 * SUBLANES
        row8 = lax.broadcasted_iota(jnp.int32, (SUBLANES, cw), 0)

        def conv(x_ref, tail, w_ref, b_ref, width, emit):
            for cb in range(width // cw):
                cs = slice(cb * cw, (cb + 1) * cw)
                sh = _dot(shift_ref[...], x_ref[0, pl.ds(c0, L), cs])
                prev = tail[:, cs]
                fix = jnp.zeros((SUBLANES, cw), F32)
                for k in range(1, SSD_CONV_W):
                    fix = fix + jnp.where(row8 < k, pltpu.roll(prev, k, 0), 0.0) * w_ref[nc - k, 0:SUBLANES, cs]
                for blk in range(L // rb):
                    acc = sh[blk * nc * rb:(blk * nc + 1) * rb, :] * w_ref[nc - 1, :, cs]
                    for k in range(2, SSD_CONV_W):
                        s0 = (blk * nc + k - 1) * rb
                        acc = acc + sh[s0:s0 + rb, :] * w_ref[nc - k, :, cs]
                    acc = acc + x_ref[0, pl.ds(c0 + blk * rb, rb), cs].astype(F32) * w_ref[nc, :, cs] + b_ref[:, cs]
                    if blk == 0:
                        acc = jnp.concatenate([acc[0:SUBLANES, :] + fix, acc[SUBLANES:, :]], axis=0)
                    emit(blk * rb, rb, cs, _silu_half(acc))

        def emit_x(r0, nr, cs, act):
            xs_s[r0:r0 + nr, cs] = act
            xs16[r0:r0 + nr, cs] = act.astype(BF16)

        def emit_bc(r0, nr, cs, act):
            bc16[r0:r0 + nr, cs] = act.astype(BF16)

        conv(xs_ref, tailx, cwx_ref, cbx_ref, d_inner, emit_x)
        conv(bc_ref, tailb, cwb_ref, cbb_ref, 2 * G * N, emit_bc)

        dt3 = _softplus(sm_ref[0, pl.ds(c0, L), :] + dtb_ref[...])
        s3 = _cumsum_rows(tril_ref, dt3 * (-jnp.exp(alog_ref[...])))
        w3 = dt3 * jnp.exp(s3[L - 1:L, :] - s3)
        lane_s = lax.broadcasted_iota(jnp.int32, (2 * L, SMALL_W), 1)
        x0, x1, _ = _residuals(jnp.concatenate([s3, w3], axis=0))
        split = jnp.where(lane_s < n_heads, x0, x1)[:, 0:DT_COPIES * n_heads].astype(BF16)
        bcast[...] = _dot(split, exp_ref[...])
        hh = n_heads // 2
        s_t = s3[:, 0:2 * L].T
        dt_t = dt3[:, 0:2 * L].T
        srow[...] = jnp.concatenate([s_t[0:hh], s_t[hh:2 * hh]], axis=1)
        dtrow[...] = jnp.concatenate([dt_t[0:hh], dt_t[hh:2 * hh]], axis=1)

        row = lax.broadcasted_iota(jnp.int32, (L, 2 * P), 0)
        lane = lax.broadcasted_iota(jnp.int32, (L, 2 * P), 1)
        causal2 = (lane % P) <= row
        left = lane < P
        ppg = gw // (2 * P)

        for g in range(G):
            gs = slice(g * gw, (g + 1) * gw)
            bm16 = bc16[:, g * N:(g + 1) * N]
            cm16 = bc16[:, (G + g) * N:(G + g + 1) * N]
            cb2 = _dot(cm16, jnp.concatenate([bm16, bm16], axis=0), _NT)
            ssq = jnp.zeros((L, 1), F32)
            for pp in range(ppg):
                p = g * ppg + pp
                sl = slice(p * 2 * P, (p + 1) * 2 * P)
                se = bcast[0:L, sl]
                dec = jnp.exp(jnp.where(causal2, se - srow[p:p + 1, :], -jnp.inf))
                m = (cb2 * dec * dtrow[p:p + 1, :]).astype(BF16)
                xp16 = xs16[:, sl]
                zero = jnp.zeros_like(xp16)
                rhs = jnp.concatenate([jnp.where(left, xp16, zero), jnp.where(left, zero, xp16)], axis=0)
                y = _dot(m, rhs)
                y = y + _dot(cm16, stT[g][:, pp * 2 * P:(pp + 1) * 2 * P].astype(BF16)) * jnp.exp(se) + dsk_ref[:, sl] * xs_s[:, sl]
                yz = y * _silu_half(z_ref[0, pl.ds(c0, L), sl].astype(F32))
                ssq = ssq + jnp.sum(yz * yz, axis=-1, keepdims=True)
                yz_s[:, pp * 2 * P:(pp + 1) * 2 * P] = yz
            s_last = bcast[L - 1:L, gs]
            xdt_end = xs_s[:, gs] * bcast[L:2 * L, gs]
            upd = _dot(bm16, xdt_end.astype(BF16), _TN)
            stT[g][...] = stT[g][...] * jnp.exp(s_last) + upd

            scale = lax.rsqrt(ssq * (1.0 / gw) + SSD_NORM_EPS)
            y_ref[0, pl.ds(c0, L), gs] = (yz_s[...] * scale * gain_ref[:, gs]).astype(y_ref.dtype)

        tailx[...] = xs_ref[0, pl.ds(c0 + L - 2 * SUBLANES, 2 * SUBLANES), :].astype(F32)[SUBLANES:, :]
        tailb[...] = bc_ref[0, pl.ds(c0 + L - 2 * SUBLANES, 2 * SUBLANES), :].astype(F32)[SUBLANES:, :]

        return carry

    lax.fori_loop(0, n_sub, chunk, 0)

    @pl.when(c == n_chunks - 1)
    def _():
        for g in range(G):
            so_ref[0, g * gw:(g + 1) * gw, :] = stT[g][...].T


def _ssd(proj3, small3, col, prm, conv_state, ssd_state):
    b, t, _ = proj3.shape
    n_heads = prm["n_heads"]
    d_inner = n_heads * SSD_HEAD_DIM
    bcw = 2 * SSD_GROUPS * SSD_D_STATE
    has_state = ssd_state is not None
    L = CHUNK
    tb = _time_block(t)
    const = lambda *shape: pl.BlockSpec(shape, lambda i, j: (0,) * len(shape))
    in_specs = [
        pl.BlockSpec((1, tb, d_inner), lambda i, j: (i, j, col["xs"] // d_inner)),
        pl.BlockSpec((1, tb, d_inner), lambda i, j: (i, j, col["z"] // d_inner)),
        pl.BlockSpec((1, tb, bcw), lambda i, j: (i, j, col["bc"] // bcw)),
        pl.BlockSpec((1, tb, SMALL_W), lambda i, j: (i, j, 0)),
        const(SSD_CONV_W, 2 * SUBLANES, d_inner), const(2 * SUBLANES, d_inner),
        const(SSD_CONV_W, 2 * SUBLANES, bcw), const(2 * SUBLANES, bcw),
        const(1, SMALL_W), const(1, SMALL_W), const(1, d_inner), const(1, d_inner),
        const(DT_COPIES * n_heads, d_inner), const(L, 3 * L), const((SSD_CONV_W - 1) * L, L),
    ]
    args = [proj3, proj3, proj3, small3, prm["conv_w_x"], prm["conv_b_x"], prm["conv_w_bc"], prm["conv_b_bc"],
            prm["dt_bias3"], prm["a_log3"], prm["d_skip_e"], prm["ssd_norm_gain"], prm["expand3"], prm["tril3"],
            prm["shift"]]
    if has_state:
        nc = SSD_CONV_W - 1
        in_specs += [
            pl.BlockSpec((1, nc, d_inner), lambda i, j: (i, 0, 0)),
            pl.BlockSpec((1, nc, bcw), lambda i, j: (i, 0, 0)),
            pl.BlockSpec((1, d_inner, SSD_D_STATE), lambda i, j: (i, 0, 0)),
        ]
        args += [conv_state[:, :, :d_inner], conv_state[:, :, d_inner:],
                 ssd_state.reshape(b, d_inner, SSD_D_STATE)]
    y, st = pl.pallas_call(
        functools.partial(_ssd_kernel, has_state, n_heads, tb // L),
        grid=(b, t // tb),
        in_specs=in_specs,
        out_specs=[
            pl.BlockSpec((1, tb, d_inner), lambda i, j: (i, j, 0)),
            pl.BlockSpec((1, d_inner, SSD_D_STATE), lambda i, j: (i, 0, 0)),
        ],
        out_shape=[
            jax.ShapeDtypeStruct((b, t, d_inner), BF16),
            jax.ShapeDtypeStruct((b, d_inner, SSD_D_STATE), F32),
        ],
        scratch_shapes=[
            pltpu.VMEM((SUBLANES, d_inner), F32),
            pltpu.VMEM((SUBLANES, bcw), F32),
            pltpu.VMEM((n_heads // 2, 2 * L), F32),
            pltpu.VMEM((n_heads // 2, 2 * L), F32),
            pltpu.VMEM((L, d_inner), F32),
            pltpu.VMEM((L, d_inner), BF16),
            pltpu.VMEM((L, bcw), BF16),
            pltpu.VMEM((2 * L, d_inner), F32),
            pltpu.VMEM((L, d_inner // SSD_GROUPS), F32),
        ] + [pltpu.VMEM((SSD_D_STATE, d_inner // SSD_GROUPS), F32) for _ in range(SSD_GROUPS)],
        compiler_params=pltpu.CompilerParams(
            dimension_semantics=("arbitrary", "arbitrary"), vmem_limit_bytes=VMEM_LIMIT_BYTES),
    )(*args)
    return y, st.reshape(b, n_heads, SSD_HEAD_DIM, SSD_D_STATE)


def _gla_kernel(has_state, n_sub, *refs):
    if has_state:
        (q_ref, k_ref, v_ref, gb_ref, sm_ref, wgk_ref, bgk_ref, gain_ref, tril_ref, st_ref,
         o_ref, so_ref, bcs, ks, qs, a_s, *stT) = refs
    else:
        (q_ref, k_ref, v_ref, gb_ref, sm_ref, wgk_ref, bgk_ref, gain_ref, tril_ref,
         o_ref, so_ref, bcs, ks, qs, a_s, *stT) = refs
    c = pl.program_id(1)
    n_chunks = pl.num_programs(1)
    L = CHUNK
    H = GLA_HEADS
    kd = q_ref.shape[2]
    vd = v_ref.shape[2]
    hk = kd // H
    hv = vd // H
    nb = L // SUB

    @pl.when(c == 0)
    def _():
        if has_state:
            for h in range(H):
                stT[h][...] = st_ref[0, h].T
        else:
            for h in range(H):
                stT[h][...] = jnp.zeros_like(stT[h])

    def chunk(si, carry):
        c0 = pl.multiple_of(si * L, L)
        sm = sm_ref[0, pl.ds(c0, L), :]
        sm_hi = sm.astype(BF16)
        sm_lo = (sm - sm_hi.astype(F32)).astype(BF16)
        x = _dot(jnp.concatenate([sm_hi, sm_hi, sm_lo], axis=1), wgk_ref[...]) + bgk_ref[...]
        glog2 = (jnp.minimum(x, 0.0) - jnp.log(1.0 + jnp.exp(-jnp.abs(x)))) * (LOG2E / GLA_GATE_NORMALIZER)
        bcs[...] = _cumsum_rows(tril_ref, glog2)
        qs[...] = q_ref[0, pl.ds(c0, L), :].astype(F32) * (hk ** -0.5)
        ks[...] = k_ref[0, pl.ds(c0, L), :].astype(F32)

        nt = SUB // SUBLANES
        lane = lax.broadcasted_iota(jnp.int32, (SUBLANES, L), 1)
        row = lax.broadcasted_iota(jnp.int32, (SUBLANES, L), 0)
        for blk in range(nb):
            r0 = blk * SUB
            off = [None] * H
            if blk > 0:
                ref_row = bcs[r0 - 1:r0, :]
                qd = (qs[r0:r0 + SUB, :] * jnp.exp2(bcs[r0:r0 + SUB, :] - ref_row)).astype(BF16)
                kt = ks[0:r0, :] * jnp.exp2(ref_row - bcs[0:r0, :])
                kt = jnp.concatenate([kt, jnp.zeros((L - r0, kd), F32)], axis=0).astype(BF16)
                for h in range(H):
                    off[h] = _dot(qd[:, h * hk:(h + 1) * hk], kt[:, h * hk:(h + 1) * hk], _NT)
            diag = [[jnp.zeros((SUBLANES, L), F32) for _ in range(nt)] for _ in range(H)]
            for jj in range(SUB):
                j = r0 + jj
                for ti in range(jj // SUBLANES, nt):
                    t0 = r0 + ti * SUBLANES
                    keep = (lane == j) & (row + t0 >= j)
                    for h in range(H):
                        tot = None
                        for lt in range(hk // LANES):
                            ls = slice(h * hk + lt * LANES, h * hk + (lt + 1) * LANES)
                            e = jnp.exp2(bcs[t0:t0 + SUBLANES, ls] - bcs[j:j + 1, ls])
                            pr = qs[t0:t0 + SUBLANES, ls] * e * ks[j:j + 1, ls]
                            tot = pr if tot is None else tot + pr
                        col = jnp.sum(tot, axis=-1, keepdims=True)
                        diag[h][ti] = jnp.where(keep, col, diag[h][ti])
            for h in range(H):
                d_h = jnp.concatenate(diag[h], axis=0)
                a_s[h, r0:r0 + SUB, :] = d_h if off[h] is None else off[h] + d_h

        for h in range(H):
            hs = slice(h * hk, (h + 1) * hk)
            bc_h = bcs[:, hs]
            bc_last = bcs[L - 1:L, hs]
            qe = (qs[:, hs] * jnp.exp2(bc_h)).astype(BF16)
            k_end = (ks[:, hs] * jnp.exp2(bc_last - bc_h)).astype(BF16)
            v_h = v_ref[0, pl.ds(c0, L), h * hv:(h + 1) * hv]
            o = _dot(a_s[h].astype(BF16), v_h) + _dot(qe, stT[h][...].astype(BF16), _NT)
            stT[h][...] = stT[h][...] * jnp.exp2(bc_last) + _dot(v_h, k_end, _TN)
            ms = jnp.mean(o * o, axis=-1, keepdims=True)
            on = o * lax.rsqrt(ms + NORM_EPS) * gain_ref[...]
            gate = _silu_half(gb_ref[0, pl.ds(c0, L), h * hv:(h + 1) * hv].astype(F32))
            o_ref[0, pl.ds(c0, L), h * hv:(h + 1) * hv] = (on * gate).astype(o_ref.dtype)

        return carry

    lax.fori_loop(0, n_sub, chunk, 0, unroll=2)

    @pl.when(c == n_chunks - 1)
    def _():
        for h in range(H):
            so_ref[0, h] = stT[h][...].T


def _gla(proj3, small3, col, prm, gla_state):
    b, t, _ = proj3.shape
    kd = prm["gla_key_dim"]
    vd = prm["gla_val_dim"]
    H = GLA_HEADS
    hk, hv = kd // H, vd // H
    has_state = gla_state is not None
    L = CHUNK
    tb = _time_block(t)
    const = lambda *shape: pl.BlockSpec(shape, lambda i, j: (0,) * len(shape))
    in_specs = [
        pl.BlockSpec((1, tb, kd), lambda i, j: (i, j, col["q"] // kd)),
        pl.BlockSpec((1, tb, kd), lambda i, j: (i, j, col["k"] // kd)),
        pl.BlockSpec((1, tb, vd), lambda i, j: (i, j, col["v"] // vd)),
        pl.BlockSpec((1, tb, vd), lambda i, j: (i, j, col["g_b"] // vd)),
        pl.BlockSpec((1, tb, SMALL_W), lambda i, j: (i, j, 0)),
        const(3 * SMALL_W, kd), const(1, kd), const(1, hv), const(L, 3 * L),
    ]
    args = [proj3, proj3, proj3, proj3, small3, prm["gk_w3"], prm["gk_b"], prm["gla_norm_gain"], prm["tril3"]]
    if has_state:
        in_specs.append(pl.BlockSpec((1, H, hk, hv), lambda i, j: (i, 0, 0, 0)))
        args.append(gla_state)
    return pl.pallas_call(
        functools.partial(_gla_kernel, has_state, tb // L),
        grid=(b, t // tb),
        in_specs=in_specs,
        out_specs=[
            pl.BlockSpec((1, tb, vd), lambda i, j: (i, j, 0)),
            pl.BlockSpec((1, H, hk, hv), lambda i, j: (i, 0, 0, 0)),
        ],
        out_shape=[
            jax.ShapeDtypeStruct((b, t, vd), BF16),
            jax.ShapeDtypeStruct((b, H, hk, hv), F32),
        ],
        scratch_shapes=[
            pltpu.VMEM((L, kd), F32),
            pltpu.VMEM((L, kd), F32),
            pltpu.VMEM((L, kd), F32),
            pltpu.VMEM((H, L, L), F32),
        ] + [pltpu.VMEM((hv, hk), F32) for _ in range(H)],
        compiler_params=pltpu.CompilerParams(
            dimension_semantics=("arbitrary", "arbitrary"), vmem_limit_bytes=VMEM_LIMIT_BYTES),
    )(*args)


def _merge_kernel(tn, ys_ref, yg_ref, g_ref, wbs_ref, wbg_ref, o_ref):
    d = o_ref.shape[1]
    for n in range(d // tn):
        ns = slice(n * tn, (n + 1) * tn)
        a = _dot(ys_ref[...], wbs_ref[:, ns])
        b = _dot(yg_ref[...], wbg_ref[:, ns])
        gate_s = _sigmoid_half(g_ref[:, ns].astype(F32))
        gate_g = _sigmoid_half(g_ref[:, d + n * tn:d + (n + 1) * tn].astype(F32))
        o_ref[:, ns] = (gate_s * a + gate_g * b).astype(o_ref.dtype)


def _post_kernel(mx_ref, x_ref, wo_ref, g_ref, o_ref):
    cc = _dot(mx_ref[...], wo_ref[...])
    ms = jnp.mean(cc * cc, axis=-1, keepdims=True)
    o_ref[...] = x_ref[...] + cc * lax.rsqrt(ms + NORM_EPS) * g_ref[...]


def _outproj(y_ssd, y_gla, proj, col, x2, prm, tm_merge, tn, tm_post):
    m, d = x2.shape
    d_inner = y_ssd.shape[1]
    vd = y_gla.shape[1]
    g0 = col["merge"] // (2 * d)
    resident = lambda r, c_: pl.BlockSpec((r, c_), lambda i: (0, 0), pipeline_mode=pl.Buffered(1))
    mixed = pl.pallas_call(
        functools.partial(_merge_kernel, tn),
        grid=(m // tm_merge,),
        in_specs=[
            pl.BlockSpec((tm_merge, d_inner), lambda i: (i, 0)),
            pl.BlockSpec((tm_merge, vd), lambda i: (i, 0)),
            pl.BlockSpec((tm_merge, 2 * d), lambda i: (i, g0)),
            resident(d_inner, d), resident(vd, d),
        ],
        out_specs=pl.BlockSpec((tm_merge, d), lambda i: (i, 0)),
        out_shape=jax.ShapeDtypeStruct((m, d), BF16),
        compiler_params=pltpu.CompilerParams(
            dimension_semantics=("arbitrary",), vmem_limit_bytes=VMEM_LIMIT_BYTES),
    )(y_ssd, y_gla, proj, prm["w_branch_ssd"], prm["w_branch_gla"])
    return pl.pallas_call(
        _post_kernel,
        grid=(m // tm_post,),
        in_specs=[
            pl.BlockSpec((tm_post, d), lambda i: (i, 0)),
            pl.BlockSpec((tm_post, d), lambda i: (i, 0)),
            pl.BlockSpec((d, d), lambda i: (0, 0)),
            pl.BlockSpec((1, d), lambda i: (0, 0)),
        ],
        out_specs=pl.BlockSpec((tm_post, d), lambda i: (i, 0)),
        out_shape=jax.ShapeDtypeStruct((m, d), F32),
        compiler_params=pltpu.CompilerParams(
            dimension_semantics=("arbitrary",), vmem_limit_bytes=VMEM_LIMIT_BYTES),
    )(mixed, x2, prm["w_out"], prm["norm_post_gain"])


def _regroup_kernel(tn, offsets, blk_ref, cls_ref, half_ref, a_ref, b_ref, o_ref):
    j = pl.program_id(0)
    scale = jnp.where(half_ref[j] == 1, 0.5, 1.0).astype(F32)
    for ci, off in enumerate(offsets):
        @pl.when(cls_ref[j] == ci)
        def _():
            x = a_ref[...] if off == 0 else jnp.concatenate([a_ref[off:tn, :], b_ref[0:off, :]], axis=0)
            o_ref[...] = (x.T * scale).astype(o_ref.dtype)


def _pick_rows_kernel(spans, *refs):
    for (start, size), src, dst in zip(spans, refs[:len(spans)], refs[len(spans):]):
        dst[...] = src[start % LANES:start % LANES + size, :]


def _pick_rows(a, *spans):
    d = a.shape[1]
    assert all(start % LANES + size <= LANES and size % SUBLANES == 0 for start, size in spans)
    return pl.pallas_call(
        functools.partial(_pick_rows_kernel, spans),
        grid=(1,),
        in_specs=[pl.BlockSpec((LANES, d), functools.partial(lambda blk, i: (blk, 0), start // LANES))
                  for start, _ in spans],
        out_specs=[pl.BlockSpec((size, d), lambda i: (0, 0)) for _, size in spans],
        out_shape=[jax.ShapeDtypeStruct((size, d), a.dtype) for _, size in spans],
    )(*([a] * len(spans)))


def _regroup_w_in(w_in_t, segments, tn):
    d = w_in_t.shape[1]
    offsets = sorted({src % tn for src, _, _ in segments})
    blk, cls, half = [], [], []
    for src, width, halve in segments:
        assert width % tn == 0 and src % tn < LANES and (src % tn) % SUBLANES == 0
        for i in range(width // tn):
            blk.append(src // tn + i)
            cls.append(offsets.index(src % tn))
            half.append(int(halve))
    n_blocks = len(blk)
    to_i32 = lambda v: jnp.asarray(np.array(v, np.int32))
    return pl.pallas_call(
        functools.partial(_regroup_kernel, tn, tuple(offsets)),
        grid_spec=pltpu.PrefetchScalarGridSpec(
            num_scalar_prefetch=3,
            grid=(n_blocks,),
            in_specs=[
                pl.BlockSpec((tn, d), lambda j, blk, cls, half: (blk[j], 0)),
                pl.BlockSpec((LANES, d), lambda j, blk, cls, half: ((blk[j] + 1) * (tn // LANES), 0)),
            ],
            out_specs=pl.BlockSpec((d, tn), lambda j, blk, cls, half: (0, j)),
        ),
        out_shape=jax.ShapeDtypeStruct((d, n_blocks * tn), BF16),
        compiler_params=pltpu.CompilerParams(
            dimension_semantics=("arbitrary",), vmem_limit_bytes=VMEM_LIMIT_BYTES),
    )(to_i32(blk), to_i32(cls), to_i32(half), w_in_t, w_in_t)
def _prepare(norm_pre_gain, w_in, conv_w, conv_b, dt_bias, a_log, d_skip, ssd_norm_gain, gla_gk_w, gla_gk_b,
             gla_norm_gain, w_branch_ssd, w_branch_gla, w_out, norm_post_gain):
    d = w_in.shape[0]
    n_heads = dt_bias.shape[0]
    d_inner = n_heads * SSD_HEAD_DIM
    bcw = 2 * SSD_GROUPS * SSD_D_STATE
    kd = gla_gk_w.shape[1]
    vd = w_branch_gla.shape[0]
    assert DT_COPIES * n_heads + GLA_GATE_RANK <= SMALL_W
    sizes = (d_inner, d_inner + bcw, n_heads, kd, kd, vd, vd, GLA_GATE_RANK, 2 * d)
    offs = np.concatenate([[0], np.cumsum(sizes)])
    z0, xbc0, dt0, q0, k0, v0, gb0, gkl0, mg0 = [int(o_) for o_ in offs[:9]]
    perm = np.concatenate([np.arange(0, n_heads, 2), np.arange(1, n_heads, 2)])
    w_in_t = w_in.T
    dt_rows, gk_rows = _pick_rows(w_in_t, (dt0, n_heads), (gkl0, GLA_GATE_RANK))
    dt_w = dt_rows[perm]
    gk0 = DT_COPIES * n_heads
    pad = SMALL_W - gk0 - GLA_GATE_RANK
    w_small = jnp.concatenate(
        [dt_w] * DT_COPIES + [gk_rows, jnp.zeros((pad, d), w_in.dtype)], axis=0).T.astype(BF16)
    half = 0.5
    pieces = [("xs", xbc0, d_inner, False), ("z", z0, d_inner, True), ("merge", mg0, 2 * d, True),
              ("bc", xbc0 + d_inner, bcw, False), ("v", v0, vd, False), ("g_b", gb0, vd, True),
              ("q", q0, kd, False), ("k", k0, kd, False)]
    col, o = {}, 0
    for name, _, width, _ in pieces:
        assert o % width == 0, name
        col[name] = o
        o += width
    w_main = _regroup_w_in(w_in_t, [p[1:] for p in pieces], REGROUP_TN)

    def small_row(v, fill):
        return jnp.concatenate([v[perm]] * DT_COPIES + [jnp.full((SMALL_W - gk0,), fill, v.dtype)])[None, :]

    expand = np.zeros((DT_COPIES * n_heads, d_inner), np.float32)
    for r in range(DT_COPIES):
        for c_ in range(n_heads):
            expand[r * n_heads + c_, perm[c_] * SSD_HEAD_DIM:(perm[c_] + 1) * SSD_HEAD_DIM] = 1.0
    tril = np.tril(np.ones((CHUNK, CHUNK), np.float32))
    rb = 2 * SUBLANES
    shift = np.concatenate([np.eye(CHUNK, k=-k, dtype=np.float32)[b * rb:(b + 1) * rb]
                            for b in range(CHUNK // rb) for k in range(1, SSD_CONV_W)], axis=0)
    gk_w_pad = jnp.zeros((SMALL_W, kd), F32).at[gk0:gk0 + GLA_GATE_RANK].set(gla_gk_w)
    gk_hi = gk_w_pad.astype(BF16)
    gk_lo = (gk_w_pad - gk_hi.astype(F32)).astype(BF16)
    rows = lambda a: jnp.broadcast_to(a, a.shape[:-2] + (2 * SUBLANES, a.shape[-1]))
    prm = dict(
        n_heads=n_heads, gla_key_dim=kd, gla_val_dim=vd,
        norm_pre_gain=norm_pre_gain[None, :], w_main=w_main, w_small=w_small,
        conv_w_x=rows(half * conv_w[:, None, :d_inner]), conv_w_bc=rows(half * conv_w[:, None, d_inner:]),
        conv_b_x=rows(half * conv_b[None, :d_inner]), conv_b_bc=rows(half * conv_b[None, d_inner:]),
        dt_bias3=small_row(dt_bias, 0.0), a_log3=small_row(a_log, 0.0),
        d_skip_e=jnp.repeat(d_skip, SSD_HEAD_DIM)[None, :], ssd_norm_gain=ssd_norm_gain[None, :],
        shift=jnp.asarray(shift, BF16), expand3=jnp.asarray(expand, BF16), tril3=jnp.asarray(np.concatenate([tril] * 3, axis=1), BF16),
        gk_w3=jnp.concatenate([gk_hi, gk_lo, gk_hi], axis=0), gk_b=gla_gk_b[None, :],
        gla_norm_gain=gla_norm_gain[None, :],
        w_branch_ssd=w_branch_ssd.astype(BF16), w_branch_gla=w_branch_gla.astype(BF16),
        w_out=w_out.astype(BF16), norm_post_gain=norm_post_gain[None, :],
    )
    return prm, col


def _time_block(t):
    tb = min(t, CHUNKS_PER_STEP * CHUNK)
    assert t % tb == 0
    return tb


def _row_tile(m, want):
    tm = min(m, want)
    assert m % tm == 0
    return tm


def _layer(x, conv_state, ssd_state, gla_state, prm, col):
    b, t, d = x.shape
    assert t % CHUNK == 0 and t >= SSD_CONV_W - 1
    m = b * t
    x2 = x.reshape(m, d)
    proj, small = _inproj(x2, prm["norm_pre_gain"], prm["w_main"], prm["w_small"], _row_tile(m, INPROJ_TM), INPROJ_TN)
    proj3 = proj.reshape(b, t, -1)
    small3 = small.reshape(b, t, -1)
    d_inner = prm["n_heads"] * SSD_HEAD_DIM
    bcw = 2 * SSD_GROUPS * SSD_D_STATE
    nc = SSD_CONV_W - 1
    new_conv = jnp.concatenate(
        [proj3[:, t - nc:, col["xs"]:col["xs"] + d_inner], proj3[:, t - nc:, col["bc"]:col["bc"] + bcw]],
        axis=-1).astype(x.dtype)
    y_ssd, new_ssd = _ssd(proj3, small3, col, prm, conv_state, ssd_state)
    y_gla, new_gla = _gla(proj3, small3, col, prm, gla_state)
    y = _outproj(y_ssd.reshape(m, -1), y_gla.reshape(m, -1), proj, col, x2, prm,
                 _row_tile(m, MERGE_TM), MERGE_TN, _row_tile(m, POST_TM))
    return y.reshape(b, t, d), new_conv, new_ssd, new_gla


def kernel(x_prompt, x_sample, state_conv_ssd, state_ssd, state_gla, norm_pre_gain, w_in, conv_w, conv_b, dt_bias,
           a_log, d_skip, ssd_norm_gain, gla_gk_w, gla_gk_b, gla_norm_gain, w_branch_ssd, w_branch_gla, w_out,
           norm_post_gain):
    depth = w_in.shape[0]
    yp, ys = x_prompt, x_sample
    outs = [[] for _ in range(6)]
    for layer in range(depth):
        prm, col = _prepare(norm_pre_gain[layer], w_in[layer], conv_w[layer], conv_b[layer], dt_bias[layer],
                            a_log[layer], d_skip[layer], ssd_norm_gain[layer], gla_gk_w[layer], gla_gk_b[layer],
                            gla_norm_gain[layer], w_branch_ssd[layer], w_branch_gla[layer], w_out[layer],
                            norm_post_gain[layer])
        yp, c_p, s_p, g_p = _layer(yp, None, None, None, prm, col)
        ys, c_s, s_s, g_s = _layer(ys, state_conv_ssd[layer], state_ssd[layer], state_gla[layer], prm, col)
        for lst, val in zip(outs, (c_p, s_p, g_p, c_s, s_s, g_s)):
            lst.append(val)
    return (yp, ys) + tuple(jnp.stack(o) for o in outs)
```

```python
import functools
import math

import numpy as np
import jax
import jax.numpy as jnp
from jax import lax
from jax.experimental import pallas as pl
from jax.experimental.pallas import tpu as pltpu

F32 = jnp.float32
BF16 = jnp.bfloat16

LANES = 128
SUBLANES = 8
VMEM_LIMIT_BYTES = 56 * 1024 * 1024

CHUNK = 64
SSD_HEAD_DIM = 64
SSD_GROUPS = 8
SSD_D_STATE = 128
SSD_CONV_W = 4
GLA_HEADS = 4
GLA_GATE_RANK = 16
GLA_GATE_NORMALIZER = 16.0
NORM_EPS = 1e-6
SSD_NORM_EPS = 1e-5
SUB = 8

CHUNKS_PER_STEP = 8
CONV_LANES = 2 * LANES
INPROJ_TM, INPROJ_TN = 1024, 2048
MERGE_TM, MERGE_TN = 256, 1024
REGROUP_TN = 1024
SMALL_W = 256
DT_COPIES = 2
LOG2E = math.log2(math.e)


def _dot(a, b, dims=(((1,), (0,)), ((), ()))):
    return lax.dot_general(a, b, dims, preferred_element_type=F32)


_NT = (((1,), (1,)), ((), ()))
_TN = (((0,), (0,)), ((), ()))


def _sigmoid_half(h):
    return 0.5 * jnp.tanh(h) + 0.5


def _silu_half(h):
    return h * jnp.tanh(h) + h


def _softplus(x):
    return jnp.maximum(x, 0.0) + jnp.log(1.0 + jnp.exp(-jnp.abs(x)))


def _residuals(x):
    r1 = x - x.astype(BF16).astype(F32)
    r2 = r1 - r1.astype(BF16).astype(F32)
    return x, r1, r2


def _cumsum_rows(tril3_ref, x):
    parts = [p.astype(BF16) for p in _residuals(x)]
    return _dot(tril3_ref[...], jnp.concatenate(parts, axis=0))


def _inproj_kernel(x_ref, g_ref, w_ref, ws_ref, o_ref, os_ref, h_ref):
    @pl.when(pl.program_id(1) == 0)
    def _():
        x = x_ref[...]
        ms = jnp.mean(x * x, axis=-1, keepdims=True)
        h_ref[...] = (x * lax.rsqrt(ms + NORM_EPS) * g_ref[...]).astype(BF16)
        os_ref[...] = _dot(h_ref[...], ws_ref[...])

    o_ref[...] = _dot(h_ref[...], w_ref[...]).astype(o_ref.dtype)


def _inproj(x2, gain, w_main, w_small, tm, tn):
    m, d = x2.shape
    n = w_main.shape[1]
    ns = w_small.shape[1]
    return pl.pallas_call(
        _inproj_kernel,
        grid=(m // tm, n // tn),
        in_specs=[
            pl.BlockSpec((tm, d), lambda i, j: (i, 0)),
            pl.BlockSpec((1, d), lambda i, j: (0, 0)),
            pl.BlockSpec((d, tn), lambda i, j: (0, j)),
            pl.BlockSpec((d, ns), lambda i, j: (0, 0)),
        ],
        out_specs=[
            pl.BlockSpec((tm, tn), lambda i, j: (i, j)),
            pl.BlockSpec((tm, ns), lambda i, j: (i, 0)),
        ],
        out_shape=[jax.ShapeDtypeStruct((m, n), BF16), jax.ShapeDtypeStruct((m, ns), F32)],
        scratch_shapes=[pltpu.VMEM((tm, d), BF16)],
        compiler_params=pltpu.CompilerParams(
            dimension_semantics=("arbitrary", "arbitrary"), vmem_limit_bytes=VMEM_LIMIT_BYTES),
    )(x2, gain, w_main, w_small)


def _ssd_kernel(has_state, n_heads, n_sub, *refs):
    if has_state:
        (xs_ref, z_ref, bc_ref, sm_ref, cwx_ref, cbx_ref, cwb_ref, cbb_ref, dtb_ref, alog_ref, dsk_ref,
         gain_ref, exp_ref, tril_ref, shift_ref, csx_ref, csb_ref, st_ref,
         y_ref, so_ref, tailx, tailb, srow, dtrow, xs_s, xs16, bc16, bcast, yz_s, *stT) = refs
    else:
        (xs_ref, z_ref, bc_ref, sm_ref, cwx_ref, cbx_ref, cwb_ref, cbb_ref, dtb_ref, alog_ref, dsk_ref,
         gain_ref, exp_ref, tril_ref, shift_ref,
         y_ref, so_ref, tailx, tailb, srow, dtrow, xs_s, xs16, bc16, bcast, yz_s, *stT) = refs
    c = pl.program_id(1)
    n_chunks = pl.num_programs(1)
    L = CHUNK
    P = SSD_HEAD_DIM
    N = SSD_D_STATE
    G = SSD_GROUPS
    gw = (n_heads // G) * P
    d_inner = n_heads * P
    nc = SSD_CONV_W - 1

    @pl.when(c == 0)
    def _():
        tailx[...] = jnp.zeros_like(tailx)
        tailb[...] = jnp.zeros_like(tailb)
        if has_state:
            tailx[SUBLANES - nc:SUBLANES, :] = csx_ref[0]
            tailb[SUBLANES - nc:SUBLANES, :] = csb_ref[0]
            for g in range(G):
                stT[g][...] = st_ref[0, g * gw:(g + 1) * gw, :].T
        else:
            for g in range(G):
                stT[g][...] = jnp.zeros_like(stT[g])

    def chunk(si, carry):
        c0 = pl.multiple_of(si * L, L)
        cw = CONV_LANES
        rb = 2 * SUBLANES
        row8 = lax.broadcasted_iota(jnp.int32, (SUBLANES, cw), 0)

        def conv(x_ref, tail, w_ref, b_ref, width, emit):
            for cb in range(width // cw):
                cs = slice(cb * cw, (cb + 1) * cw)
                sh = _dot(shift_ref[...], x_ref[0, pl.ds(c0, L), cs])
                prev = tail[:, cs]
                fix = jnp.zeros((SUBLANES, cw), F32)
                for k in range(1, SSD_CONV_W):
                    fix = fix + jnp.where(row8 < k, pltpu.roll(prev, k, 0), 0.0) * w_ref[nc - k, 0:SUBLANES, cs]
                for blk in range(L // rb):
                    acc = sh[blk * nc * rb:(blk * nc + 1) * rb, :] * w_ref[nc - 1, :, cs]
                    for k in range(2, SSD_CONV_W):
                        s0 = (blk * nc + k - 1) * rb
                        acc = acc + sh[s0:s0 + rb, :] * w_ref[nc - k, :, cs]
                    acc = acc + x_ref[0, pl.ds(c0 + blk * rb, rb), cs].astype(F32) * w_ref[nc, :, cs] + b_ref[:, cs]
                    if blk == 0:
                        acc = jnp.concatenate([acc[0:SUBLANES, :] + fix, acc[SUBLANES:, :]], axis=0)
                    emit(blk * rb, rb, cs, _silu_half(acc))

        def emit_x(r0, nr, cs, act):
            xs_s[r0:r0 + nr, cs] = act
            xs16[r0:r0 + nr, cs] = act.astype(BF16)

        def emit_bc(r0, nr, cs, act):
            bc16[r0:r0 + nr, cs] = act.astype(BF16)

        conv(xs_ref, tailx, cwx_ref, cbx_ref, d_inner, emit_x)
        conv(bc_ref, tailb, cwb_ref, cbb_ref, 2 * G * N, emit_bc)

        dt3 = _softplus(sm_ref[0, pl.ds(c0, L), :] + dtb_ref[...])
        s3 = _cumsum_rows(tril_ref, dt3 * (-jnp.exp(alog_ref[...])))
        w3 = dt3 * jnp.exp(s3[L - 1:L, :] - s3)
        lane_s = lax.broadcasted_iota(jnp.int32, (2 * L, SMALL_W), 1)
        x0, x1, _ = _residuals(jnp.concatenate([s3, w3], axis=0))
        split = jnp.where(lane_s < n_heads, x0, x1)[:, 0:DT_COPIES * n_heads].astype(BF16)
        bcast[...] = _dot(split, exp_ref[...])
        hh = n_heads // 2
        s_t = s3[:, 0:2 * L].T
        dt_t = dt3[:, 0:2 * L].T
        srow[...] = jnp.concatenate([s_t[0:hh], s_t[hh:2 * hh]], axis=1)
        dtrow[...] = jnp.concatenate([dt_t[0:hh], dt_t[hh:2 * hh]], axis=1)

        row = lax.broadcasted_iota(jnp.int32, (L, 2 * P), 0)
        lane = lax.broadcasted_iota(jnp.int32, (L, 2 * P), 1)
        causal2 = (lane % P) <= row
        left = lane < P
        ppg = gw // (2 * P)

        for g in range(G):
            gs = slice(g * gw, (g + 1) * gw)
            bm16 = bc16[:, g * N:(g + 1) * N]
            cm16 = bc16[:, (G + g) * N:(G + g + 1) * N]
            cb2 = _dot(cm16, jnp.concatenate([bm16, bm16], axis=0), _NT)
            ssq = jnp.zeros((L, 1), F32)
            for pp in range(ppg):
                p = g * ppg + pp
                sl = slice(p * 2 * P, (p + 1) * 2 * P)
                se = bcast[0:L, sl]
                dec = jnp.exp(jnp.where(causal2, se - srow[p:p + 1, :], -jnp.inf))
                m = (cb2 * dec * dtrow[p:p + 1, :]).astype(BF16)
                xp16 = xs16[:, sl]
                zero = jnp.zeros_like(xp16)
                rhs = jnp.concatenate([jnp.where(left, xp16, zero), jnp.where(left, zero, xp16)], axis=0)
                y = _dot(m, rhs)
                y = y + _dot(cm16, stT[g][:, pp * 2 * P:(pp + 1) * 2 * P].astype(BF16)) * jnp.exp(se) + dsk_ref[:, sl] * xs_s[:, sl]
                yz = y * _silu_half(z_ref[0, pl.ds(c0, L), sl].astype(F32))
                ssq = ssq + jnp.sum(yz * yz, axis=-1, keepdims=True)
                yz_s[:, pp * 2 * P:(pp + 1) * 2 * P] = yz
            s_last = bcast[L - 1:L, gs]
            xdt_end = xs_s[:, gs] * bcast[L:2 * L, gs]
            upd = _dot(bm16, xdt_end.astype(BF16), _TN)
            stT[g][...] = stT[g][...] * jnp.exp(s_last) + upd

            scale = lax.rsqrt(ssq * (1.0 / gw) + SSD_NORM_EPS)
            y_ref[0, pl.ds(c0, L), gs] = (yz_s[...] * scale * gain_ref[:, gs]).astype(y_ref.dtype)

        tailx[...] = xs_ref[0, pl.ds(c0 + L - 2 * SUBLANES, 2 * SUBLANES), :].astype(F32)[SUBLANES:, :]
        tailb[...] = bc_ref[0, pl.ds(c0 + L - 2 * SUBLANES, 2 * SUBLANES), :].astype(F32)[SUBLANES:, :]

        return carry

    lax.fori_loop(0, n_sub, chunk, 0)

    @pl.when(c == n_chunks - 1)
    def _():
        for g in range(G):
            so_ref[0, g * gw:(g + 1) * gw, :] = stT[g][...].T


def _ssd(proj3, small3, col, prm, conv_state, ssd_state):
    b, t, _ = proj3.shape
    n_heads = prm["n_heads"]
    d_inner = n_heads * SSD_HEAD_DIM
    bcw = 2 * SSD_GROUPS * SSD_D_STATE
    has_state = ssd_state is not None
    L = CHUNK
    tb = _time_block(t)
    const = lambda *shape: pl.BlockSpec(shape, lambda i, j: (0,) * len(shape))
    in_specs = [
        pl.BlockSpec((1, tb, d_inner), lambda i, j: (i, j, col["xs"] // d_inner)),
        pl.BlockSpec((1, tb, d_inner), lambda i, j: (i, j, col["z"] // d_inner)),
        pl.BlockSpec((1, tb, bcw), lambda i, j: (i, j, col["bc"] // bcw)),
        pl.BlockSpec((1, tb, SMALL_W), lambda i, j: (i, j, 0)),
        const(SSD_CONV_W, 2 * SUBLANES, d_inner), const(2 * SUBLANES, d_inner),
        const(SSD_CONV_W, 2 * SUBLANES, bcw), const(2 * SUBLANES, bcw),
        const(1, SMALL_W), const(1, SMALL_W), const(1, d_inner), const(1, d_inner),
        const(DT_COPIES * n_heads, d_inner), const(L, 3 * L), const((SSD_CONV_W - 1) * L, L),
    ]
    args = [proj3, proj3, proj3, small3, prm["conv_w_x"], prm["conv_b_x"], prm["conv_w_bc"], prm["conv_b_bc"],
            prm["dt_bias3"], prm["a_log3"], prm["d_skip_e"], prm["ssd_norm_gain"], prm["expand3"], prm["tril3"],
            prm["shift"]]
    if has_state:
        nc = SSD_CONV_W - 1
        in_specs += [
            pl.BlockSpec((1, nc, d_inner), lambda i, j: (i, 0, 0)),
            pl.BlockSpec((1, nc, bcw), lambda i, j: (i, 0, 0)),
            pl.BlockSpec((1, d_inner, SSD_D_STATE), lambda i, j: (i, 0, 0)),
        ]
        args += [conv_state[:, :, :d_inner], conv_state[:, :, d_inner:],
                 ssd_state.reshape(b, d_inner, SSD_D_STATE)]
    y, st = pl.pallas_call(
        functools.partial(_ssd_kernel, has_state, n_heads, tb // L),
        grid=(b, t // tb),
        in_specs=in_specs,
        out_specs=[
            pl.BlockSpec((1, tb, d_inner), lambda i, j: (i, j, 0)),
            pl.BlockSpec((1, d_inner, SSD_D_STATE), lambda i, j: (i, 0, 0)),
        ],
        out_shape=[
            jax.ShapeDtypeStruct((b, t, d_inner), BF16),
            jax.ShapeDtypeStruct((b, d_inner, SSD_D_STATE), F32),
        ],
        scratch_shapes=[
            pltpu.VMEM((SUBLANES, d_inner), F32),
            pltpu.VMEM((SUBLANES, bcw), F32),
            pltpu.VMEM((n_heads // 2, 2 * L), F32),
            pltpu.VMEM((n_heads // 2, 2 * L), F32),
            pltpu.VMEM((L, d_inner), F32),
            pltpu.VMEM((L, d_inner), BF16),
            pltpu.VMEM((L, bcw), BF16),
            pltpu.VMEM((2 * L, d_inner), F32),
            pltpu.VMEM((L, d_inner // SSD_GROUPS), F32),
        ] + [pltpu.VMEM((SSD_D_STATE, d_inner // SSD_GROUPS), F32) for _ in range(SSD_GROUPS)],
        compiler_params=pltpu.CompilerParams(
            dimension_semantics=("arbitrary", "arbitrary"), vmem_limit_bytes=VMEM_LIMIT_BYTES),
    )(*args)
    return y, st.reshape(b, n_heads, SSD_HEAD_DIM, SSD_D_STATE)


def _gla_kernel(has_state, n_sub, *refs):
    if has_state:
        (q_ref, k_ref, v_ref, gb_ref, sm_ref, wgk_ref, bgk_ref, gain_ref, tril_ref, st_ref,
         o_ref, so_ref, bcs, ks, qs, a_s, *stT) = refs
    else:
        (q_ref, k_ref, v_ref, gb_ref, sm_ref, wgk_ref, bgk_ref, gain_ref, tril_ref,
         o_ref, so_ref, bcs, ks, qs, a_s, *stT) = refs
    c = pl.program_id(1)
    n_chunks = pl.num_programs(1)
    L = CHUNK
    H = GLA_HEADS
    kd = q_ref.shape[2]
    vd = v_ref.shape[2]
    hk = kd // H
    hv = vd // H
    nb = L // SUB

    @pl.when(c == 0)
    def _():
        if has_state:
            for h in range(H):
                stT[h][...] = st_ref[0, h].T
        else:
            for h in range(H):
                stT[h][...] = jnp.zeros_like(stT[h])

    def chunk(si, carry):
        c0 = pl.multiple_of(si * L, L)
        sm = sm_ref[0, pl.ds(c0, L), :]
        sm_hi = sm.astype(BF16)
        sm_lo = (sm - sm_hi.astype(F32)).astype(BF16)
        x = _dot(jnp.concatenate([sm_hi, sm_hi, sm_lo], axis=1), wgk_ref[...]) + bgk_ref[...]
        glog2 = (jnp.minimum(x, 0.0) - jnp.log(1.0 + jnp.exp(-jnp.abs(x)))) * (LOG2E / GLA_GATE_NORMALIZER)
        bcs[...] = _cumsum_rows(tril_ref, glog2)
        qs[...] = q_ref[0, pl.ds(c0, L), :].astype(F32) * (hk ** -0.5)
        ks[...] = k_ref[0, pl.ds(c0, L), :].astype(F32)

        nt = SUB // SUBLANES
        lane = lax.broadcasted_iota(jnp.int32, (SUBLANES, L), 1)
        row = lax.broadcasted_iota(jnp.int32, (SUBLANES, L), 0)
        for blk in range(nb):
            r0 = blk * SUB
            off = [None] * H
            if blk > 0:
                ref_row = bcs[r0 - 1:r0, :]
                qd = (qs[r0:r0 + SUB, :] * jnp.exp2(bcs[r0:r0 + SUB, :] - ref_row)).astype(BF16)
                kt = ks[0:r0, :] * jnp.exp2(ref_row - bcs[0:r0, :])
                kt = jnp.concatenate([kt, jnp.zeros((L - r0, kd), F32)], axis=0).astype(BF16)
                for h in range(H):
                    off[h] = _dot(qd[:, h * hk:(h + 1) * hk], kt[:, h * hk:(h + 1) * hk], _NT)
            diag = [[jnp.zeros((SUBLANES, L), F32) for _ in range(nt)] for _ in range(H)]
            for jj in range(SUB):
                j = r0 + jj
                for ti in range(jj // SUBLANES, nt):
                    t0 = r0 + ti * SUBLANES
                    keep = (lane == j) & (row + t0 >= j)
                    for h in range(H):
                        tot = None
                        for lt in range(hk // LANES):
                            ls = slice(h * hk + lt * LANES, h * hk + (lt + 1) * LANES)
                            e = jnp.exp2(bcs[t0:t0 + SUBLANES, ls] - bcs[j:j + 1, ls])
                            pr = qs[t0:t0 + SUBLANES, ls] * e * ks[j:j + 1, ls]
                            tot = pr if tot is None else tot + pr
                        col = jnp.sum(tot, axis=-1, keepdims=True)
                        diag[h][ti] = jnp.where(keep, col, diag[h][ti])
            for h in range(H):
                d_h = jnp.concatenate(diag[h], axis=0)
                a_s[h, r0:r0 + SUB, :] = d_h if off[h] is None else off[h] + d_h

        for h in range(H):
            hs = slice(h * hk, (h + 1) * hk)
            bc_h = bcs[:, hs]
            bc_last = bcs[L - 1:L, hs]
            qe = (qs[:, hs] * jnp.exp2(bc_h)).astype(BF16)
            k_end = (ks[:, hs] * jnp.exp2(bc_last - bc_h)).astype(BF16)
            v_h = v_ref[0, pl.ds(c0, L), h * hv:(h + 1) * hv]
            o = _dot(a_s[h].astype(BF16), v_h) + _dot(qe, stT[h][...].astype(BF16), _NT)
            stT[h][...] = stT[h][...] * jnp.exp2(bc_last) + _dot(v_h, k_end, _TN)
            ms = jnp.mean(o * o, axis=-1, keepdims=True)
            on = o * lax.rsqrt(ms + NORM_EPS) * gain_ref[...]
            gate = _silu_half(gb_ref[0, pl.ds(c0, L), h * hv:(h + 1) * hv].astype(F32))
            o_ref[0, pl.ds(c0, L), h * hv:(h + 1) * hv] = (on * gate).astype(o_ref.dtype)

        return carry

    lax.fori_loop(0, n_sub, chunk, 0, unroll=2)

    @pl.when(c == n_chunks - 1)
    def _():
        for h in range(H):
            so_ref[0, h] = stT[h][...].T


def _gla(proj3, small3, col, prm, gla_state):
    b, t, _ = proj3.shape
    kd = prm["gla_key_dim"]
    vd = prm["gla_val_dim"]
    H = GLA_HEADS
    hk, hv = kd // H, vd // H
    has_state = gla_state is not None
    L = CHUNK
    tb = _time_block(t)
    const = lambda *shape: pl.BlockSpec(shape, lambda i, j: (0,) * len(shape))
    in_specs = [
        pl.BlockSpec((1, tb, kd), lambda i, j: (i, j, col["q"] // kd)),
        pl.BlockSpec((1, tb, kd), lambda i, j: (i, j, col["k"] // kd)),
        pl.BlockSpec((1, tb, vd), lambda i, j: (i, j, col["v"] // vd)),
        pl.BlockSpec((1, tb, vd), lambda i, j: (i, j, col["g_b"] // vd)),
        pl.BlockSpec((1, tb, SMALL_W), lambda i, j: (i, j, 0)),
        const(3 * SMALL_W, kd), const(1, kd), const(1, hv), const(L, 3 * L),
    ]
    args = [proj3, proj3, proj3, proj3, small3, prm["gk_w3"], prm["gk_b"], prm["gla_norm_gain"], prm["tril3"]]
    if has_state:
        in_specs.append(pl.BlockSpec((1, H, hk, hv), lambda i, j: (i, 0, 0, 0)))
        args.append(gla_state)
    return pl.pallas_call(
        functools.partial(_gla_kernel, has_state, tb // L),
        grid=(b, t // tb),
        in_specs=in_specs,
        out_specs=[
            pl.BlockSpec((1, tb, vd), lambda i, j: (i, j, 0)),
            pl.BlockSpec((1, H, hk, hv), lambda i, j: (i, 0, 0, 0)),
        ],
        out_shape=[
            jax.ShapeDtypeStruct((b, t, vd), BF16),
            jax.ShapeDtypeStruct((b, H, hk, hv), F32),
        ],
        scratch_shapes=[
            pltpu.VMEM((L, kd), F32),
            pltpu.VMEM((L, kd), F32),
            pltpu.VMEM((L, kd), F32),
            pltpu.VMEM((H, L, L), F32),
        ] + [pltpu.VMEM((hv, hk), F32) for _ in range(H)],
        compiler_params=pltpu.CompilerParams(
            dimension_semantics=("arbitrary", "arbitrary"), vmem_limit_bytes=VMEM_LIMIT_BYTES),
    )(*args)


def _merge_post_kernel(tn, ys_ref, yg_ref, g_ref, x_ref, wbs_ref, wbg_ref, wo_ref, gn_ref, o_ref, mx_ref):
    d = o_ref.shape[1]
    for n in range(d // tn):
        ns = slice(n * tn, (n + 1) * tn)
        a = _dot(ys_ref[...], wbs_ref[:, ns])
        b = _dot(yg_ref[...], wbg_ref[:, ns])
        gate_s = _sigmoid_half(g_ref[:, ns].astype(F32))
        gate_g = _sigmoid_half(g_ref[:, d + n * tn:d + (n + 1) * tn].astype(F32))
        mx_ref[:, ns] = (gate_s * a + gate_g * b).astype(mx_ref.dtype)
    cc = _dot(mx_ref[...], wo_ref[...])
    ms = jnp.mean(cc * cc, axis=-1, keepdims=True)
    o_ref[...] = x_ref[...] + cc * lax.rsqrt(ms + NORM_EPS) * gn_ref[...]


def _outproj(y_ssd, y_gla, proj, col, x2, prm, tm, tn):
    m, d = x2.shape
    d_inner = y_ssd.shape[1]
    vd = y_gla.shape[1]
    g0 = col["merge"] // (2 * d)
    resident = lambda r, c_: pl.BlockSpec((r, c_), lambda i: (0, 0), pipeline_mode=pl.Buffered(1))
    return pl.pallas_call(
        functools.partial(_merge_post_kernel, tn),
        grid=(m // tm,),
        in_specs=[
            pl.BlockSpec((tm, d_inner), lambda i: (i, 0)),
            pl.BlockSpec((tm, vd), lambda i: (i, 0)),
            pl.BlockSpec((tm, 2 * d), lambda i: (i, g0)),
            pl.BlockSpec((tm, d), lambda i: (i, 0)),
            resident(d_inner, d), resident(vd, d), resident(d, d), resident(1, d),
        ],
        out_specs=pl.BlockSpec((tm, d), lambda i: (i, 0)),
        out_shape=jax.ShapeDtypeStruct((m, d), F32),
        scratch_shapes=[pltpu.VMEM((tm, d), BF16)],
        compiler_params=pltpu.CompilerParams(
            dimension_semantics=("arbitrary",), vmem_limit_bytes=VMEM_LIMIT_BYTES),
    )(y_ssd, y_gla, proj, x2, prm["w_branch_ssd"], prm["w_branch_gla"], prm["w_out"], prm["norm_post_gain"])


def _regroup_kernel(tn, offsets, blk_ref, cls_ref, half_ref, a_ref, b_ref, o_ref):
    j = pl.program_id(0)
    scale = jnp.where(half_ref[j] == 1, 0.5, 1.0).astype(F32)
    for ci, off in enumerate(offsets):
        @pl.when(cls_ref[j] == ci)
        def _():
            x = a_ref[...] if off == 0 else jnp.concatenate([a_ref[off:tn, :], b_ref[0:off, :]], axis=0)
            o_ref[...] = (x.T * scale).astype(o_ref.dtype)


def _pick_rows_kernel(spans, *refs):
    for (start, size), src, dst in zip(spans, refs[:len(spans)], refs[len(spans):]):
        dst[...] = src[start % LANES:start % LANES + size, :]


def _pick_rows(a, *spans):
    d = a.shape[1]
    assert all(start % LANES + size <= LANES and size % SUBLANES == 0 for start, size in spans)
    return pl.pallas_call(
        functools.partial(_pick_rows_kernel, spans),
        grid=(1,),
        in_specs=[pl.BlockSpec((LANES, d), functools.partial(lambda blk, i: (blk, 0), start // LANES))
                  for start, _ in spans],
        out_specs=[pl.BlockSpec((size, d), lambda i: (0, 0)) for _, size in spans],
        out_shape=[jax.ShapeDtypeStruct((size, d), a.dtype) for _, size in spans],
    )(*([a] * len(spans)))


def _regroup_w_in(w_in_t, segments, tn):
    d = w_in_t.shape[1]
    offsets = sorted({src % tn for src, _, _ in segments})
    blk, cls, half = [], [], []
    for src, width, halve in segments:
        assert width % tn == 0 and src % tn < LANES and (src % tn) % SUBLANES == 0
        for i in range(width // tn):
            blk.append(src // tn + i)
            cls.append(offsets.index(src % tn))
            half.append(int(halve))
    n_blocks = len(blk)
    to_i32 = lambda v: jnp.asarray(np.array(v, np.int32))
    return pl.pallas_call(
        functools.partial(_regroup_kernel, tn, tuple(offsets)),
        grid_spec=pltpu.PrefetchScalarGridSpec(
            num_scalar_prefetch=3,
            grid=(n_blocks,),
            in_specs=[
                pl.BlockSpec((tn, d), lambda j, blk, cls, half: (blk[j], 0)),
                pl.BlockSpec((LANES, d), lambda j, blk, cls, half: ((blk[j] + 1) * (tn // LANES), 0)),
            ],
            out_specs=pl.BlockSpec((d, tn), lambda j, blk, cls, half: (0, j)),
        ),
        out_shape=jax.ShapeDtypeStruct((d, n_blocks * tn), BF16),
        compiler_params=pltpu.CompilerParams(
            dimension_semantics=("arbitrary",), vmem_limit_bytes=VMEM_LIMIT_BYTES),
    )(to_i32(blk), to_i32(cls), to_i32(half), w_in_t, w_in_t)
def _prepare(norm_pre_gain, w_in, conv_w, conv_b, dt_bias, a_log, d_skip, ssd_norm_gain, gla_gk_w, gla_gk_b,
             gla_norm_gain, w_branch_ssd, w_branch_gla, w_out, norm_post_gain):
    d = w_in.shape[0]
    n_heads = dt_bias.shape[0]
    d_inner = n_heads * SSD_HEAD_DIM
    bcw = 2 * SSD_GROUPS * SSD_D_STATE
    kd = gla_gk_w.shape[1]
    vd = w_branch_gla.shape[0]
    assert DT_COPIES * n_heads + GLA_GATE_RANK <= SMALL_W
    sizes = (d_inner, d_inner + bcw, n_heads, kd, kd, vd, vd, GLA_GATE_RANK, 2 * d)
    offs = np.concatenate([[0], np.cumsum(sizes)])
    z0, xbc0, dt0, q0, k0, v0, gb0, gkl0, mg0 = [int(o_) for o_ in offs[:9]]
    perm = np.concatenate([np.arange(0, n_heads, 2), np.arange(1, n_heads, 2)])
    w_in_t = w_in.T
    dt_rows, gk_rows = _pick_rows(w_in_t, (dt0, n_heads), (gkl0, GLA_GATE_RANK))
    dt_w = dt_rows[perm]
    gk0 = DT_COPIES * n_heads
    pad = SMALL_W - gk0 - GLA_GATE_RANK
    w_small = jnp.concatenate(
        [dt_w] * DT_COPIES + [gk_rows, jnp.zeros((pad, d), w_in.dtype)], axis=0).T.astype(BF16)
    half = 0.5
    pieces = [("xs", xbc0, d_inner, False), ("z", z0, d_inner, True), ("merge", mg0, 2 * d, True),
              ("bc", xbc0 + d_inner, bcw, False), ("v", v0, vd, False), ("g_b", gb0, vd, True),
              ("q", q0, kd, False), ("k", k0, kd, False)]
    col, o = {}, 0
    for name, _, width, _ in pieces:
        assert o % width == 0, name
        col[name] = o
        o += width
    w_main = _regroup_w_in(w_in_t, [p[1:] for p in pieces], REGROUP_TN)

    def small_row(v, fill):
        return jnp.concatenate([v[perm]] * DT_COPIES + [jnp.full((SMALL_W - gk0,), fill, v.dtype)])[None, :]

    expand = np.zeros((DT_COPIES * n_heads, d_inner), np.float32)
    for r in range(DT_COPIES):
        for c_ in range(n_heads):
            expand[r * n_heads + c_, perm[c_] * SSD_HEAD_DIM:(perm[c_] + 1) * SSD_HEAD_DIM] = 1.0
    tril = np.tril(np.ones((CHUNK, CHUNK), np.float32))
    rb = 2 * SUBLANES
    shift = np.concatenate([np.eye(CHUNK, k=-k, dtype=np.float32)[b * rb:(b + 1) * rb]
                            for b in range(CHUNK // rb) for k in range(1, SSD_CONV_W)], axis=0)
    gk_w_pad = jnp.zeros((SMALL_W, kd), F32).at[gk0:gk0 + GLA_GATE_RANK].set(gla_gk_w)
    gk_hi = gk_w_pad.astype(BF16)
    gk_lo = (gk_w_pad - gk_hi.astype(F32)).astype(BF16)
    rows = lambda a: jnp.broadcast_to(a, a.shape[:-2] + (2 * SUBLANES, a.shape[-1]))
    prm = dict(
        n_heads=n_heads, gla_key_dim=kd, gla_val_dim=vd,
        norm_pre_gain=norm_pre_gain[None, :], w_main=w_main, w_small=w_small,
        conv_w_x=rows(half * conv_w[:, None, :d_inner]), conv_w_bc=rows(half * conv_w[:, None, d_inner:]),
        conv_b_x=rows(half * conv_b[None, :d_inner]), conv_b_bc=rows(half * conv_b[None, d_inner:]),
        dt_bias3=small_row(dt_bias, 0.0), a_log3=small_row(a_log, 0.0),
        d_skip_e=jnp.repeat(d_skip, SSD_HEAD_DIM)[None, :], ssd_norm_gain=ssd_norm_gain[None, :],
        shift=jnp.asarray(shift, BF16), expand3=jnp.asarray(expand, BF16), tril3=jnp.asarray(np.concatenate([tril] * 3, axis=1), BF16),
        gk_w3=jnp.concatenate([gk_hi, gk_lo, gk_hi], axis=0), gk_b=gla_gk_b[None, :],
        gla_norm_gain=gla_norm_gain[None, :],
        w_branch_ssd=w_branch_ssd.astype(BF16), w_branch_gla=w_branch_gla.astype(BF16),
        w_out=w_out.astype(BF16), norm_post_gain=norm_post_gain[None, :],
    )
    return prm, col


def _time_block(t):
    tb = min(t, CHUNKS_PER_STEP * CHUNK)
    assert t % tb == 0
    return tb


def _row_tile(m, want):
    tm = min(m, want)
    assert m % tm == 0
    return tm


def _layer(x, conv_state, ssd_state, gla_state, prm, col):
    b, t, d = x.shape
    assert t % CHUNK == 0 and t >= SSD_CONV_W - 1
    m = b * t
    x2 = x.reshape(m, d)
    proj, small = _inproj(x2, prm["norm_pre_gain"], prm["w_main"], prm["w_small"], _row_tile(m, INPROJ_TM), INPROJ_TN)
    proj3 = proj.reshape(b, t, -1)
    small3 = small.reshape(b, t, -1)
    d_inner = prm["n_heads"] * SSD_HEAD_DIM
    bcw = 2 * SSD_GROUPS * SSD_D_STATE
    nc = SSD_CONV_W - 1
    new_conv = jnp.concatenate(
        [proj3[:, t - nc:, col["xs"]:col["xs"] + d_inner], proj3[:, t - nc:, col["bc"]:col["bc"] + bcw]],
        axis=-1).astype(x.dtype)
    y_ssd, new_ssd = _ssd(proj3, small3, col, prm, conv_state, ssd_state)
    y_gla, new_gla = _gla(proj3, small3, col, prm, gla_state)
    y = _outproj(y_ssd.reshape(m, -1), y_gla.reshape(m, -1), proj, col, x2, prm, _row_tile(m, MERGE_TM), MERGE_TN)
    return y.reshape(b, t, d), new_conv, new_ssd, new_gla


def kernel(x_prompt, x_sample, state_conv_ssd, state_ssd, state_gla, norm_pre_gain, w_in, conv_w, conv_b, dt_bias,
           a_log, d_skip, ssd_norm_gain, gla_gk_w, gla_gk_b, gla_norm_gain, w_branch_ssd, w_branch_gla, w_out,
           norm_post_gain):
    depth = w_in.shape[0]
    yp, ys = x_prompt, x_sample
    outs = [[] for _ in range(6)]
    for layer in range(depth):
        prm, col = _prepare(norm_pre_gain[layer], w_in[layer], conv_w[layer], conv_b[layer], dt_bias[layer],
                            a_log[layer], d_skip[layer], ssd_norm_gain[layer], gla_gk_w[layer], gla_gk_b[layer],
                            gla_norm_gain[layer], w_branch_ssd[layer], w_branch_gla[layer], w_out[layer],
                            norm_post_gain[layer])
        yp, c_p, s_p, g_p = _layer(yp, None, None, None, prm, col)
        ys, c_s, s_s, g_s = _layer(ys, state_conv_ssd[layer], state_ssd[layer], state_gla[layer], prm, col)
        for lst, val in zip(outs, (c_p, s_p, g_p, c_s, s_s, g_s)):
            lst.append(val)
    return (yp, ys) + tuple(jnp.stack(o) for o in outs)
```
